```python
import jax, jax.numpy as jnp
from jax import lax
import numpy as np

D_MODEL = 1024
BATCH = 4
SEQ = 4096
DEPTH = 2

N_MIXERS = 2
HEAD_DIM = 64
N_Q_HEADS = D_MODEL // HEAD_DIM
N_KV_HEADS = N_Q_HEADS // 4
GQA_GROUP = N_Q_HEADS // N_KV_HEADS
WINDOW = 128
BLOCK = 128
ROT_DIM = HEAD_DIM // 4
ROPE_THETA = 500000.0
LRU_WIDTH = D_MODEL
LRU_BLOCKS = 4
LRU_BLOCK_W = LRU_WIDTH // LRU_BLOCKS
CONV_WIDTH = 4
CONV_LEFT = 2
LRU_C = 8.0
N_EXPERTS = 16
EXPERT_FF = D_MODEL
CAPACITY_FACTOR = 2
LN_EPS = 1e-5
ALPHA = (2 * DEPTH) ** 0.25
BETA = (8 * DEPTH) ** -0.25
N_ATTN_LAYERS = (DEPTH + 1) // 2
N_LRU_LAYERS = DEPTH // 2

kernel_name = 'hybrid_swa_sink_rglru_ecmoe_deepnorm'


def layer_norm(x, g, b):
    xf = x.astype(jnp.float32)
    mu = jnp.mean(xf, axis=-1, keepdims=True)
    var = jnp.mean(jnp.square(xf - mu), axis=-1, keepdims=True)
    y = (xf - mu) * lax.rsqrt(var + LN_EPS) * g.astype(jnp.float32) + b.astype(jnp.float32)
    return y.astype(x.dtype)


def partial_rotary(t, cos, sin):
    tr, tp = t[..., :ROT_DIM], t[..., ROT_DIM:]
    t1, t2 = tr[..., :ROT_DIM // 2], tr[..., ROT_DIM // 2:]
    rot = jnp.concatenate([-t2, t1], axis=-1)
    return jnp.concatenate([tr * cos + rot * sin, tp], axis=-1)


def windowed_gqa_sink(x, w_qkv, w_o, sink):
    B, S, _ = x.shape
    nb = S // BLOCK
    qkv = x @ w_qkv
    q, k, v = jnp.split(qkv, [N_Q_HEADS * HEAD_DIM, (N_Q_HEADS + N_KV_HEADS) * HEAD_DIM], axis=-1)
    q = q.reshape(B, S, N_KV_HEADS, GQA_GROUP, HEAD_DIM)
    k = k.reshape(B, S, N_KV_HEADS, HEAD_DIM)
    v = v.reshape(B, S, N_KV_HEADS, HEAD_DIM)
    pos = jnp.arange(S, dtype=jnp.float32)
    inv_freq = ROPE_THETA ** (-jnp.arange(0, ROT_DIM, 2, dtype=jnp.float32) / ROT_DIM)
    ang = pos[:, None] * inv_freq[None, :]
    ang = jnp.concatenate([ang, ang], axis=-1)
    cos, sin = jnp.cos(ang).astype(x.dtype), jnp.sin(ang).astype(x.dtype)
    q = partial_rotary(q, cos[None, :, None, None, :], sin[None, :, None, None, :])
    k = partial_rotary(k, cos[None, :, None, :], sin[None, :, None, :])

    pad = ((0, 0), (BLOCK, BLOCK), (0, 0), (0, 0))
    kb = jnp.pad(k, pad).reshape(B, nb + 2, BLOCK, N_KV_HEADS, HEAD_DIM)
    vb = jnp.pad(v, pad).reshape(B, nb + 2, BLOCK, N_KV_HEADS, HEAD_DIM)
    k_band = jnp.concatenate([kb[:, :-2], kb[:, 1:-1], kb[:, 2:]], axis=2)
    v_band = jnp.concatenate([vb[:, :-2], vb[:, 1:-1], vb[:, 2:]], axis=2)
    qb = q.reshape(B, nb, BLOCK, N_KV_HEADS, GQA_GROUP, HEAD_DIM)

    s = jnp.einsum('bnqhgd,bnkhd->bnhgqk', qb, k_band).astype(jnp.float32) * (HEAD_DIM ** -0.5)
    blk = jnp.arange(nb)
    qpos = blk[:, None] * BLOCK + jnp.arange(BLOCK)[None, :]
    kpos = (blk[:, None] - 1) * BLOCK + jnp.arange(3 * BLOCK)[None, :]
    valid = ((jnp.abs(qpos[:, :, None] - kpos[:, None, :]) <= WINDOW)
             & (kpos[:, None, :] >= 0) & (kpos[:, None, :] < S))
    s = jnp.where(valid[None, :, None, None], s, -jnp.inf)
    sink_l = sink.astype(jnp.float32).reshape(N_KV_HEADS, GQA_GROUP)[None, None, :, :, None, None]
    m = jnp.maximum(jnp.max(s, axis=-1, keepdims=True), sink_l)
    p = jnp.exp(s - m)
    denom = jnp.sum(p, axis=-1, keepdims=True) + jnp.exp(sink_l - m)
    probs = (p / denom).astype(x.dtype)
    o = jnp.einsum('bnhgqk,bnkhd->bnqhgd', probs, v_band).reshape(B, S, D_MODEL)
    return o @ w_o


def _linear_combine(left, right):
    a1, b1 = left
    a2, b2 = right
    return a1 * a2, a2 * b1 + b2


def bidir_rglru_block(x, w_in, conv_w, conv_b, w_rgate, b_rgate, w_igate, b_igate, lam, w_out):
    B, S, _ = x.shape
    gate_br, xb = jnp.split(x @ w_in, 2, axis=-1)
    xp = jnp.pad(xb, ((0, 0), (CONV_LEFT, CONV_WIDTH - 1 - CONV_LEFT), (0, 0)))
    xc = conv_b
    for j in range(CONV_WIDTH):
        xc = xc + conv_w[j] * xp[:, j:j + S]
    xr = xc.reshape(B, S, LRU_BLOCKS, LRU_BLOCK_W)
    r = jax.nn.sigmoid(jnp.einsum('bsnc,zncf->zbsnf', xr, w_rgate).reshape(2, B, S, LRU_WIDTH).astype(jnp.float32)
                       + b_rgate.astype(jnp.float32)[:, None, None, :])
    i = jax.nn.sigmoid(jnp.einsum('bsnc,zncf->zbsnf', xr, w_igate).reshape(2, B, S, LRU_WIDTH).astype(jnp.float32)
                       + b_igate.astype(jnp.float32)[:, None, None, :])
    log_a = -LRU_C * r * jax.nn.softplus(-lam.astype(jnp.float32))[:, None, None, :]
    a = jnp.exp(log_a)
    mult = jnp.sqrt(-jnp.expm1(2.0 * log_a))
    t = jnp.arange(S)
    is_start = jnp.stack([t == 0, t == S - 1])
    mult = jnp.where(is_start[:, None, :, None], 1.0, mult)
    u = mult * i * xc.astype(jnp.float32)[None]
    _, h_f = lax.associative_scan(_linear_combine, (a[0], u[0]), axis=1)
    _, h_b = lax.associative_scan(_linear_combine, (a[1], u[1]), reverse=True, axis=1)
    h = (h_f + h_b).astype(x.dtype)
    return (h * jax.nn.gelu(gate_br)) @ w_out


def expert_choice_moe(x, w_router, w_gate_up, w_down):
    B, S, _ = x.shape
    cap = CAPACITY_FACTOR * S // N_EXPERTS
    aff = jax.nn.softmax((x @ w_router).astype(jnp.float32), axis=-1)
    gates, idx = lax.top_k(jnp.swapaxes(aff, 1, 2), cap)
    bi = jnp.arange(B)[:, None, None]
    xs = x[bi, idx]
    g, u = jnp.split(jnp.einsum('becd,edf->becf', xs, w_gate_up), 2, axis=-1)
    out = jnp.einsum('becf,efd->becd', jax.nn.silu(g) * u, w_down) * gates[..., None].astype(x.dtype)
    return jnp.zeros_like(x).at[bi, idx].add(out)


def setup_inputs(seed: int = 0) -> dict:
    key = jax.random.key(seed)
    ks = jax.random.split(key, 24)
    f32 = jnp.float32
    nrm = lambda k, shape, scale: jax.random.normal(k, shape, f32) * scale
    qkv_out = (N_Q_HEADS + 2 * N_KV_HEADS) * HEAD_DIM
    a8 = jax.random.uniform(ks[11], (N_LRU_LAYERS, 2, LRU_WIDTH), f32, 0.9, 0.999)
    a_base = a8 ** (1.0 / LRU_C)
    lam = jnp.log(a_base) - jnp.log1p(-a_base)
    return {
        'x': jax.random.normal(ks[0], (BATCH, SEQ, D_MODEL), f32),
        'attn_w_qkv': nrm(ks[1], (N_ATTN_LAYERS, D_MODEL, qkv_out), D_MODEL ** -0.5),
        'attn_w_o': nrm(ks[2], (N_ATTN_LAYERS, D_MODEL, D_MODEL), BETA * D_MODEL ** -0.5),
        'attn_sink': nrm(ks[3], (N_ATTN_LAYERS, N_Q_HEADS), 0.5),
        'lru_w_in': nrm(ks[4], (N_LRU_LAYERS, D_MODEL, 2 * LRU_WIDTH), D_MODEL ** -0.5),
        'lru_conv_w': nrm(ks[5], (N_LRU_LAYERS, CONV_WIDTH, LRU_WIDTH), CONV_WIDTH ** -0.5),
        'lru_conv_b': nrm(ks[6], (N_LRU_LAYERS, LRU_WIDTH), 0.02),
        'lru_w_rgate': nrm(ks[7], (N_LRU_LAYERS, 2, LRU_BLOCKS, LRU_BLOCK_W, LRU_BLOCK_W), LRU_BLOCK_W ** -0.5),
        'lru_b_rgate': nrm(ks[8], (N_LRU_LAYERS, 2, LRU_WIDTH), 0.02),
        'lru_w_igate': nrm(ks[9], (N_LRU_LAYERS, 2, LRU_BLOCKS, LRU_BLOCK_W, LRU_BLOCK_W), LRU_BLOCK_W ** -0.5),
        'lru_b_igate': nrm(ks[10], (N_LRU_LAYERS, 2, LRU_WIDTH), 0.02),
        'lru_lambda': lam,
        'lru_w_out': nrm(ks[12], (N_LRU_LAYERS, LRU_WIDTH, D_MODEL), BETA * LRU_WIDTH ** -0.5),
        'moe_w_router': nrm(ks[13], (DEPTH, D_MODEL, N_EXPERTS), D_MODEL ** -0.5),
        'moe_w_gate_up': nrm(ks[14], (DEPTH, N_EXPERTS, D_MODEL, 2 * EXPERT_FF), D_MODEL ** -0.5),
        'moe_w_down': nrm(ks[15], (DEPTH, N_EXPERTS, EXPERT_FF, D_MODEL), BETA * EXPERT_FF ** -0.5),
        'ln_mix_g': 1.0 + nrm(ks[16], (DEPTH, D_MODEL), 0.02),
        'ln_mix_b': nrm(ks[17], (DEPTH, D_MODEL), 0.02),
        'ln_ffn_g': 1.0 + nrm(ks[18], (DEPTH, D_MODEL), 0.02),
        'ln_ffn_b': nrm(ks[19], (DEPTH, D_MODEL), 0.02),
    }


def reference(x, attn_w_qkv, attn_w_o, attn_sink, lru_w_in, lru_conv_w, lru_conv_b,
              lru_w_rgate, lru_b_rgate, lru_w_igate, lru_b_igate, lru_lambda, lru_w_out,
              moe_w_router, moe_w_gate_up, moe_w_down, ln_mix_g, ln_mix_b, ln_ffn_g, ln_ffn_b):
    for layer in range(DEPTH):
        j = layer // N_MIXERS
        if layer % N_MIXERS == 0:
            mix = windowed_gqa_sink(x, attn_w_qkv[j], attn_w_o[j], attn_sink[j])
        else:
            mix = bidir_rglru_block(x, lru_w_in[j], lru_conv_w[j], lru_conv_b[j],
                                    lru_w_rgate[j], lru_b_rgate[j], lru_w_igate[j], lru_b_igate[j],
                                    lru_lambda[j], lru_w_out[j])
        x = layer_norm(ALPHA * x + mix, ln_mix_g[layer], ln_mix_b[layer])
        ffn = expert_choice_moe(x, moe_w_router[layer], moe_w_gate_up[layer], moe_w_down[layer])
        x = layer_norm(ALPHA * x + ffn, ln_ffn_g[layer], ln_ffn_b[layer])
    return x
```

```python
import functools

import jax
import jax.numpy as jnp
from jax import lax
from jax.experimental import pallas as pl
from jax.experimental.pallas import tpu as pltpu

F32 = jnp.float32
BF16 = jnp.bfloat16

D_MODEL = 1024
HEAD_DIM = 64
N_Q_HEADS = 16
N_KV_HEADS = 4
GQA_GROUP = 4
WINDOW = 128
BLOCK = 128
ROT_DIM = 16
ROPE_THETA = 500000.0
LRU_BLOCKS = 4
LRU_BLOCK_W = 256
CONV_WIDTH = 4
LRU_C = 8.0
N_EXPERTS = 16
EXPERT_FF = 1024
CAPACITY_FACTOR = 2
LN_EPS = 1e-5
DEPTH = 2
ALPHA = (2 * DEPTH) ** 0.25

LANES = 128
SUBLANES = 8
TOK_ROWS = D_MODEL // LANES
VMEM_LIMIT_MOE = 61 * 1024 * 1024


def _layer_norm(v, g, b):
    mu = jnp.mean(v, axis=-1, keepdims=True)
    vc = v - mu
    var = jnp.mean(vc * vc, axis=-1, keepdims=True)
    return vc * lax.rsqrt(var + LN_EPS) * g + b


def _router_affinity_t(xn, wr_t):
    logits_t = lax.dot_general(wr_t, xn.astype(BF16), (((1,), (1,)), ((), ())),
                               preferred_element_type=F32)
    m = jnp.max(logits_t, axis=0, keepdims=True)
    p = jnp.exp(logits_t - m)
    return p / jnp.sum(p, axis=0, keepdims=True)


def _qkv_kernel(x_ref, w_ref, cos_ref, sin_ref, q_ref, k_ref, v_ref):
    x = x_ref[...].astype(BF16)
    acc = jnp.dot(x, w_ref[...], preferred_element_type=F32)
    cosb = cos_ref[...]
    sinb = sin_ref[...]
    lane = lax.broadcasted_iota(jnp.int32, cosb.shape, 1)
    low = (lane % HEAD_DIM) < (ROT_DIM // 2)
    nq = N_Q_HEADS * HEAD_DIM
    nk = N_KV_HEADS * HEAD_DIM

    def rot(t):
        up = pltpu.roll(t, LANES - ROT_DIM // 2, axis=1)
        dn = pltpu.roll(t, ROT_DIM // 2, axis=1)
        return t * cosb + jnp.where(low, up, dn) * sinb

    qs = [rot(acc[:, c * LANES:(c + 1) * LANES]) * (HEAD_DIM ** -0.5) for c in range(nq // LANES)]
    q_ref[...] = jnp.concatenate(qs, axis=1).astype(BF16)
    ks = [rot(acc[:, nq + c * LANES:nq + (c + 1) * LANES]) for c in range(nk // LANES)]
    k_ref[...] = jnp.concatenate(ks, axis=1).astype(BF16)
    v_ref[...] = acc[:, nq + nk:].astype(BF16)


def _qkv_proj(x2d, w_qkv, cos_t, sin_t, seq):
    T = x2d.shape[0]
    tm = min(512, seq)
    nq = N_Q_HEADS * HEAD_DIM
    nk = N_KV_HEADS * HEAD_DIM
    nts = seq // tm
    return pl.pallas_call(
        _qkv_kernel,
        grid=(T // tm,),
        in_specs=[
            pl.BlockSpec((tm, D_MODEL), lambda i: (i, 0)),
            pl.BlockSpec((D_MODEL, nq + 2 * nk), lambda i: (0, 0)),
            pl.BlockSpec((tm, LANES), lambda i: (i % nts, 0)),
            pl.BlockSpec((tm, LANES), lambda i: (i % nts, 0)),
        ],
        out_specs=[
            pl.BlockSpec((tm, nq), lambda i: (i, 0)),
            pl.BlockSpec((tm, nk), lambda i: (i, 0)),
            pl.BlockSpec((tm, nk), lambda i: (i, 0)),
        ],
        out_shape=[
            jax.ShapeDtypeStruct((T, nq), BF16),
            jax.ShapeDtypeStruct((T, nk), BF16),
            jax.ShapeDtypeStruct((T, nk), BF16),
        ],
        compiler_params=pltpu.CompilerParams(dimension_semantics=("arbitrary",)),
        name="qkv_proj",
    )(x2d, w_qkv, cos_t, sin_t)


QBLKS = 4


def _attn_kernel(sink_ref, q_ref, kp_ref, kc_ref, kn_ref, vp_ref, vc_ref, vn_ref, o_ref, *, seq):
    i = pl.program_id(1)
    kall = jnp.concatenate([kp_ref[...], kc_ref[...], kn_ref[...]], axis=0)
    vall = jnp.concatenate([vp_ref[...], vc_ref[...], vn_ref[...]], axis=0)
    band = 3 * BLOCK
    r = lax.broadcasted_iota(jnp.int32, (BLOCK, band), 0)
    c = lax.broadcasted_iota(jnp.int32, (BLOCK, band), 1)
    in_window = (c >= r) & (c <= r + 2 * WINDOW)
    for j in range(QBLKS):
        kpos = c + (i * QBLKS + j - 1) * BLOCK
        valid = in_window & (kpos >= 0) & (kpos < seq)
        valid = jnp.concatenate([valid] * GQA_GROUP, axis=0)
        q_blk = q_ref[j * BLOCK:(j + 1) * BLOCK, :]
        outs = []
        for h in range(N_KV_HEADS):
            kb = kall[j * BLOCK:j * BLOCK + band, h * HEAD_DIM:(h + 1) * HEAD_DIM]
            vb = vall[j * BLOCK:j * BLOCK + band, h * HEAD_DIM:(h + 1) * HEAD_DIM]
            heads = [h * GQA_GROUP + g for g in range(GQA_GROUP)]
            qg = jnp.concatenate(
                [q_blk[:, hh * HEAD_DIM:(hh + 1) * HEAD_DIM] for hh in heads], axis=0)
            s = lax.dot_general(qg, kb, (((1,), (1,)), ((), ())), preferred_element_type=F32)
            s = jnp.where(valid, s, -jnp.inf)
            sink_col = jnp.concatenate(
                [jnp.full((BLOCK, 1), sink_ref[hh], F32) for hh in heads], axis=0)
            m = jnp.maximum(jnp.max(s, axis=1, keepdims=True), sink_col)
            p = jnp.exp(s - m)
            denom = jnp.sum(p, axis=1, keepdims=True) + jnp.exp(sink_col - m)
            o = jnp.dot(p.astype(BF16), vb, preferred_element_type=F32) / denom
            outs.extend([o[g * BLOCK:(g + 1) * BLOCK, :] for g in range(GQA_GROUP)])
        o_ref[j * BLOCK:(j + 1) * BLOCK, :] = jnp.concatenate(outs, axis=1).astype(BF16)


def _attention(q, k, v, sink, batch, seq):
    T = q.shape[0]
    qt = QBLKS * BLOCK
    nq = seq // qt
    nb = seq // BLOCK
    nk = N_KV_HEADS * HEAD_DIM
    cur = lambda b, i: (b * nq + i, 0)
    prev = lambda b, i: (b * nb + jnp.maximum(i * QBLKS - 1, 0), 0)
    nxt = lambda b, i: (b * nb + jnp.minimum(i * QBLKS + QBLKS, nb - 1), 0)
    return pl.pallas_call(
        functools.partial(_attn_kernel, seq=seq),
        grid=(batch, nq),
        in_specs=[
            pl.BlockSpec(memory_space=pltpu.SMEM),
            pl.BlockSpec((qt, D_MODEL), cur),
            pl.BlockSpec((BLOCK, nk), prev),
            pl.BlockSpec((qt, nk), cur),
            pl.BlockSpec((BLOCK, nk), nxt),
            pl.BlockSpec((BLOCK, nk), prev),
            pl.BlockSpec((qt, nk), cur),
            pl.BlockSpec((BLOCK, nk), nxt),
        ],
        out_specs=pl.BlockSpec((qt, D_MODEL), cur),
        out_shape=jax.ShapeDtypeStruct((T, D_MODEL), BF16),
        compiler_params=pltpu.CompilerParams(dimension_semantics=("arbitrary", "arbitrary")),
        name="swa_attention",
    )(sink, q, k, k, k, v, v, v)


def _proj_ln_kernel(a_ref, w_ref, x_ref, g_ref, b_ref, wr_ref, xo_ref, aff_ref):
    y = jnp.dot(a_ref[...], w_ref[...], preferred_element_type=F32)
    xn = _layer_norm(ALPHA * x_ref[...] + y, g_ref[...], b_ref[...])
    xo_ref[...] = xn
    aff_ref[0] = _router_affinity_t(xn, wr_ref[...])


def _proj_ln(a, w, x2d, g, b, wr_t, batch, seq):
    T = x2d.shape[0]
    tm = min(512, seq)
    nts = seq // tm
    return pl.pallas_call(
        _proj_ln_kernel,
        grid=(T // tm,),
        in_specs=[
            pl.BlockSpec((tm, D_MODEL), lambda i: (i, 0)),
            pl.BlockSpec((D_MODEL, D_MODEL), lambda i: (0, 0)),
            pl.BlockSpec((tm, D_MODEL), lambda i: (i, 0)),
            pl.BlockSpec((1, D_MODEL), lambda i: (0, 0)),
            pl.BlockSpec((1, D_MODEL), lambda i: (0, 0)),
            pl.BlockSpec((N_EXPERTS, D_MODEL), lambda i: (0, 0)),
        ],
        out_specs=[
            pl.BlockSpec((tm, D_MODEL), lambda i: (i, 0)),
            pl.BlockSpec((1, N_EXPERTS, tm), lambda i: (i // nts, 0, i % nts)),
        ],
        out_shape=[
            jax.ShapeDtypeStruct((T, D_MODEL), F32),
            jax.ShapeDtypeStruct((batch, N_EXPERTS, seq), F32),
        ],
        compiler_params=pltpu.CompilerParams(dimension_semantics=("arbitrary",)),
        name="proj_ln_router",
    )(a, w, x2d, g, b, wr_t)


def _lane_prefix(mask, tri, dst_ref):
    seq = mask.shape[1]
    off = jnp.zeros((mask.shape[0], 1), F32)
    mb = mask.astype(BF16)
    for c in range(seq // LANES):
        res = jnp.dot(mb[:, c * LANES:(c + 1) * LANES], tri, preferred_element_type=F32) + off
        dst_ref[:, c * LANES:(c + 1) * LANES] = res
        off = res[:, LANES - 1:LANES]


def _route_kernel(aff_ref, idx_ref, gate_ref, pos_ref, affv_ref, *, cap):
    aff = aff_ref[0]
    seq = aff.shape[1]
    bits = pltpu.bitcast(aff, jnp.int32)
    cur = jnp.zeros((N_EXPERTS, 1), jnp.int32)
    for bit in range(30, -1, -1):
        cand = cur | (1 << bit)
        cnt = jnp.sum((bits >= cand).astype(jnp.int32), axis=1, keepdims=True)
        cur = jnp.where(cnt >= cap, cand, cur)
    gt = bits > cur
    eq = bits == cur
    n_gt = jnp.sum(gt.astype(jnp.int32), axis=1, keepdims=True)
    ties_taken = (cap - n_gt).astype(F32)
    ri = lax.broadcasted_iota(jnp.int32, (LANES, LANES), 0)
    ci = lax.broadcasted_iota(jnp.int32, (LANES, LANES), 1)
    tri = (ri <= ci).astype(BF16)
    _lane_prefix(eq, tri, pos_ref)
    sel = gt | (eq & (pos_ref[...] <= ties_taken))
    _lane_prefix(sel, tri, pos_ref)
    pos_ref[...] = jnp.where(sel, pos_ref[...], 0.0)
    affv_ref[...] = aff
    tok = lax.broadcasted_iota(jnp.int32, (SUBLANES, seq), 1).astype(F32)
    lane_e = lax.broadcasted_iota(jnp.int32, (SUBLANES, N_EXPERTS), 1)

    for e in range(N_EXPERTS):
        posb = jnp.broadcast_to(pos_ref[e:e + 1, :], (SUBLANES, seq))
        affb = jnp.broadcast_to(affv_ref[e:e + 1, :], (SUBLANES, seq))

        def slot_group(g, carry):
            slot = (g * SUBLANES + lax.broadcasted_iota(jnp.int32, (SUBLANES, 1), 0) + 1).astype(F32)
            hit = posb == slot
            tok_col = jnp.sum(jnp.where(hit, tok, 0.0), axis=1, keepdims=True)
            gate_col = jnp.sum(jnp.where(hit, affb, 0.0), axis=1, keepdims=True)
            rows = pl.ds(pl.multiple_of(g * SUBLANES, SUBLANES), SUBLANES)
            old_idx = jnp.zeros_like(lane_e) if e == 0 else idx_ref[0, rows, :]
            old_gate = jnp.zeros(lane_e.shape, F32) if e == 0 else gate_ref[0, rows, :]
            idx_ref[0, rows, :] = jnp.where(lane_e == e, tok_col.astype(jnp.int32), old_idx)
            gate_ref[0, rows, :] = jnp.where(lane_e == e, gate_col, old_gate)
            return carry

        lax.fori_loop(0, cap // SUBLANES, slot_group, 0)


def _route(aff_t, cap):
    batch, _, seq = aff_t.shape
    return pl.pallas_call(
        functools.partial(_route_kernel, cap=cap),
        grid=(batch,),
        in_specs=[pl.BlockSpec((1, N_EXPERTS, seq), lambda b: (b, 0, 0))],
        out_specs=[
            pl.BlockSpec((1, cap, N_EXPERTS), lambda b: (b, 0, 0)),
            pl.BlockSpec((1, cap, N_EXPERTS), lambda b: (b, 0, 0)),
        ],
        out_shape=[
            jax.ShapeDtypeStruct((batch, cap, N_EXPERTS), jnp.int32),
            jax.ShapeDtypeStruct((batch, cap, N_EXPERTS), F32),
        ],
        scratch_shapes=[pltpu.VMEM((N_EXPERTS, seq), F32), pltpu.VMEM((N_EXPERTS, seq), F32)],
        compiler_params=pltpu.CompilerParams(dimension_semantics=("arbitrary",)),
        name="expert_choice_route",
    )(aff_t)


FF_CHUNKS = 2
LN_CHUNK = 256


def _moe_kernel(idx_ref, x1t_hbm, gate_ref, wg_ref, wu_ref, wd_ref, lng_ref, lnb_ref, out_hbm,
                x1v, acc, xs3, xs2d, yacc, stage, sem_in, sem_out, *, seq, cap):
    b = pl.program_id(0)
    e = pl.program_id(1)
    j = pl.program_id(2)
    last_j = FF_CHUNKS - 1
    tok_rows = seq * TOK_ROWS

    def slot_token(i):
        return idx_ref[(b * cap + i) * N_EXPERTS + e]

    def tok_slice(t):
        return pl.ds(pl.multiple_of(t * TOK_ROWS, TOK_ROWS), TOK_ROWS)

    @pl.when((e == 0) & (j == 0))
    def _load_sequence():
        cp = pltpu.make_async_copy(x1t_hbm.at[pl.ds(b * tok_rows, tok_rows)], x1v, sem_in)
        cp.start()
        acc[...] = jnp.zeros_like(acc)
        cp.wait()

    @pl.when(j == 0)
    def _gather():
        def body(i, carry):
            xs3[tok_slice(i), :] = x1v[tok_slice(slot_token(i)), :]
            return carry
        lax.fori_loop(0, cap, body, 0, unroll=8)
        xs2d[...] = jnp.concatenate(
            [xs3[pl.ds(s, cap, stride=TOK_ROWS), :] for s in range(TOK_ROWS)], axis=1).astype(BF16)

    xs = xs2d[...]
    g = jnp.dot(xs, wg_ref[0].astype(BF16), preferred_element_type=F32)
    u = jnp.dot(xs, wu_ref[0].astype(BF16), preferred_element_type=F32)
    act = (g * jax.nn.sigmoid(g) * u).astype(BF16)
    yp = jnp.dot(act, wd_ref[0].astype(BF16), preferred_element_type=F32)

    @pl.when(j == 0)
    def _first():
        yacc[...] = yp

    @pl.when(j > 0)
    def _rest():
        yacc[...] += yp

    @pl.when(j == last_j)
    def _scatter():
        gates = gate_ref[0]
        lane_e = lax.broadcasted_iota(jnp.int32, gates.shape, 1)
        gcol = jnp.sum(jnp.where(lane_e == e, gates, 0.0), axis=1, keepdims=True)
        y = yacc[...] * gcol
        for s in range(TOK_ROWS):
            xs3[pl.ds(s, cap, stride=TOK_ROWS), :] = y[:, s * LANES:(s + 1) * LANES]

        def body(gi, carry):
            toks = [slot_token(gi * SUBLANES + k) for k in range(SUBLANES)]
            vals = [acc[tok_slice(toks[k]), :] + xs3[tok_slice(gi * SUBLANES + k), :]
                    for k in range(SUBLANES)]
            for k in range(SUBLANES):
                acc[tok_slice(toks[k]), :] = vals[k]
            return carry
        lax.fori_loop(0, cap // SUBLANES, body, 0)

    @pl.when((e == N_EXPERTS - 1) & (j == last_j))
    def _finalize():
        n_chunks = seq // LN_CHUNK

        def out_copy(c, slot):
            return pltpu.make_async_copy(
                stage.at[slot], out_hbm.at[pl.ds(b * seq + c * LN_CHUNK, LN_CHUNK)], sem_out.at[slot])

        def body(c, carry):
            slot = c % 2
            base = c * (LN_CHUNK * TOK_ROWS)

            def cols(ref):
                return jnp.concatenate(
                    [ref[pl.ds(base + s, LN_CHUNK, stride=TOK_ROWS), :] for s in range(TOK_ROWS)], axis=1)

            res = _layer_norm(ALPHA * cols(x1v) + cols(acc), lng_ref[...], lnb_ref[...])

            @pl.when(c >= 2)
            def _():
                out_copy(c - 2, slot).wait()
            stage[slot] = res
            out_copy(c, slot).start()
            return carry
        lax.fori_loop(0, n_chunks, body, 0)
        out_copy(n_chunks - 2, (n_chunks - 2) % 2).wait()
        out_copy(n_chunks - 1, (n_chunks - 1) % 2).wait()


def _moe(x1_2d, idx_t, gate_t, w_gate_up, w_down, ln_g, ln_b, batch, seq):
    cap = idx_t.shape[1]
    T = x1_2d.shape[0]
    fc = EXPERT_FF // FF_CHUNKS
    x1t = x1_2d.reshape(T * TOK_ROWS, LANES)
    idx_flat = idx_t.reshape(-1)
    grid_spec = pltpu.PrefetchScalarGridSpec(
        num_scalar_prefetch=1,
        grid=(batch, N_EXPERTS, FF_CHUNKS),
        in_specs=[
            pl.BlockSpec(memory_space=pl.ANY),
            pl.BlockSpec((1, cap, N_EXPERTS), lambda b, e, j, idx: (b, 0, 0)),
            pl.BlockSpec((1, D_MODEL, fc), lambda b, e, j, idx: (e, 0, j)),
            pl.BlockSpec((1, D_MODEL, fc), lambda b, e, j, idx: (e, 0, FF_CHUNKS + j)),
            pl.BlockSpec((1, fc, D_MODEL), lambda b, e, j, idx: (e, j, 0)),
            pl.BlockSpec((1, D_MODEL), lambda b, e, j, idx: (0, 0)),
            pl.BlockSpec((1, D_MODEL), lambda b, e, j, idx: (0, 0)),
        ],
        out_specs=pl.BlockSpec(memory_space=pl.ANY),
        scratch_shapes=[
            pltpu.VMEM((seq * TOK_ROWS, LANES), F32),
            pltpu.VMEM((seq * TOK_ROWS, LANES), F32),
            pltpu.VMEM((cap * TOK_ROWS, LANES), F32),
            pltpu.VMEM((cap, D_MODEL), BF16),
            pltpu.VMEM((cap, D_MODEL), F32),
            pltpu.VMEM((2, LN_CHUNK, D_MODEL), F32),
            pltpu.SemaphoreType.DMA(()),
            pltpu.SemaphoreType.DMA((2,)),
        ],
    )
    return pl.pallas_call(
        functools.partial(_moe_kernel, seq=seq, cap=cap),
        grid_spec=grid_spec,
        out_shape=jax.ShapeDtypeStruct((T, D_MODEL), F32),
        compiler_params=pltpu.CompilerParams(
            dimension_semantics=("arbitrary", "arbitrary", "arbitrary"),
            vmem_limit_bytes=VMEM_LIMIT_MOE),
        name="expert_ffn_combine_ln",
    )(idx_flat, x1t, gate_t, w_gate_up, w_gate_up, w_down, ln_g, ln_b)


HALO = SUBLANES


def _gelu_tanh(x):
    return 0.5 * x * (1.0 + jnp.tanh(0.7978845608028654 * (x + 0.044715 * (x * x * x))))


def _lru_in_kernel(x_ref, xp_ref, xn_ref, w_ref, cw_ref, cb_ref, gg_ref, xc_ref, ext, *, nts):
    i = pl.program_id(0)
    tm = x_ref.shape[0]
    w = w_ref[...]
    acc = jnp.dot(x_ref[...].astype(BF16), w, preferred_element_type=F32)
    gg_ref[...] = _gelu_tanh(acc[:, :D_MODEL])
    wx = w[:, D_MODEL:]
    first = (i % nts) == 0
    last = (i % nts) == nts - 1
    xbp = jnp.dot(xp_ref[...].astype(BF16), wx, preferred_element_type=F32)
    xbn = jnp.dot(xn_ref[...].astype(BF16), wx, preferred_element_type=F32)
    ext[0:HALO, :] = jnp.where(first, 0.0, xbp)
    ext[HALO:HALO + tm, :] = acc[:, D_MODEL:]
    ext[HALO + tm:, :] = jnp.where(last, 0.0, xbn)
    xc = cb_ref[...]
    for tap in range(CONV_WIDTH):
        xc = xc + cw_ref[tap:tap + 1, :] * ext[HALO - 2 + tap:HALO - 2 + tap + tm, :]
    xc_ref[...] = xc


def _lru_in(x2d, w_in, conv_w, conv_b, seq):
    T = x2d.shape[0]
    tm = min(512, seq)
    nts = seq // tm
    hb = tm // HALO
    nblk = T // HALO
    return pl.pallas_call(
        functools.partial(_lru_in_kernel, nts=nts),
        grid=(T // tm,),
        in_specs=[
            pl.BlockSpec((tm, D_MODEL), lambda i: (i, 0)),
            pl.BlockSpec((HALO, D_MODEL), lambda i: (jnp.maximum(i * hb - 1, 0), 0)),
            pl.BlockSpec((HALO, D_MODEL), lambda i: (jnp.minimum((i + 1) * hb, nblk - 1), 0)),
            pl.BlockSpec((D_MODEL, 2 * D_MODEL), lambda i: (0, 0)),
            pl.BlockSpec((CONV_WIDTH, D_MODEL), lambda i: (0, 0)),
            pl.BlockSpec((1, D_MODEL), lambda i: (0, 0)),
        ],
        out_specs=[
            pl.BlockSpec((tm, D_MODEL), lambda i: (i, 0)),
            pl.BlockSpec((tm, D_MODEL), lambda i: (i, 0)),
        ],
        out_shape=[
            jax.ShapeDtypeStruct((T, D_MODEL), F32),
            jax.ShapeDtypeStruct((T, D_MODEL), F32),
        ],
        scratch_shapes=[pltpu.VMEM((tm + 2 * HALO, D_MODEL), F32)],
        compiler_params=pltpu.CompilerParams(dimension_semantics=("arbitrary",)),
        name="lru_in_conv",
    )(x2d, x2d, x2d, w_in, conv_w, conv_b)


def _lru_gates_scan(xc_ref, wg_ref, br_ref, bi_ref, lam_ref, carry, hbuf, abuf, *, reverse, seq):
    i = pl.program_id(1)
    tm = xc_ref.shape[0]
    nt = seq // tm
    tile = (nt - 1 - i) if reverse else i

    @pl.when(i == 0)
    def _():
        carry[...] = jnp.zeros_like(carry)

    xc = xc_ref[...]
    xcb = xc.astype(BF16)
    r_parts, i_parts = [], []
    for n in range(LRU_BLOCKS):
        res = jnp.dot(xcb[:, n * LRU_BLOCK_W:(n + 1) * LRU_BLOCK_W], wg_ref[n],
                      preferred_element_type=F32)
        r_parts.append(res[:, :LRU_BLOCK_W])
        i_parts.append(res[:, LRU_BLOCK_W:])
    r = jax.nn.sigmoid(jnp.concatenate(r_parts, axis=1) + br_ref[...])
    ig = jax.nn.sigmoid(jnp.concatenate(i_parts, axis=1) + bi_ref[...])
    log_a = -LRU_C * r * jax.nn.softplus(-lam_ref[...])
    a = jnp.exp(log_a)
    mult = jnp.sqrt(-jnp.tanh(log_a) * (1.0 + a * a))
    row = lax.broadcasted_iota(jnp.int32, (tm, 1), 0)
    tpos = tile * tm + row
    is_start = tpos == (seq - 1 if reverse else 0)
    mult = jnp.where(is_start, 1.0, mult)
    u = mult * ig * xc

    sub = row % SUBLANES
    for k in (1, 2, 4):
        if reverse:
            a_sh = pltpu.roll(a, tm - k, axis=0)
            u_sh = pltpu.roll(u, tm - k, axis=0)
            ok = sub < SUBLANES - k
        else:
            a_sh = pltpu.roll(a, k, axis=0)
            u_sh = pltpu.roll(u, k, axis=0)
            ok = sub >= k
        u = jnp.where(ok, a * u_sh + u, u)
        a = jnp.where(ok, a * a_sh, a)
    hbuf[...] = u
    abuf[...] = a

    n_groups = tm // SUBLANES
    edge = 0 if reverse else SUBLANES - 1

    def body(gi, c):
        g = (n_groups - 1 - gi) if reverse else gi
        rows = pl.ds(pl.multiple_of(g * SUBLANES, SUBLANES), SUBLANES)
        h = hbuf[rows, :] + abuf[rows, :] * c
        hbuf[rows, :] = h
        return jnp.broadcast_to(h[edge:edge + 1, :], h.shape)

    c0 = jnp.broadcast_to(carry[...], (SUBLANES, D_MODEL))
    c_end = lax.fori_loop(0, n_groups, body, c0)
    carry[...] = c_end[0:1, :]


def _lru_fwd_kernel(xc_ref, wg_ref, br_ref, bi_ref, lam_ref, h_ref, carry, hbuf, abuf, *, seq):
    _lru_gates_scan(xc_ref, wg_ref, br_ref, bi_ref, lam_ref, carry, hbuf, abuf, reverse=False, seq=seq)
    h_ref[...] = hbuf[...]


def _lru_bwd_out_kernel(xc_ref, wg_ref, br_ref, bi_ref, lam_ref, hf_ref, gg_ref, x_ref, wo_ref,
                        g_ref, b_ref, wr_ref, xo_ref, aff_ref, carry, hbuf, abuf, *, seq):
    _lru_gates_scan(xc_ref, wg_ref, br_ref, bi_ref, lam_ref, carry, hbuf, abuf, reverse=True, seq=seq)
    h = hf_ref[...] + hbuf[...]
    y = jnp.dot((h * gg_ref[...]).astype(BF16), wo_ref[...], preferred_element_type=F32)
    xn = _layer_norm(ALPHA * x_ref[...] + y, g_ref[...], b_ref[...])
    xo_ref[...] = xn
    aff_ref[0] = _router_affinity_t(xn, wr_ref[...])


LRU_TM = 256


def _lru_common_specs(tile_map):
    return [
        pl.BlockSpec((LRU_TM, D_MODEL), tile_map),
        pl.BlockSpec((LRU_BLOCKS, LRU_BLOCK_W, 2 * LRU_BLOCK_W), lambda b, i: (0, 0, 0)),
        pl.BlockSpec((1, D_MODEL), lambda b, i: (0, 0)),
        pl.BlockSpec((1, D_MODEL), lambda b, i: (0, 0)),
        pl.BlockSpec((1, D_MODEL), lambda b, i: (0, 0)),
    ]


def _lru_scratch():
    return [pltpu.VMEM((1, D_MODEL), F32), pltpu.VMEM((LRU_TM, D_MODEL), F32),
            pltpu.VMEM((LRU_TM, D_MODEL), F32)]


def _lru_forward(xc, wg, br, bi, lam, batch, seq):
    T = xc.shape[0]
    nt = seq // LRU_TM
    tile_map = lambda b, i: (b * nt + i, 0)
    return pl.pallas_call(
        functools.partial(_lru_fwd_kernel, seq=seq),
        grid=(batch, nt),
        in_specs=_lru_common_specs(tile_map),
        out_specs=pl.BlockSpec((LRU_TM, D_MODEL), tile_map),
        out_shape=jax.ShapeDtypeStruct((T, D_MODEL), F32),
        scratch_shapes=_lru_scratch(),
        compiler_params=pltpu.CompilerParams(dimension_semantics=("arbitrary", "arbitrary")),
        name="lru_forward_scan",
    )(xc, wg, br, bi, lam)


def _lru_backward_out(xc, wg, br, bi, lam, hf, gg, x2d, w_out, g, b, wr_t, batch, seq):
    T = xc.shape[0]
    nt = seq // LRU_TM
    tile_map = lambda b, i: (b * nt + nt - 1 - i, 0)
    row_spec = pl.BlockSpec((1, D_MODEL), lambda b, i: (0, 0))
    return pl.pallas_call(
        functools.partial(_lru_bwd_out_kernel, seq=seq),
        grid=(batch, nt),
        in_specs=_lru_common_specs(tile_map) + [
            pl.BlockSpec((LRU_TM, D_MODEL), tile_map),
            pl.BlockSpec((LRU_TM, D_MODEL), tile_map),
            pl.BlockSpec((LRU_TM, D_MODEL), tile_map),
            pl.BlockSpec((D_MODEL, D_MODEL), lambda b, i: (0, 0)),
            row_spec, row_spec,
            pl.BlockSpec((N_EXPERTS, D_MODEL), lambda b, i: (0, 0)),
        ],
        out_specs=[
            pl.BlockSpec((LRU_TM, D_MODEL), tile_map),
            pl.BlockSpec((1, N_EXPERTS, LRU_TM), lambda b, i: (b, 0, nt - 1 - i)),
        ],
        out_shape=[
            jax.ShapeDtypeStruct((T, D_MODEL), F32),
            jax.ShapeDtypeStruct((batch, N_EXPERTS, seq), F32),
        ],
        scratch_shapes=_lru_scratch(),
        compiler_params=pltpu.CompilerParams(dimension_semantics=("arbitrary", "arbitrary")),
        name="lru_backward_scan_out_ln_router",
    )(xc, wg, br, bi, lam, hf, gg, x2d, w_out, g, b, wr_t)


def _rotary_tables(seq):
    pos = jnp.arange(seq, dtype=F32)
    inv_freq = ROPE_THETA ** (-jnp.arange(0, ROT_DIM, 2, dtype=F32) / ROT_DIM)
    ang = pos[:, None] * inv_freq[None, :]
    cos, sin = jnp.cos(ang), jnp.sin(ang)
    pad = HEAD_DIM - ROT_DIM
    cos_h = jnp.concatenate([cos, cos, jnp.ones((seq, pad), F32)], axis=1)
    sin_h = jnp.concatenate([-sin, sin, jnp.zeros((seq, pad), F32)], axis=1)
    reps = LANES // HEAD_DIM
    return jnp.tile(cos_h, (1, reps)), jnp.tile(sin_h, (1, reps))


def _moe_block(x1, aff_t, w_gate_up, w_down, ln_g, ln_b, batch, seq):
    cap = CAPACITY_FACTOR * seq // N_EXPERTS
    idx_t, gate_t = _route(aff_t, cap)
    return _moe(x1, idx_t, gate_t, w_gate_up, w_down, ln_g.reshape(1, -1), ln_b.reshape(1, -1), batch, seq)


def kernel(x, attn_w_qkv, attn_w_o, attn_sink, lru_w_in, lru_conv_w, lru_conv_b, lru_w_rgate,
           lru_b_rgate, lru_w_igate, lru_b_igate, lru_lambda, lru_w_out, moe_w_router, moe_w_gate_up,
           moe_w_down, ln_mix_g, ln_mix_b, ln_ffn_g, ln_ffn_b):
    batch, seq, _ = x.shape
    x2d = x.reshape(batch * seq, D_MODEL)
    row = lambda v: v.reshape(1, -1)

    cos_t, sin_t = _rotary_tables(seq)
    q, k, v = _qkv_proj(x2d, attn_w_qkv[0].astype(BF16), cos_t, sin_t, seq)
    o = _attention(q, k, v, attn_sink[0], batch, seq)
    x1, aff_t = _proj_ln(o, attn_w_o[0].astype(BF16), x2d, row(ln_mix_g[0]), row(ln_mix_b[0]),
                         moe_w_router[0].T.astype(BF16), batch, seq)
    x2 = _moe_block(x1, aff_t, moe_w_gate_up[0], moe_w_down[0], ln_ffn_g[0], ln_ffn_b[0], batch, seq)

    gg, xc = _lru_in(x2, lru_w_in[0].astype(BF16), lru_conv_w[0], row(lru_conv_b[0]), seq)
    wg = jnp.concatenate([lru_w_rgate[0], lru_w_igate[0]], axis=-1).astype(BF16)
    hf = _lru_forward(xc, wg[0], row(lru_b_rgate[0, 0]), row(lru_b_igate[0, 0]), row(lru_lambda[0, 0]),
                      batch, seq)
    x3, aff_t = _lru_backward_out(xc, wg[1], row(lru_b_rgate[0, 1]), row(lru_b_igate[0, 1]),
                                  row(lru_lambda[0, 1]), hf, gg, x2, lru_w_out[0].astype(BF16),
                                  row(ln_mix_g[1]), row(ln_mix_b[1]), moe_w_router[1].T.astype(BF16),
                                  batch, seq)
    x4 = _moe_block(x3, aff_t, moe_w_gate_up[1], moe_w_down[1], ln_ffn_g[1], ln_ffn_b[1], batch, seq)
    return x4.reshape(batch, seq, D_MODEL)
```

```python
import functools

import jax
import jax.numpy as jnp
from jax import lax
from jax.experimental import pallas as pl
from jax.experimental.pallas import tpu as pltpu

F32 = jnp.float32
BF16 = jnp.bfloat16

D_MODEL = 1024
HEAD_DIM = 64
N_Q_HEADS = 16
N_KV_HEADS = 4
GQA_GROUP = 4
WINDOW = 128
BLOCK = 128
ROT_DIM = 16
ROPE_THETA = 500000.0
LRU_BLOCKS = 4
LRU_BLOCK_W = 256
CONV_WIDTH = 4
LRU_C = 8.0
N_EXPERTS = 16
EXPERT_FF = 1024
CAPACITY_FACTOR = 2
LN_EPS = 1e-5
DEPTH = 2
ALPHA = (2 * DEPTH) ** 0.25

LANES = 128
SUBLANES = 8
TOK_ROWS = D_MODEL // LANES
VMEM_LIMIT_MOE = 61 * 1024 * 1024


def _layer_norm(v, g, b):
    mu = jnp.mean(v, axis=-1, keepdims=True)
    vc = v - mu
    var = jnp.mean(vc * vc, axis=-1, keepdims=True)
    return vc * lax.rsqrt(var + LN_EPS) * g + b


def _store_token_layout(dst_ref, val):
    n = val.shape[0]
    for s in range(TOK_ROWS):
        dst_ref[pl.ds(s, n, stride=TOK_ROWS), :] = val[:, s * LANES:(s + 1) * LANES]


def _load_token_layout(src_ref, first_token, n):
    base = first_token * TOK_ROWS
    return jnp.concatenate(
        [src_ref[pl.ds(base + s, n, stride=TOK_ROWS), :] for s in range(TOK_ROWS)], axis=1)


def _router_affinity_t(xn, wr_t):
    logits_t = lax.dot_general(wr_t, xn.astype(BF16), (((1,), (1,)), ((), ())),
                               preferred_element_type=F32)
    m = jnp.max(logits_t, axis=0, keepdims=True)
    p = jnp.exp(logits_t - m)
    return p / jnp.sum(p, axis=0, keepdims=True)


def _qkv_kernel(x_ref, w_ref, cos_ref, sin_ref, q_ref, k_ref, v_ref):
    x = x_ref[...].astype(BF16)
    acc = jnp.dot(x, w_ref[...], preferred_element_type=F32)
    cosb = cos_ref[...]
    sinb = sin_ref[...]
    lane = lax.broadcasted_iota(jnp.int32, cosb.shape, 1)
    low = (lane % HEAD_DIM) < (ROT_DIM // 2)
    nq = N_Q_HEADS * HEAD_DIM
    nk = N_KV_HEADS * HEAD_DIM

    def rot(t):
        up = pltpu.roll(t, LANES - ROT_DIM // 2, axis=1)
        dn = pltpu.roll(t, ROT_DIM // 2, axis=1)
        return t * cosb + jnp.where(low, up, dn) * sinb

    qs = [rot(acc[:, c * LANES:(c + 1) * LANES]) * (HEAD_DIM ** -0.5) for c in range(nq // LANES)]
    q_ref[...] = jnp.concatenate(qs, axis=1).astype(BF16)
    ks = [rot(acc[:, nq + c * LANES:nq + (c + 1) * LANES]) for c in range(nk // LANES)]
    k_ref[...] = jnp.concatenate(ks, axis=1).astype(BF16)
    v_ref[...] = acc[:, nq + nk:].astype(BF16)


def _qkv_proj(x2d, w_qkv, cos_t, sin_t, seq):
    T = x2d.shape[0]
    tm = min(512, seq)
    nq = N_Q_HEADS * HEAD_DIM
    nk = N_KV_HEADS * HEAD_DIM
    nts = seq // tm
    return pl.pallas_call(
        _qkv_kernel,
        grid=(T // tm,),
        in_specs=[
            pl.BlockSpec((tm, D_MODEL), lambda i: (i, 0)),
            pl.BlockSpec((D_MODEL, nq + 2 * nk), lambda i: (0, 0)),
            pl.BlockSpec((tm, LANES), lambda i: (i % nts, 0)),
            pl.BlockSpec((tm, LANES), lambda i: (i % nts, 0)),
        ],
        out_specs=[
            pl.BlockSpec((tm, nq), lambda i: (i, 0)),
            pl.BlockSpec((tm, nk), lambda i: (i, 0)),
            pl.BlockSpec((tm, nk), lambda i: (i, 0)),
        ],
        out_shape=[
            jax.ShapeDtypeStruct((T, nq), BF16),
            jax.ShapeDtypeStruct((T, nk), BF16),
            jax.ShapeDtypeStruct((T, nk), BF16),
        ],
        compiler_params=pltpu.CompilerParams(dimension_semantics=("arbitrary",)),
        name="qkv_proj",
    )(x2d, w_qkv, cos_t, sin_t)


QBLKS = 4


def _attn_kernel(sink_ref, q_ref, kp_ref, kc_ref, kn_ref, vp_ref, vc_ref, vn_ref, o_ref, *, seq):
    i = pl.program_id(1)
    kall = jnp.concatenate([kp_ref[...], kc_ref[...], kn_ref[...]], axis=0)
    vall = jnp.concatenate([vp_ref[...], vc_ref[...], vn_ref[...]], axis=0)
    band = 3 * BLOCK
    r = lax.broadcasted_iota(jnp.int32, (BLOCK, band), 0)
    c = lax.broadcasted_iota(jnp.int32, (BLOCK, band), 1)
    in_window = (c >= r) & (c <= r + 2 * WINDOW)
    for j in range(QBLKS):
        kpos = c + (i * QBLKS + j - 1) * BLOCK
        valid = in_window & (kpos >= 0) & (kpos < seq)
        valid = jnp.concatenate([valid] * GQA_GROUP, axis=0)
        q_blk = q_ref[j * BLOCK:(j + 1) * BLOCK, :]
        outs = []
        for h in range(N_KV_HEADS):
            kb = kall[j * BLOCK:j * BLOCK + band, h * HEAD_DIM:(h + 1) * HEAD_DIM]
            vb = vall[j * BLOCK:j * BLOCK + band, h * HEAD_DIM:(h + 1) * HEAD_DIM]
            heads = [h * GQA_GROUP + g for g in range(GQA_GROUP)]
            qg = jnp.concatenate(
                [q_blk[:, hh * HEAD_DIM:(hh + 1) * HEAD_DIM] for hh in heads], axis=0)
            s = lax.dot_general(qg, kb, (((1,), (1,)), ((), ())), preferred_element_type=F32)
            s = jnp.where(valid, s, -jnp.inf)
            sink_col = jnp.concatenate(
                [jnp.full((BLOCK, 1), sink_ref[hh], F32) for hh in heads], axis=0)
            m = jnp.maximum(jnp.max(s, axis=1, keepdims=True), sink_col)
            p = jnp.exp(s - m)
            denom = jnp.sum(p, axis=1, keepdims=True) + jnp.exp(sink_col - m)
            o = jnp.dot(p.astype(BF16), vb, preferred_element_type=F32) / denom
            outs.extend([o[g * BLOCK:(g + 1) * BLOCK, :] for g in range(GQA_GROUP)])
        o_ref[j * BLOCK:(j + 1) * BLOCK, :] = jnp.concatenate(outs, axis=1).astype(BF16)


def _attention(q, k, v, sink, batch, seq):
    T = q.shape[0]
    qt = QBLKS * BLOCK
    nq = seq // qt
    nb = seq // BLOCK
    nk = N_KV_HEADS * HEAD_DIM
    cur = lambda b, i: (b * nq + i, 0)
    prev = lambda b, i: (b * nb + jnp.maximum(i * QBLKS - 1, 0), 0)
    nxt = lambda b, i: (b * nb + jnp.minimum(i * QBLKS + QBLKS, nb - 1), 0)
    return pl.pallas_call(
        functools.partial(_attn_kernel, seq=seq),
        grid=(batch, nq),
        in_specs=[
            pl.BlockSpec(memory_space=pltpu.SMEM),
            pl.BlockSpec((qt, D_MODEL), cur),
            pl.BlockSpec((BLOCK, nk), prev),
            pl.BlockSpec((qt, nk), cur),
            pl.BlockSpec((BLOCK, nk), nxt),
            pl.BlockSpec((BLOCK, nk), prev),
            pl.BlockSpec((qt, nk), cur),
            pl.BlockSpec((BLOCK, nk), nxt),
        ],
        out_specs=pl.BlockSpec((qt, D_MODEL), cur),
        out_shape=jax.ShapeDtypeStruct((T, D_MODEL), BF16),
        compiler_params=pltpu.CompilerParams(dimension_semantics=("arbitrary", "arbitrary")),
        name="swa_attention",
    )(sink, q, k, k, k, v, v, v)


def _proj_ln_kernel(a_ref, w_ref, x_ref, g_ref, b_ref, wr_ref, xo_ref, aff_ref):
    y = jnp.dot(a_ref[...], w_ref[...], preferred_element_type=F32)
    xn = _layer_norm(ALPHA * x_ref[...] + y, g_ref[...], b_ref[...])
    _store_token_layout(xo_ref, xn)
    aff_ref[0] = _router_affinity_t(xn, wr_ref[...])


def _proj_ln(a, w, x2d, g, b, wr_t, batch, seq):
    T = x2d.shape[0]
    tm = min(512, seq)
    nts = seq // tm
    return pl.pallas_call(
        _proj_ln_kernel,
        grid=(T // tm,),
        in_specs=[
            pl.BlockSpec((tm, D_MODEL), lambda i: (i, 0)),
            pl.BlockSpec((D_MODEL, D_MODEL), lambda i: (0, 0)),
            pl.BlockSpec((tm, D_MODEL), lambda i: (i, 0)),
            pl.BlockSpec((1, D_MODEL), lambda i: (0, 0)),
            pl.BlockSpec((1, D_MODEL), lambda i: (0, 0)),
            pl.BlockSpec((N_EXPERTS, D_MODEL), lambda i: (0, 0)),
        ],
        out_specs=[
            pl.BlockSpec((tm * TOK_ROWS, LANES), lambda i: (i, 0)),
            pl.BlockSpec((1, N_EXPERTS, tm), lambda i: (i // nts, 0, i % nts)),
        ],
        out_shape=[
            jax.ShapeDtypeStruct((T * TOK_ROWS, LANES), F32),
            jax.ShapeDtypeStruct((batch, N_EXPERTS, seq), F32),
        ],
        compiler_params=pltpu.CompilerParams(dimension_semantics=("arbitrary",)),
        name="proj_ln_router",
    )(a, w, x2d, g, b, wr_t)


SLOT_BLOCK = 64


def _lane_prefix(mask, tri, dst_ref):
    seq = mask.shape[1]
    off = jnp.zeros((mask.shape[0], 1), F32)
    mb = mask.astype(BF16)
    ends = []
    for c in range(seq // LANES):
        res = jnp.dot(mb[:, c * LANES:(c + 1) * LANES], tri, preferred_element_type=F32) + off
        dst_ref[:, c * LANES:(c + 1) * LANES] = res
        off = res[:, LANES - 1:LANES]
        ends.append(off)
    return ends


def _select_kernel(aff_ref, pos_ref, bnd_ref, *, cap):
    aff = aff_ref[0]
    bits = pltpu.bitcast(aff, jnp.int32)
    cur = jnp.zeros((N_EXPERTS, 1), jnp.int32)
    for bit in range(30, -1, -1):
        cand = cur | (1 << bit)
        cnt = jnp.sum((bits >= cand).astype(jnp.int32), axis=1, keepdims=True)
        cur = jnp.where(cnt >= cap, cand, cur)
    gt = bits > cur
    eq = bits == cur
    n_gt = jnp.sum(gt.astype(jnp.int32), axis=1, keepdims=True)
    ties_taken = (cap - n_gt).astype(F32)
    ri = lax.broadcasted_iota(jnp.int32, (LANES, LANES), 0)
    ci = lax.broadcasted_iota(jnp.int32, (LANES, LANES), 1)
    tri = (ri <= ci).astype(BF16)
    _lane_prefix(eq, tri, pos_ref.at[0])
    sel = gt | (eq & (pos_ref[0] <= ties_taken))
    ends = _lane_prefix(sel, tri, pos_ref.at[0])
    pos_ref[0] = jnp.where(sel, pos_ref[0], 0.0)
    cols = []
    for sb in range(cap // SLOT_BLOCK):
        for target in (sb * SLOT_BLOCK + 1, (sb + 1) * SLOT_BLOCK):
            blk = jnp.zeros((N_EXPERTS, 1), jnp.int32)
            for end in ends[:-1]:
                blk = blk + (end < target).astype(jnp.int32)
            cols.append(blk)
    bnd_ref[0] = jnp.concatenate(cols, axis=1)


def _compact_kernel(bnd_ref, pos_ref, aff_ref, idx_ref, gate_ref, ptok, pgate, *, cap):
    b = pl.program_id(0)
    n_sb = cap // SLOT_BLOCK
    groups = SLOT_BLOCK // SUBLANES
    lane = lax.broadcasted_iota(jnp.int32, (SUBLANES, LANES), 1).astype(F32)
    sub1 = (lax.broadcasted_iota(jnp.int32, (SUBLANES, 1), 0) + 1).astype(F32)
    lane_e = lax.broadcasted_iota(jnp.int32, (cap, N_EXPERTS), 1)
    idx_ref[0] = jnp.zeros((cap, N_EXPERTS), jnp.int32)
    gate_ref[0] = jnp.zeros((cap, N_EXPERTS), F32)

    def expert(e, carry):
        def slot_block(sb, carry2):
            base = ((b * N_EXPERTS + e) * n_sb + sb) * 2
            slot0 = jnp.asarray(sb * SLOT_BLOCK, F32)

            def lane_block(c, accs):
                off = pl.multiple_of(c * LANES, LANES)
                p = jnp.broadcast_to(pos_ref[0, e:e + 1, pl.ds(off, LANES)], (SUBLANES, LANES))
                a = jnp.broadcast_to(aff_ref[0, e:e + 1, pl.ds(off, LANES)], (SUBLANES, LANES))
                tokv = lane + jnp.asarray(c * LANES, F32)
                out = []
                for g in range(groups):
                    hit = p == (slot0 + (g * SUBLANES) + sub1)
                    out.append(accs[2 * g] + jnp.where(hit, tokv, 0.0))
                    out.append(accs[2 * g + 1] + jnp.where(hit, a, 0.0))
                return tuple(out)

            zeros = tuple(jnp.zeros((SUBLANES, LANES), F32) for _ in range(2 * groups))
            accs = lax.fori_loop(bnd_ref[base], bnd_ref[base + 1] + 1, lane_block, zeros)
            for g in range(groups):
                rows = pl.ds(pl.multiple_of(sb * SLOT_BLOCK + g * SUBLANES, SUBLANES), SUBLANES)
                ptok[rows, :] = accs[2 * g]
                pgate[rows, :] = accs[2 * g + 1]
            return carry2

        lax.fori_loop(0, n_sb, slot_block, 0)
        tok_col = jnp.sum(ptok[...], axis=1, keepdims=True).astype(jnp.int32)
        gate_col = jnp.sum(pgate[...], axis=1, keepdims=True)
        idx_ref[0] = jnp.where(lane_e == e, tok_col, idx_ref[0])
        gate_ref[0] = jnp.where(lane_e == e, gate_col, gate_ref[0])
        return carry

    for e in range(N_EXPERTS):
        expert(e, 0)


def _route(aff_t, cap):
    batch, _, seq = aff_t.shape
    n_bnd = 2 * (cap // SLOT_BLOCK)
    seq_spec = pl.BlockSpec((1, N_EXPERTS, seq), lambda b, *_: (b, 0, 0))
    pos, bnd = pl.pallas_call(
        functools.partial(_select_kernel, cap=cap),
        grid=(batch,),
        in_specs=[seq_spec],
        out_specs=[seq_spec, pl.BlockSpec((1, N_EXPERTS, n_bnd), lambda b: (b, 0, 0))],
        out_shape=[
            jax.ShapeDtypeStruct((batch, N_EXPERTS, seq), F32),
            jax.ShapeDtypeStruct((batch, N_EXPERTS, n_bnd), jnp.int32),
        ],
        compiler_params=pltpu.CompilerParams(dimension_semantics=("arbitrary",)),
        name="expert_choice_select",
    )(aff_t)
    slot_spec = pl.BlockSpec((1, cap, N_EXPERTS), lambda b, *_: (b, 0, 0))
    idx_t, gate_t = pl.pallas_call(
        functools.partial(_compact_kernel, cap=cap),
        grid_spec=pltpu.PrefetchScalarGridSpec(
            num_scalar_prefetch=1,
            grid=(batch,),
            in_specs=[seq_spec, seq_spec],
            out_specs=[slot_spec, slot_spec],
            scratch_shapes=[pltpu.VMEM((cap, LANES), F32), pltpu.VMEM((cap, LANES), F32)],
        ),
        out_shape=[
            jax.ShapeDtypeStruct((batch, cap, N_EXPERTS), jnp.int32),
            jax.ShapeDtypeStruct((batch, cap, N_EXPERTS), F32),
        ],
        compiler_params=pltpu.CompilerParams(dimension_semantics=("arbitrary",)),
        name="expert_choice_compact",
    )(bnd.reshape(-1), pos, aff_t)
    return jnp.swapaxes(idx_t, 1, 2), gate_t


FF_CHUNKS = 2
LN_CHUNK = 256


def _moe_kernel(idx_ref, x1t_hbm, gate_ref, wg_ref, wu_ref, wd_ref, lng_ref, lnb_ref, out_hbm,
                x1v, acc, xs3, xs2d, yacc, stage, sem_in, sem_out, *, seq, cap):
    b = pl.program_id(0)
    e = pl.program_id(1)
    j = pl.program_id(2)
    last_j = FF_CHUNKS - 1
    tok_rows = seq * TOK_ROWS

    slot_base = (b * N_EXPERTS + e) * cap

    def slot_token(i):
        return idx_ref[slot_base + i]

    def tok_slice(t):
        return pl.ds(pl.multiple_of(t * TOK_ROWS, TOK_ROWS), TOK_ROWS)

    @pl.when((e == 0) & (j == 0))
    def _load_sequence():
        cp = pltpu.make_async_copy(x1t_hbm.at[pl.ds(b * tok_rows, tok_rows)], x1v, sem_in)
        cp.start()
        acc[...] = jnp.zeros_like(acc)
        cp.wait()

    @pl.when(j == 0)
    def _gather():
        def body(i, carry):
            xs3[tok_slice(i), :] = x1v[tok_slice(slot_token(i)), :]
            return carry
        lax.fori_loop(0, cap, body, 0, unroll=8)
        xs2d[...] = _load_token_layout(xs3, 0, cap).astype(BF16)

    xs = xs2d[...]
    g = jnp.dot(xs, wg_ref[0, 0].astype(BF16), preferred_element_type=F32)
    u = jnp.dot(xs, wu_ref[0, 0].astype(BF16), preferred_element_type=F32)
    act = (g * jax.nn.sigmoid(g) * u).astype(BF16)
    yp = jnp.dot(act, wd_ref[0, 0].astype(BF16), preferred_element_type=F32)

    @pl.when(j == 0)
    def _first():
        yacc[...] = yp

    @pl.when(j > 0)
    def _rest():
        yacc[...] += yp

    @pl.when(j == last_j)
    def _scatter():
        gates = gate_ref[0]
        lane_e = lax.broadcasted_iota(jnp.int32, gates.shape, 1)
        gcol = jnp.sum(jnp.where(lane_e == e, gates, 0.0), axis=1, keepdims=True)
        _store_token_layout(xs3, yacc[...] * gcol)

        def body(gi, carry):
            toks = [slot_token(gi * SUBLANES + k) for k in range(SUBLANES)]
            vals = [acc[tok_slice(toks[k]), :] + xs3[tok_slice(gi * SUBLANES + k), :]
                    for k in range(SUBLANES)]
            for k in range(SUBLANES):
                acc[tok_slice(toks[k]), :] = vals[k]
            return carry
        lax.fori_loop(0, cap // SUBLANES, body, 0)

    @pl.when((e == N_EXPERTS - 1) & (j == last_j))
    def _finalize():
        n_chunks = seq // LN_CHUNK

        def out_copy(c, slot):
            return pltpu.make_async_copy(
                stage.at[slot], out_hbm.at[pl.ds(b * seq + c * LN_CHUNK, LN_CHUNK)], sem_out.at[slot])

        def body(c, carry):
            slot = c % 2
            first = c * LN_CHUNK
            res = _layer_norm(ALPHA * _load_token_layout(x1v, first, LN_CHUNK)
                              + _load_token_layout(acc, first, LN_CHUNK), lng_ref[...], lnb_ref[...])

            @pl.when(c >= 2)
            def _():
                out_copy(c - 2, slot).wait()
            stage[slot] = res
            out_copy(c, slot).start()
            return carry
        lax.fori_loop(0, n_chunks, body, 0)
        out_copy(n_chunks - 2, (n_chunks - 2) % 2).wait()
        out_copy(n_chunks - 1, (n_chunks - 1) % 2).wait()


def _moe(x1t, idx, gate_t, w_gate_up, w_down, ln_g, ln_b, layer, batch, seq):
    cap = idx.shape[2]
    T = batch * seq
    fc = EXPERT_FF // FF_CHUNKS
    idx_flat = idx.reshape(-1)
    grid_spec = pltpu.PrefetchScalarGridSpec(
        num_scalar_prefetch=1,
        grid=(batch, N_EXPERTS, FF_CHUNKS),
        in_specs=[
            pl.BlockSpec(memory_space=pl.ANY),
            pl.BlockSpec((1, cap, N_EXPERTS), lambda b, e, j, idx: (b, 0, 0)),
            pl.BlockSpec((1, 1, D_MODEL, fc), lambda b, e, j, idx: (layer, e, 0, j)),
            pl.BlockSpec((1, 1, D_MODEL, fc), lambda b, e, j, idx: (layer, e, 0, FF_CHUNKS + j)),
            pl.BlockSpec((1, 1, fc, D_MODEL), lambda b, e, j, idx: (layer, e, j, 0)),
            pl.BlockSpec((1, D_MODEL), lambda b, e, j, idx: (0, 0)),
            pl.BlockSpec((1, D_MODEL), lambda b, e, j, idx: (0, 0)),
        ],
        out_specs=pl.BlockSpec(memory_space=pl.ANY),
        scratch_shapes=[
            pltpu.VMEM((seq * TOK_ROWS, LANES), F32),
            pltpu.VMEM((seq * TOK_ROWS, LANES), F32),
            pltpu.VMEM((cap * TOK_ROWS, LANES), F32),
            pltpu.VMEM((cap, D_MODEL), BF16),
            pltpu.VMEM((cap, D_MODEL), F32),
            pltpu.VMEM((2, LN_CHUNK, D_MODEL), F32),
            pltpu.SemaphoreType.DMA(()),
            pltpu.SemaphoreType.DMA((2,)),
        ],
    )
    return pl.pallas_call(
        functools.partial(_moe_kernel, seq=seq, cap=cap),
        grid_spec=grid_spec,
        out_shape=jax.ShapeDtypeStruct((T, D_MODEL), F32),
        compiler_params=pltpu.CompilerParams(
            dimension_semantics=("arbitrary", "arbitrary", "arbitrary"),
            vmem_limit_bytes=VMEM_LIMIT_MOE),
        name="expert_ffn_combine_ln",
    )(idx_flat, x1t, gate_t, w_gate_up, w_gate_up, w_down, ln_g, ln_b)


HALO = SUBLANES


def _gelu_tanh(x):
    return 0.5 * x * (1.0 + jnp.tanh(0.7978845608028654 * (x + 0.044715 * (x * x * x))))


def _lru_in_kernel(x_ref, xp_ref, xn_ref, w_ref, cw_ref, cb_ref, gg_ref, xc_ref, ext, *, nts):
    i = pl.program_id(0)
    tm = x_ref.shape[0]
    w = w_ref[...]
    acc = jnp.dot(x_ref[...].astype(BF16), w, preferred_element_type=F32)
    gg_ref[...] = _gelu_tanh(acc[:, :D_MODEL])
    wx = w[:, D_MODEL:]
    first = (i % nts) == 0
    last = (i % nts) == nts - 1
    xbp = jnp.dot(xp_ref[...].astype(BF16), wx, preferred_element_type=F32)
    xbn = jnp.dot(xn_ref[...].astype(BF16), wx, preferred_element_type=F32)
    ext[0:HALO, :] = jnp.where(first, 0.0, xbp)
    ext[HALO:HALO + tm, :] = acc[:, D_MODEL:]
    ext[HALO + tm:, :] = jnp.where(last, 0.0, xbn)
    xc = cb_ref[...]
    for tap in range(CONV_WIDTH):
        xc = xc + cw_ref[tap:tap + 1, :] * ext[HALO - 2 + tap:HALO - 2 + tap + tm, :]
    xc_ref[...] = xc


def _lru_in(x2d, w_in, conv_w, conv_b, seq):
    T = x2d.shape[0]
    tm = min(512, seq)
    nts = seq // tm
    hb = tm // HALO
    nblk = T // HALO
    return pl.pallas_call(
        functools.partial(_lru_in_kernel, nts=nts),
        grid=(T // tm,),
        in_specs=[
            pl.BlockSpec((tm, D_MODEL), lambda i: (i, 0)),
            pl.BlockSpec((HALO, D_MODEL), lambda i: (jnp.maximum(i * hb - 1, 0), 0)),
            pl.BlockSpec((HALO, D_MODEL), lambda i: (jnp.minimum((i + 1) * hb, nblk - 1), 0)),
            pl.BlockSpec((D_MODEL, 2 * D_MODEL), lambda i: (0, 0)),
            pl.BlockSpec((CONV_WIDTH, D_MODEL), lambda i: (0, 0)),
            pl.BlockSpec((1, D_MODEL), lambda i: (0, 0)),
        ],
        out_specs=[
            pl.BlockSpec((tm, D_MODEL), lambda i: (i, 0)),
            pl.BlockSpec((tm, D_MODEL), lambda i: (i, 0)),
        ],
        out_shape=[
            jax.ShapeDtypeStruct((T, D_MODEL), F32),
            jax.ShapeDtypeStruct((T, D_MODEL), F32),
        ],
        scratch_shapes=[pltpu.VMEM((tm + 2 * HALO, D_MODEL), F32)],
        compiler_params=pltpu.CompilerParams(dimension_semantics=("arbitrary",)),
        name="lru_in_conv",
    )(x2d, x2d, x2d, w_in, conv_w, conv_b)


def _lru_gates_scan(xc_ref, wg_ref, br_ref, bi_ref, lam_ref, carry, hbuf, abuf, *, reverse, seq):
    i = pl.program_id(1)
    tm = xc_ref.shape[0]
    nt = seq // tm
    tile = (nt - 1 - i) if reverse else i

    @pl.when(i == 0)
    def _():
        carry[...] = jnp.zeros_like(carry)

    xc = xc_ref[...]
    xcb = xc.astype(BF16)
    r_parts, i_parts = [], []
    for n in range(LRU_BLOCKS):
        res = jnp.dot(xcb[:, n * LRU_BLOCK_W:(n + 1) * LRU_BLOCK_W], wg_ref[n],
                      preferred_element_type=F32)
        r_parts.append(res[:, :LRU_BLOCK_W])
        i_parts.append(res[:, LRU_BLOCK_W:])
    r = jax.nn.sigmoid(jnp.concatenate(r_parts, axis=1) + br_ref[...])
    ig = jax.nn.sigmoid(jnp.concatenate(i_parts, axis=1) + bi_ref[...])
    log_a = -LRU_C * r * jax.nn.softplus(-lam_ref[...])
    a = jnp.exp(log_a)
    mult = jnp.sqrt(-jnp.tanh(log_a) * (1.0 + a * a))
    row = lax.broadcasted_iota(jnp.int32, (tm, 1), 0)
    tpos = tile * tm + row
    is_start = tpos == (seq - 1 if reverse else 0)
    mult = jnp.where(is_start, 1.0, mult)
    u = mult * ig * xc

    sub = row % SUBLANES
    for k in (1, 2, 4):
        if reverse:
            a_sh = pltpu.roll(a, tm - k, axis=0)
            u_sh = pltpu.roll(u, tm - k, axis=0)
            ok = sub < SUBLANES - k
        else:
            a_sh = pltpu.roll(a, k, axis=0)
            u_sh = pltpu.roll(u, k, axis=0)
            ok = sub >= k
        u = jnp.where(ok, a * u_sh + u, u)
        a = jnp.where(ok, a * a_sh, a)
    hbuf[...] = u
    abuf[...] = a

    n_groups = tm // SUBLANES
    edge = 0 if reverse else SUBLANES - 1

    def body(gi, c):
        g = (n_groups - 1 - gi) if reverse else gi
        rows = pl.ds(pl.multiple_of(g * SUBLANES, SUBLANES), SUBLANES)
        h = hbuf[rows, :] + abuf[rows, :] * c
        hbuf[rows, :] = h
        return jnp.broadcast_to(h[edge:edge + 1, :], h.shape)

    c0 = jnp.broadcast_to(carry[...], (SUBLANES, D_MODEL))
    c_end = lax.fori_loop(0, n_groups, body, c0)
    carry[...] = c_end[0:1, :]


def _lru_fwd_kernel(xc_ref, wg_ref, br_ref, bi_ref, lam_ref, h_ref, carry, hbuf, abuf, *, seq):
    _lru_gates_scan(xc_ref, wg_ref, br_ref, bi_ref, lam_ref, carry, hbuf, abuf, reverse=False, seq=seq)
    h_ref[...] = hbuf[...]


def _lru_bwd_out_kernel(xc_ref, wg_ref, br_ref, bi_ref, lam_ref, hf_ref, gg_ref, x_ref, wo_ref,
                        g_ref, b_ref, wr_ref, xo_ref, aff_ref, carry, hbuf, abuf, *, seq):
    _lru_gates_scan(xc_ref, wg_ref, br_ref, bi_ref, lam_ref, carry, hbuf, abuf, reverse=True, seq=seq)
    h = hf_ref[...] + hbuf[...]
    y = jnp.dot((h * gg_ref[...]).astype(BF16), wo_ref[...], preferred_element_type=F32)
    xn = _layer_norm(ALPHA * x_ref[...] + y, g_ref[...], b_ref[...])
    _store_token_layout(xo_ref, xn)
    aff_ref[0] = _router_affinity_t(xn, wr_ref[...])


LRU_TM = 256


def _lru_common_specs(tile_map):
    return [
        pl.BlockSpec((LRU_TM, D_MODEL), tile_map),
        pl.BlockSpec((LRU_BLOCKS, LRU_BLOCK_W, 2 * LRU_BLOCK_W), lambda b, i: (0, 0, 0)),
        pl.BlockSpec((1, D_MODEL), lambda b, i: (0, 0)),
        pl.BlockSpec((1, D_MODEL), lambda b, i: (0, 0)),
        pl.BlockSpec((1, D_MODEL), lambda b, i: (0, 0)),
    ]


def _lru_scratch():
    return [pltpu.VMEM((1, D_MODEL), F32), pltpu.VMEM((LRU_TM, D_MODEL), F32),
            pltpu.VMEM((LRU_TM, D_MODEL), F32)]


def _lru_forward(xc, wg, br, bi, lam, batch, seq):
    T = xc.shape[0]
    nt = seq // LRU_TM
    tile_map = lambda b, i: (b * nt + i, 0)
    return pl.pallas_call(
        functools.partial(_lru_fwd_kernel, seq=seq),
        grid=(batch, nt),
        in_specs=_lru_common_specs(tile_map),
        out_specs=pl.BlockSpec((LRU_TM, D_MODEL), tile_map),
        out_shape=jax.ShapeDtypeStruct((T, D_MODEL), F32),
        scratch_shapes=_lru_scratch(),
        compiler_params=pltpu.CompilerParams(dimension_semantics=("arbitrary", "arbitrary")),
        name="lru_forward_scan",
    )(xc, wg, br, bi, lam)


def _lru_backward_out(xc, wg, br, bi, lam, hf, gg, x2d, w_out, g, b, wr_t, batch, seq):
    T = xc.shape[0]
    nt = seq // LRU_TM
    tile_map = lambda b, i: (b * nt + nt - 1 - i, 0)
    row_spec = pl.BlockSpec((1, D_MODEL), lambda b, i: (0, 0))
    return pl.pallas_call(
        functools.partial(_lru_bwd_out_kernel, seq=seq),
        grid=(batch, nt),
        in_specs=_lru_common_specs(tile_map) + [
            pl.BlockSpec((LRU_TM, D_MODEL), tile_map),
            pl.BlockSpec((LRU_TM, D_MODEL), tile_map),
            pl.BlockSpec((LRU_TM, D_MODEL), tile_map),
            pl.BlockSpec((D_MODEL, D_MODEL), lambda b, i: (0, 0)),
            row_spec, row_spec,
            pl.BlockSpec((N_EXPERTS, D_MODEL), lambda b, i: (0, 0)),
        ],
        out_specs=[
            pl.BlockSpec((LRU_TM * TOK_ROWS, LANES), tile_map),
            pl.BlockSpec((1, N_EXPERTS, LRU_TM), lambda b, i: (b, 0, nt - 1 - i)),
        ],
        out_shape=[
            jax.ShapeDtypeStruct((T * TOK_ROWS, LANES), F32),
            jax.ShapeDtypeStruct((batch, N_EXPERTS, seq), F32),
        ],
        scratch_shapes=_lru_scratch(),
        compiler_params=pltpu.CompilerParams(dimension_semantics=("arbitrary", "arbitrary")),
        name="lru_backward_scan_out_ln_router",
    )(xc, wg, br, bi, lam, hf, gg, x2d, w_out, g, b, wr_t)


def _rotary_tables(seq):
    pos = jnp.arange(seq, dtype=F32)
    inv_freq = ROPE_THETA ** (-jnp.arange(0, ROT_DIM, 2, dtype=F32) / ROT_DIM)
    ang = pos[:, None] * inv_freq[None, :]
    cos, sin = jnp.cos(ang), jnp.sin(ang)
    pad = HEAD_DIM - ROT_DIM
    cos_h = jnp.concatenate([cos, cos, jnp.ones((seq, pad), F32)], axis=1)
    sin_h = jnp.concatenate([-sin, sin, jnp.zeros((seq, pad), F32)], axis=1)
    reps = LANES // HEAD_DIM
    return jnp.tile(cos_h, (1, reps)), jnp.tile(sin_h, (1, reps))


def _moe_block(x1t, aff_t, w_gate_up, w_down, ln_g, ln_b, layer, batch, seq):
    cap = CAPACITY_FACTOR * seq // N_EXPERTS
    idx, gate_t = _route(aff_t, cap)
    return _moe(x1t, idx, gate_t, w_gate_up, w_down, ln_g[layer].reshape(1, -1), ln_b[layer].reshape(1, -1),
                layer, batch, seq)


def kernel(x, attn_w_qkv, attn_w_o, attn_sink, lru_w_in, lru_conv_w, lru_conv_b, lru_w_rgate,
           lru_b_rgate, lru_w_igate, lru_b_igate, lru_lambda, lru_w_out, moe_w_router, moe_w_gate_up,
           moe_w_down, ln_mix_g, ln_mix_b, ln_ffn_g, ln_ffn_b):
    batch, seq, _ = x.shape
    x2d = x.reshape(batch * seq, D_MODEL)
    row = lambda v: v.reshape(1, -1)

    cos_t, sin_t = _rotary_tables(seq)
    q, k, v = _qkv_proj(x2d, attn_w_qkv[0].astype(BF16), cos_t, sin_t, seq)
    o = _attention(q, k, v, attn_sink[0], batch, seq)
    x1, aff_t = _proj_ln(o, attn_w_o[0].astype(BF16), x2d, row(ln_mix_g[0]), row(ln_mix_b[0]),
                         moe_w_router[0].T.astype(BF16), batch, seq)
    x2 = _moe_block(x1, aff_t, moe_w_gate_up, moe_w_down, ln_ffn_g, ln_ffn_b, 0, batch, seq)

    gg, xc = _lru_in(x2, lru_w_in[0].astype(BF16), lru_conv_w[0], row(lru_conv_b[0]), seq)
    wg = jnp.concatenate([lru_w_rgate[0], lru_w_igate[0]], axis=-1).astype(BF16)
    hf = _lru_forward(xc, wg[0], row(lru_b_rgate[0, 0]), row(lru_b_igate[0, 0]), row(lru_lambda[0, 0]),
                      batch, seq)
    x3, aff_t = _lru_backward_out(xc, wg[1], row(lru_b_rgate[0, 1]), row(lru_b_igate[0, 1]),
                                  row(lru_lambda[0, 1]), hf, gg, x2, lru_w_out[0].astype(BF16),
                                  row(ln_mix_g[1]), row(ln_mix_b[1]), moe_w_router[1].T.astype(BF16),
                                  batch, seq)
    x4 = _moe_block(x3, aff_t, moe_w_gate_up, moe_w_down, ln_ffn_g, ln_ffn_b, 1, batch, seq)
    return x4.reshape(batch, seq, D_MODEL)
```

```python
import functools

import jax
import jax.numpy as jnp
from jax import lax
from jax.experimental import pallas as pl
from jax.experimental.pallas import tpu as pltpu

F32 = jnp.float32
BF16 = jnp.bfloat16

D_MODEL = 1024
HEAD_DIM = 64
N_Q_HEADS = 16
N_KV_HEADS = 4
GQA_GROUP = 4
WINDOW = 128
BLOCK = 128
ROT_DIM = 16
ROPE_THETA = 500000.0
LRU_BLOCKS = 4
LRU_BLOCK_W = 256
CONV_WIDTH = 4
LRU_C = 8.0
N_EXPERTS = 16
EXPERT_FF = 1024
CAPACITY_FACTOR = 2
LN_EPS = 1e-5
DEPTH = 2
ALPHA = (2 * DEPTH) ** 0.25

LANES = 128
SUBLANES = 8
TOK_ROWS = D_MODEL // LANES
VMEM_LIMIT_MOE = 61 * 1024 * 1024


def _layer_norm(v, g, b):
    mu = jnp.mean(v, axis=-1, keepdims=True)
    vc = v - mu
    var = jnp.mean(vc * vc, axis=-1, keepdims=True)
    return vc * lax.rsqrt(var + LN_EPS) * g + b


def _store_token_layout(dst_ref, val):
    n = val.shape[0]
    for s in range(TOK_ROWS):
        dst_ref[pl.ds(s, n, stride=TOK_ROWS), :] = val[:, s * LANES:(s + 1) * LANES]


def _load_token_layout(src_ref, first_token, n):
    base = first_token * TOK_ROWS
    return jnp.concatenate(
        [src_ref[pl.ds(base + s, n, stride=TOK_ROWS), :] for s in range(TOK_ROWS)], axis=1)


def _router_affinity_t(xn, wr_t):
    logits_t = lax.dot_general(wr_t, xn.astype(BF16), (((1,), (1,)), ((), ())),
                               preferred_element_type=F32)
    m = jnp.max(logits_t, axis=0, keepdims=True)
    p = jnp.exp(logits_t - m)
    return p / jnp.sum(p, axis=0, keepdims=True)


LOG2E = 1.4426950408889634


def _qkv_kernel(x_ref, w_ref, cos_ref, sin_ref, q_ref, kv_ref):
    x = x_ref[...].astype(BF16)
    acc = jnp.dot(x, w_ref[...], preferred_element_type=F32)
    cosb = cos_ref[...]
    sinb = sin_ref[...]
    lane = lax.broadcasted_iota(jnp.int32, cosb.shape, 1)
    low = (lane % HEAD_DIM) < (ROT_DIM // 2)
    nq = N_Q_HEADS * HEAD_DIM
    nk = N_KV_HEADS * HEAD_DIM

    def rot(t):
        up = pltpu.roll(t, LANES - ROT_DIM // 2, axis=1)
        dn = pltpu.roll(t, ROT_DIM // 2, axis=1)
        return t * cosb + jnp.where(low, up, dn) * sinb

    qs = [rot(acc[:, c * LANES:(c + 1) * LANES]) * (LOG2E * HEAD_DIM ** -0.5) for c in range(nq // LANES)]
    q_ref[...] = jnp.concatenate(qs, axis=1).astype(BF16)
    ks = [rot(acc[:, nq + c * LANES:nq + (c + 1) * LANES]) for c in range(nk // LANES)]
    vs = [acc[:, nq + nk + c * LANES:nq + nk + (c + 1) * LANES] for c in range(nk // LANES)]
    swap = lambda t: pltpu.roll(t, HEAD_DIM, axis=1)
    kv_ref[...] = jnp.concatenate(
        ks + [swap(t) for t in ks] + vs + [swap(t) for t in vs], axis=1).astype(BF16)


def _qkv_proj(x2d, w_qkv, cos_t, sin_t, seq):
    T = x2d.shape[0]
    tm = min(512, seq)
    nq = N_Q_HEADS * HEAD_DIM
    nk = N_KV_HEADS * HEAD_DIM
    nts = seq // tm
    return pl.pallas_call(
        _qkv_kernel,
        grid=(T // tm,),
        in_specs=[
            pl.BlockSpec((tm, D_MODEL), lambda i: (i, 0)),
            pl.BlockSpec((D_MODEL, nq + 2 * nk), lambda i: (0, 0)),
            pl.BlockSpec((tm, LANES), lambda i: (i % nts, 0)),
            pl.BlockSpec((tm, LANES), lambda i: (i % nts, 0)),
        ],
        out_specs=[
            pl.BlockSpec((tm, nq), lambda i: (i, 0)),
            pl.BlockSpec((tm, 4 * nk), lambda i: (i, 0)),
        ],
        out_shape=[
            jax.ShapeDtypeStruct((T, nq), BF16),
            jax.ShapeDtypeStruct((T, 4 * nk), BF16),
        ],
        compiler_params=pltpu.CompilerParams(dimension_semantics=("arbitrary",)),
        name="qkv_proj",
    )(x2d, w_qkv, cos_t, sin_t)


QBLKS = 4


def _attn_kernel(sink_ref, q_ref, kvp_ref, kvc_ref, kvn_ref, o_ref, *, seq):
    i = pl.program_id(1)
    nk = N_KV_HEADS * HEAD_DIM
    kv = jnp.concatenate([kvp_ref[...], kvc_ref[...], kvn_ref[...]], axis=0)
    band = 3 * BLOCK
    lo = lax.broadcasted_iota(jnp.int32, (1, LANES), 1) < HEAD_DIM
    zero = jnp.zeros((), BF16)
    k_lo, k_hi, v_dup = [], [], []
    for h in range(N_KV_HEADS):
        grp, half = h // 2, h % 2

        def block(base, swapped):
            col = base + (nk if swapped else 0) + grp * LANES
            return kv[:, col:col + LANES]
        k_lo.append(jnp.where(lo, block(0, half == 1), zero))
        k_hi.append(jnp.where(lo, zero, block(0, half == 0)))
        v_dup.append(jnp.where(lo, block(2 * nk, half == 1), block(2 * nk, half == 0)))

    r = lax.broadcasted_iota(jnp.int32, (BLOCK, BLOCK), 0)
    c = lax.broadcasted_iota(jnp.int32, (BLOCK, BLOCK), 1)
    units = [(j, h) for j in range(QBLKS) for h in range(N_KV_HEADS)]

    def scores(j, h):
        rows = slice(j * BLOCK, (j + 1) * BLOCK)
        qp = jnp.concatenate([q_ref[rows, (2 * h) * LANES:(2 * h + 1) * LANES],
                              q_ref[rows, (2 * h + 1) * LANES:(2 * h + 2) * LANES]], axis=0)
        keys = jnp.concatenate([k_lo[h][j * BLOCK:j * BLOCK + band], k_hi[h][j * BLOCK:j * BLOCK + band]],
                               axis=0)
        return lax.dot_general(qp, keys, (((1,), (1,)), ((), ())), preferred_element_type=F32)

    s_next = scores(*units[0])
    for u, (j, h) in enumerate(units):
        s = s_next
        if u + 1 < len(units):
            s_next = scores(*units[u + 1])
        n = i * QBLKS + j
        ok_prev = jnp.concatenate([(c >= r) & (n > 0)] * 2, axis=0)
        ok_next = jnp.concatenate([(c <= r) & (n < seq // BLOCK - 1)] * 2, axis=0)
        vb = v_dup[h][j * BLOCK:j * BLOCK + band]
        halves = []
        for par in range(2):
            sp = s[:, par * band:(par + 1) * band]
            sp = jnp.concatenate([jnp.where(ok_prev, sp[:, :BLOCK], -jnp.inf), sp[:, BLOCK:2 * BLOCK],
                                  jnp.where(ok_next, sp[:, 2 * BLOCK:], -jnp.inf)], axis=1)
            sink_col = jnp.concatenate(
                [jnp.full((BLOCK, 1), sink_ref[h * GQA_GROUP + g] * LOG2E, F32) for g in (par, par + 2)],
                axis=0)
            m = jnp.maximum(jnp.max(sp, axis=1, keepdims=True), sink_col)
            p = jnp.exp2(sp - m)
            denom = jnp.sum(p, axis=1, keepdims=True) + jnp.exp2(sink_col - m)
            halves.append(jnp.dot(p.astype(BF16), vb, preferred_element_type=F32) / denom)
        out = jnp.where(lo, halves[0], halves[1])
        o_ref[j * BLOCK:(j + 1) * BLOCK, (2 * h) * LANES:(2 * h + 2) * LANES] = jnp.concatenate(
            [out[:BLOCK], out[BLOCK:]], axis=1).astype(BF16)


def _attention(q, kv, sink, batch, seq):
    T = q.shape[0]
    qt = QBLKS * BLOCK
    nq = seq // qt
    nb = seq // BLOCK
    wkv = kv.shape[1]
    cur = lambda b, i: (b * nq + i, 0)
    prev = lambda b, i: (b * nb + jnp.maximum(i * QBLKS - 1, 0), 0)
    nxt = lambda b, i: (b * nb + jnp.minimum(i * QBLKS + QBLKS, nb - 1), 0)
    return pl.pallas_call(
        functools.partial(_attn_kernel, seq=seq),
        grid=(batch, nq),
        in_specs=[
            pl.BlockSpec(memory_space=pltpu.SMEM),
            pl.BlockSpec((qt, D_MODEL), cur),
            pl.BlockSpec((BLOCK, wkv), prev),
            pl.BlockSpec((qt, wkv), cur),
            pl.BlockSpec((BLOCK, wkv), nxt),
        ],
        out_specs=pl.BlockSpec((qt, D_MODEL), cur),
        out_shape=jax.ShapeDtypeStruct((T, D_MODEL), BF16),
        compiler_params=pltpu.CompilerParams(dimension_semantics=("arbitrary", "arbitrary")),
        name="swa_attention",
    )(sink, q, kv, kv, kv)


def _proj_ln_kernel(a_ref, w_ref, x_ref, g_ref, b_ref, wr_ref, xo_ref, aff_ref):
    y = jnp.dot(a_ref[...], w_ref[...], preferred_element_type=F32)
    xn = _layer_norm(ALPHA * x_ref[...] + y, g_ref[...], b_ref[...])
    _store_token_layout(xo_ref, xn)
    aff_ref[0] = _router_affinity_t(xn, wr_ref[...])


def _proj_ln(a, w, x2d, g, b, wr_t, batch, seq):
    T = x2d.shape[0]
    tm = min(512, seq)
    nts = seq // tm
    return pl.pallas_call(
        _proj_ln_kernel,
        grid=(T // tm,),
        in_specs=[
            pl.BlockSpec((tm, D_MODEL), lambda i: (i, 0)),
            pl.BlockSpec((D_MODEL, D_MODEL), lambda i: (0, 0)),
            pl.BlockSpec((tm, D_MODEL), lambda i: (i, 0)),
            pl.BlockSpec((1, D_MODEL), lambda i: (0, 0)),
            pl.BlockSpec((1, D_MODEL), lambda i: (0, 0)),
            pl.BlockSpec((N_EXPERTS, D_MODEL), lambda i: (0, 0)),
        ],
        out_specs=[
            pl.BlockSpec((tm * TOK_ROWS, LANES), lambda i: (i, 0)),
            pl.BlockSpec((1, N_EXPERTS, tm), lambda i: (i // nts, 0, i % nts)),
        ],
        out_shape=[
            jax.ShapeDtypeStruct((T * TOK_ROWS, LANES), F32),
            jax.ShapeDtypeStruct((batch, N_EXPERTS, seq), F32),
        ],
        compiler_params=pltpu.CompilerParams(dimension_semantics=("arbitrary",)),
        name="proj_ln_router",
    )(a, w, x2d, g, b, wr_t)


SLOT_BLOCK = 64


def _lane_prefix(mask, tri, dst_ref):
    seq = mask.shape[1]
    off = jnp.zeros((mask.shape[0], 1), F32)
    mb = mask.astype(BF16)
    ends = []
    for c in range(seq // LANES):
        res = jnp.dot(mb[:, c * LANES:(c + 1) * LANES], tri, preferred_element_type=F32) + off
        dst_ref[:, c * LANES:(c + 1) * LANES] = res
        off = res[:, LANES - 1:LANES]
        ends.append(off)
    return ends


def _select_kernel(aff_ref, pos_ref, bnd_ref, *, cap):
    aff = aff_ref[0]
    bits = pltpu.bitcast(aff, jnp.int32)
    cur = jnp.zeros((N_EXPERTS, 1), jnp.int32)
    for bit in range(30, -1, -1):
        cand = cur | (1 << bit)
        cnt = jnp.sum((bits >= cand).astype(jnp.int32), axis=1, keepdims=True)
        cur = jnp.where(cnt >= cap, cand, cur)
    gt = bits > cur
    eq = bits == cur
    n_gt = jnp.sum(gt.astype(jnp.int32), axis=1, keepdims=True)
    ties_taken = (cap - n_gt).astype(F32)
    ri = lax.broadcasted_iota(jnp.int32, (LANES, LANES), 0)
    ci = lax.broadcasted_iota(jnp.int32, (LANES, LANES), 1)
    tri = (ri <= ci).astype(BF16)
    _lane_prefix(eq, tri, pos_ref.at[0])
    sel = gt | (eq & (pos_ref[0] <= ties_taken))
    ends = _lane_prefix(sel, tri, pos_ref.at[0])
    pos_ref[0] = jnp.where(sel, pos_ref[0], 0.0)
    cols = []
    for sb in range(cap // SLOT_BLOCK):
        for target in (sb * SLOT_BLOCK + 1, (sb + 1) * SLOT_BLOCK):
            blk = jnp.zeros((N_EXPERTS, 1), jnp.int32)
            for end in ends[:-1]:
                blk = blk + (end < target).astype(jnp.int32)
            cols.append(blk)
    bnd_ref[0] = jnp.concatenate(cols, axis=1)


def _compact_kernel(bnd_ref, pos_ref, aff_ref, idx_ref, gate_ref, ptok, pgate, *, cap):
    b = pl.program_id(0)
    n_sb = cap // SLOT_BLOCK
    groups = SLOT_BLOCK // SUBLANES
    lane = lax.broadcasted_iota(jnp.int32, (SUBLANES, LANES), 1).astype(F32)
    sub1 = (lax.broadcasted_iota(jnp.int32, (SUBLANES, 1), 0) + 1).astype(F32)
    lane_e = lax.broadcasted_iota(jnp.int32, (cap, N_EXPERTS), 1)
    idx_ref[0] = jnp.zeros((cap, N_EXPERTS), jnp.int32)
    gate_ref[0] = jnp.zeros((cap, N_EXPERTS), F32)

    def expert(e, carry):
        def slot_block(sb, carry2):
            base = ((b * N_EXPERTS + e) * n_sb + sb) * 2
            slot0 = jnp.asarray(sb * SLOT_BLOCK, F32)

            def lane_block(c, accs):
                off = pl.multiple_of(c * LANES, LANES)
                p = jnp.broadcast_to(pos_ref[0, e:e + 1, pl.ds(off, LANES)], (SUBLANES, LANES))
                a = jnp.broadcast_to(aff_ref[0, e:e + 1, pl.ds(off, LANES)], (SUBLANES, LANES))
                tokv = lane + jnp.asarray(c * LANES, F32)
                out = []
                for g in range(groups):
                    hit = p == (slot0 + (g * SUBLANES) + sub1)
                    out.append(accs[2 * g] + jnp.where(hit, tokv, 0.0))
                    out.append(accs[2 * g + 1] + jnp.where(hit, a, 0.0))
                return tuple(out)

            zeros = tuple(jnp.zeros((SUBLANES, LANES), F32) for _ in range(2 * groups))
            accs = lax.fori_loop(bnd_ref[base], bnd_ref[base + 1] + 1, lane_block, zeros)
            for g in range(groups):
                rows = pl.ds(pl.multiple_of(sb * SLOT_BLOCK + g * SUBLANES, SUBLANES), SUBLANES)
                ptok[rows, :] = accs[2 * g]
                pgate[rows, :] = accs[2 * g + 1]
            return carry2

        lax.fori_loop(0, n_sb, slot_block, 0)
        tok_col = jnp.sum(ptok[...], axis=1, keepdims=True).astype(jnp.int32)
        gate_col = jnp.sum(pgate[...], axis=1, keepdims=True)
        idx_ref[0] = jnp.where(lane_e == e, tok_col, idx_ref[0])
        gate_ref[0] = jnp.where(lane_e == e, gate_col, gate_ref[0])
        return carry

    for e in range(N_EXPERTS):
        expert(e, 0)


def _route(aff_t, cap):
    batch, _, seq = aff_t.shape
    n_bnd = 2 * (cap // SLOT_BLOCK)
    seq_spec = pl.BlockSpec((1, N_EXPERTS, seq), lambda b, *_: (b, 0, 0))
    pos, bnd = pl.pallas_call(
        functools.partial(_select_kernel, cap=cap),
        grid=(batch,),
        in_specs=[seq_spec],
        out_specs=[seq_spec, pl.BlockSpec((1, N_EXPERTS, n_bnd), lambda b: (b, 0, 0))],
        out_shape=[
            jax.ShapeDtypeStruct((batch, N_EXPERTS, seq), F32),
            jax.ShapeDtypeStruct((batch, N_EXPERTS, n_bnd), jnp.int32),
        ],
        compiler_params=pltpu.CompilerParams(dimension_semantics=("arbitrary",)),
        name="expert_choice_select",
    )(aff_t)
    slot_spec = pl.BlockSpec((1, cap, N_EXPERTS), lambda b, *_: (b, 0, 0))
    idx_t, gate_t = pl.pallas_call(
        functools.partial(_compact_kernel, cap=cap),
        grid_spec=pltpu.PrefetchScalarGridSpec(
            num_scalar_prefetch=1,
            grid=(batch,),
            in_specs=[seq_spec, seq_spec],
            out_specs=[slot_spec, slot_spec],
            scratch_shapes=[pltpu.VMEM((cap, LANES), F32), pltpu.VMEM((cap, LANES), F32)],
        ),
        out_shape=[
            jax.ShapeDtypeStruct((batch, cap, N_EXPERTS), jnp.int32),
            jax.ShapeDtypeStruct((batch, cap, N_EXPERTS), F32),
        ],
        compiler_params=pltpu.CompilerParams(dimension_semantics=("arbitrary",)),
        name="expert_choice_compact",
    )(bnd.reshape(-1), pos, aff_t)
    return jnp.swapaxes(idx_t, 1, 2), gate_t


FF_CHUNKS = 2
LN_CHUNK = 256


def _moe_kernel(idx_ref, x1t_hbm, gate_ref, wg_ref, wu_ref, wd_ref, lng_ref, lnb_ref, out_hbm,
                x1v, acc, xs3, xs2d, yacc, stage, sem_in, sem_out, *, seq, cap):
    b = pl.program_id(0)
    e = pl.program_id(1)
    j = pl.program_id(2)
    last_j = FF_CHUNKS - 1
    tok_rows = seq * TOK_ROWS

    slot_base = (b * N_EXPERTS + e) * cap

    def slot_token(i):
        return idx_ref[slot_base + i]

    def tok_slice(t):
        return pl.ds(pl.multiple_of(t * TOK_ROWS, TOK_ROWS), TOK_ROWS)

    @pl.when((e == 0) & (j == 0))
    def _load_sequence():
        cp = pltpu.make_async_copy(x1t_hbm.at[pl.ds(b * tok_rows, tok_rows)], x1v, sem_in)
        cp.start()
        acc[...] = jnp.zeros_like(acc)
        cp.wait()

    @pl.when(j == 0)
    def _gather():
        def body(i, carry):
            xs3[tok_slice(i), :] = x1v[tok_slice(slot_token(i)), :]
            return carry
        lax.fori_loop(0, cap, body, 0, unroll=8)
        xs2d[...] = _load_token_layout(xs3, 0, cap).astype(BF16)

    xs = xs2d[...]
    g = jnp.dot(xs, wg_ref[0, 0].astype(BF16), preferred_element_type=F32)
    u = jnp.dot(xs, wu_ref[0, 0].astype(BF16), preferred_element_type=F32)
    act = (g * jax.nn.sigmoid(g) * u).astype(BF16)
    yp = jnp.dot(act, wd_ref[0, 0].astype(BF16), preferred_element_type=F32)

    @pl.when(j == 0)
    def _first():
        yacc[...] = yp

    @pl.when(j > 0)
    def _rest():
        yacc[...] += yp

    @pl.when(j == last_j)
    def _scatter():
        gates = gate_ref[0]
        lane_e = lax.broadcasted_iota(jnp.int32, gates.shape, 1)
        gcol = jnp.sum(jnp.where(lane_e == e, gates, 0.0), axis=1, keepdims=True)
        _store_token_layout(xs3, yacc[...] * gcol)

        def body(gi, carry):
            toks = [slot_token(gi * SUBLANES + k) for k in range(SUBLANES)]
            vals = [acc[tok_slice(toks[k]), :] + xs3[tok_slice(gi * SUBLANES + k), :]
                    for k in range(SUBLANES)]
            for k in range(SUBLANES):
                acc[tok_slice(toks[k]), :] = vals[k]
            return carry
        lax.fori_loop(0, cap // SUBLANES, body, 0)

    @pl.when((e == N_EXPERTS - 1) & (j == last_j))
    def _finalize():
        n_chunks = seq // LN_CHUNK

        def out_copy(c, slot):
            return pltpu.make_async_copy(
                stage.at[slot], out_hbm.at[pl.ds(b * seq + c * LN_CHUNK, LN_CHUNK)], sem_out.at[slot])

        def body(c, carry):
            slot = c % 2
            first = c * LN_CHUNK
            res = _layer_norm(ALPHA * _load_token_layout(x1v, first, LN_CHUNK)
                              + _load_token_layout(acc, first, LN_CHUNK), lng_ref[...], lnb_ref[...])

            @pl.when(c >= 2)
            def _():
                out_copy(c - 2, slot).wait()
            stage[slot] = res
            out_copy(c, slot).start()
            return carry
        lax.fori_loop(0, n_chunks, body, 0)
        out_copy(n_chunks - 2, (n_chunks - 2) % 2).wait()
        out_copy(n_chunks - 1, (n_chunks - 1) % 2).wait()


def _moe_pipelined_kernel(idx_ref, x1t_hbm, gate_ref, wg_ref, wu_ref, wd_ref, lng_ref, lnb_ref, out_hbm,
                          x1v, acc, xs3, y3, xs2d, yacc, stage, sem_in, sem_out, *, seq, cap):
    b = pl.program_id(0)
    e = pl.program_id(1)
    j = pl.program_id(2)
    tok_rows = seq * TOK_ROWS
    last_e = N_EXPERTS - 1

    def tok_slice(t):
        return pl.ds(pl.multiple_of(t * TOK_ROWS, TOK_ROWS), TOK_ROWS)

    def gather_rows(expert, lo, hi):
        base = (b * N_EXPERTS + expert) * cap
        for i in range(lo, hi):
            xs3[i * TOK_ROWS:(i + 1) * TOK_ROWS, :] = x1v[tok_slice(idx_ref[base + i]), :]

    def scatter_rows(expert, lo, hi):
        base = (b * N_EXPERTS + expert) * cap
        for g0 in range(lo, hi, SUBLANES):
            toks = [idx_ref[base + g0 + k] for k in range(SUBLANES)]
            vals = [acc[tok_slice(toks[k]), :] + y3[(g0 + k) * TOK_ROWS:(g0 + k + 1) * TOK_ROWS, :]
                    for k in range(SUBLANES)]
            for k in range(SUBLANES):
                acc[tok_slice(toks[k]), :] = vals[k]

    def expert_chunk(slot):
        xs = xs2d[slot]
        g = jnp.dot(xs, wg_ref[0, 0].astype(BF16), preferred_element_type=F32)
        u = jnp.dot(xs, wu_ref[0, 0].astype(BF16), preferred_element_type=F32)
        act = (g * jax.nn.sigmoid(g) * u).astype(BF16)
        return jnp.dot(act, wd_ref[0, 0].astype(BF16), preferred_element_type=F32)

    @pl.when((e == 0) & (j == 0))
    def _start_sequence():
        cp = pltpu.make_async_copy(x1t_hbm.at[pl.ds(b * tok_rows, tok_rows)], x1v, sem_in)
        cp.start()
        acc[...] = jnp.zeros_like(acc)
        y3[...] = jnp.zeros_like(y3)
        cp.wait()

        def body(i, carry):
            xs3[tok_slice(i), :] = x1v[tok_slice(idx_ref[b * N_EXPERTS * cap + i]), :]
            return carry
        lax.fori_loop(0, cap, body, 0, unroll=8)
        xs2d[0] = _load_token_layout(xs3, 0, cap).astype(BF16)

    @pl.when(j == 0)
    def _chunk0():
        scatter_rows(jnp.maximum(e - 1, 0), 0, cap)
        yacc[...] = expert_chunk(e % 2)

    @pl.when(j == 1)
    def _chunk1():
        nxt = jnp.minimum(e + 1, last_e)
        gather_rows(nxt, 0, cap)
        y = yacc[...] + expert_chunk(e % 2)
        gates = gate_ref[0]
        lane_e = lax.broadcasted_iota(jnp.int32, gates.shape, 1)
        gcol = jnp.sum(jnp.where(lane_e == e, gates, 0.0), axis=1, keepdims=True)
        _store_token_layout(y3, y * gcol)
        xs2d[(e + 1) % 2] = _load_token_layout(xs3, 0, cap).astype(BF16)

    @pl.when((e == last_e) & (j == 1))
    def _finalize():
        def body(gi, carry):
            base = (b * N_EXPERTS + last_e) * cap + gi * SUBLANES
            toks = [idx_ref[base + k] for k in range(SUBLANES)]
            vals = [acc[tok_slice(toks[k]), :] + y3[tok_slice(gi * SUBLANES + k), :]
                    for k in range(SUBLANES)]
            for k in range(SUBLANES):
                acc[tok_slice(toks[k]), :] = vals[k]
            return carry
        lax.fori_loop(0, cap // SUBLANES, body, 0)

        n_chunks = seq // LN_CHUNK

        def out_copy(c, slot):
            return pltpu.make_async_copy(
                stage.at[slot], out_hbm.at[pl.ds(b * seq + c * LN_CHUNK, LN_CHUNK)], sem_out.at[slot])

        def ln_body(c, carry):
            slot = c % 2
            first = c * LN_CHUNK
            res = _layer_norm(ALPHA * _load_token_layout(x1v, first, LN_CHUNK)
                              + _load_token_layout(acc, first, LN_CHUNK), lng_ref[...], lnb_ref[...])

            @pl.when(c >= 2)
            def _():
                out_copy(c - 2, slot).wait()
            stage[slot] = res
            out_copy(c, slot).start()
            return carry
        lax.fori_loop(0, n_chunks, ln_body, 0)
        out_copy(n_chunks - 2, (n_chunks - 2) % 2).wait()
        out_copy(n_chunks - 1, (n_chunks - 1) % 2).wait()


def _moe(x1t, idx, gate_t, w_gate_up, w_down, ln_g, ln_b, layer, batch, seq):
    cap = idx.shape[2]
    T = batch * seq
    fc = EXPERT_FF // FF_CHUNKS
    assert FF_CHUNKS == 2
    idx_flat = idx.reshape(-1)
    grid_spec = pltpu.PrefetchScalarGridSpec(
        num_scalar_prefetch=1,
        grid=(batch, N_EXPERTS, FF_CHUNKS),
        in_specs=[
            pl.BlockSpec(memory_space=pl.ANY),
            pl.BlockSpec((1, cap, N_EXPERTS), lambda b, e, j, idx: (b, 0, 0)),
            pl.BlockSpec((1, 1, D_MODEL, fc), lambda b, e, j, idx: (layer, e, 0, j)),
            pl.BlockSpec((1, 1, D_MODEL, fc), lambda b, e, j, idx: (layer, e, 0, FF_CHUNKS + j)),
            pl.BlockSpec((1, 1, fc, D_MODEL), lambda b, e, j, idx: (layer, e, j, 0)),
            pl.BlockSpec((1, D_MODEL), lambda b, e, j, idx: (0, 0)),
            pl.BlockSpec((1, D_MODEL), lambda b, e, j, idx: (0, 0)),
        ],
        out_specs=pl.BlockSpec(memory_space=pl.ANY),
        scratch_shapes=[
            pltpu.VMEM((seq * TOK_ROWS, LANES), F32),
            pltpu.VMEM((seq * TOK_ROWS, LANES), F32),
            pltpu.VMEM((cap * TOK_ROWS, LANES), F32),
            pltpu.VMEM((cap * TOK_ROWS, LANES), F32),
            pltpu.VMEM((2, cap, D_MODEL), BF16),
            pltpu.VMEM((cap, D_MODEL), F32),
            pltpu.VMEM((2, LN_CHUNK, D_MODEL), F32),
            pltpu.SemaphoreType.DMA(()),
            pltpu.SemaphoreType.DMA((2,)),
        ],
    )
    return pl.pallas_call(
        functools.partial(_moe_pipelined_kernel, seq=seq, cap=cap),
        grid_spec=grid_spec,
        out_shape=jax.ShapeDtypeStruct((T, D_MODEL), F32),
        compiler_params=pltpu.CompilerParams(
            dimension_semantics=("arbitrary", "arbitrary", "arbitrary"),
            vmem_limit_bytes=VMEM_LIMIT_MOE),
        name="expert_ffn_combine_ln",
    )(idx_flat, x1t, gate_t, w_gate_up, w_gate_up, w_down, ln_g, ln_b)


HALO = SUBLANES


def _gelu_tanh(x):
    return 0.5 * x * (1.0 + jnp.tanh(0.7978845608028654 * (x + 0.044715 * (x * x * x))))


def _lru_in_kernel(x_ref, xp_ref, xn_ref, w_ref, cw_ref, cb_ref, gg_ref, xc_ref, ext, *, nts):
    i = pl.program_id(0)
    tm = x_ref.shape[0]
    w = w_ref[...]
    acc = jnp.dot(x_ref[...].astype(BF16), w, preferred_element_type=F32)
    gg_ref[...] = _gelu_tanh(acc[:, :D_MODEL])
    wx = w[:, D_MODEL:]
    first = (i % nts) == 0
    last = (i % nts) == nts - 1
    xbp = jnp.dot(xp_ref[...].astype(BF16), wx, preferred_element_type=F32)
    xbn = jnp.dot(xn_ref[...].astype(BF16), wx, preferred_element_type=F32)
    ext[0:HALO, :] = jnp.where(first, 0.0, xbp)
    ext[HALO:HALO + tm, :] = acc[:, D_MODEL:]
    ext[HALO + tm:, :] = jnp.where(last, 0.0, xbn)
    xc = cb_ref[...]
    for tap in range(CONV_WIDTH):
        xc = xc + cw_ref[tap:tap + 1, :] * ext[HALO - 2 + tap:HALO - 2 + tap + tm, :]
    xc_ref[...] = xc


def _lru_in(x2d, w_in, conv_w, conv_b, seq):
    T = x2d.shape[0]
    tm = min(512, seq)
    nts = seq // tm
    hb = tm // HALO
    nblk = T // HALO
    return pl.pallas_call(
        functools.partial(_lru_in_kernel, nts=nts),
        grid=(T // tm,),
        in_specs=[
            pl.BlockSpec((tm, D_MODEL), lambda i: (i, 0)),
            pl.BlockSpec((HALO, D_MODEL), lambda i: (jnp.maximum(i * hb - 1, 0), 0)),
            pl.BlockSpec((HALO, D_MODEL), lambda i: (jnp.minimum((i + 1) * hb, nblk - 1), 0)),
            pl.BlockSpec((D_MODEL, 2 * D_MODEL), lambda i: (0, 0)),
            pl.BlockSpec((CONV_WIDTH, D_MODEL), lambda i: (0, 0)),
            pl.BlockSpec((1, D_MODEL), lambda i: (0, 0)),
        ],
        out_specs=[
            pl.BlockSpec((tm, D_MODEL), lambda i: (i, 0)),
            pl.BlockSpec((tm, D_MODEL), lambda i: (i, 0)),
        ],
        out_shape=[
            jax.ShapeDtypeStruct((T, D_MODEL), F32),
            jax.ShapeDtypeStruct((T, D_MODEL), F32),
        ],
        scratch_shapes=[pltpu.VMEM((tm + 2 * HALO, D_MODEL), F32)],
        compiler_params=pltpu.CompilerParams(dimension_semantics=("arbitrary",)),
        name="lru_in_conv",
    )(x2d, x2d, x2d, w_in, conv_w, conv_b)


def _lru_gates_scan(xc_ref, wg_ref, br_ref, bi_ref, lam_ref, carry, hbuf, abuf, *, reverse, seq):
    i = pl.program_id(1)
    tm = xc_ref.shape[0]
    nt = seq // tm
    tile = (nt - 1 - i) if reverse else i

    @pl.when(i == 0)
    def _():
        carry[...] = jnp.zeros_like(carry)

    xc = xc_ref[...]
    xcb = xc.astype(BF16)
    r_parts, i_parts = [], []
    for n in range(LRU_BLOCKS):
        res = jnp.dot(xcb[:, n * LRU_BLOCK_W:(n + 1) * LRU_BLOCK_W], wg_ref[n],
                      preferred_element_type=F32)
        r_parts.append(res[:, :LRU_BLOCK_W])
        i_parts.append(res[:, LRU_BLOCK_W:])
    r = jax.nn.sigmoid(jnp.concatenate(r_parts, axis=1) + br_ref[...])
    ig = jax.nn.sigmoid(jnp.concatenate(i_parts, axis=1) + bi_ref[...])
    log_a = -LRU_C * r * jax.nn.softplus(-lam_ref[...])
    a = jnp.exp(log_a)
    mult = jnp.sqrt(-jnp.tanh(log_a) * (1.0 + a * a))
    row = lax.broadcasted_iota(jnp.int32, (tm, 1), 0)
    tpos = tile * tm + row
    is_start = tpos == (seq - 1 if reverse else 0)
    mult = jnp.where(is_start, 1.0, mult)
    u = mult * ig * xc

    sub = row % SUBLANES
    for k in (1, 2, 4):
        if reverse:
            a_sh = pltpu.roll(a, tm - k, axis=0)
            u_sh = pltpu.roll(u, tm - k, axis=0)
            ok = sub < SUBLANES - k
        else:
            a_sh = pltpu.roll(a, k, axis=0)
            u_sh = pltpu.roll(u, k, axis=0)
            ok = sub >= k
        u = jnp.where(ok, a * u_sh + u, u)
        a = jnp.where(ok, a * a_sh, a)
    hbuf[...] = u
    abuf[...] = a

    n_groups = tm // SUBLANES
    edge = 0 if reverse else SUBLANES - 1

    def body(gi, c):
        g = (n_groups - 1 - gi) if reverse else gi
        rows = pl.ds(pl.multiple_of(g * SUBLANES, SUBLANES), SUBLANES)
        h = hbuf[rows, :] + abuf[rows, :] * c
        hbuf[rows, :] = h
        return jnp.broadcast_to(h[edge:edge + 1, :], h.shape)

    c0 = jnp.broadcast_to(carry[...], (SUBLANES, D_MODEL))
    c_end = lax.fori_loop(0, n_groups, body, c0)
    carry[...] = c_end[0:1, :]


def _lru_fwd_kernel(xc_ref, wg_ref, br_ref, bi_ref, lam_ref, h_ref, carry, hbuf, abuf, *, seq):
    _lru_gates_scan(xc_ref, wg_ref, br_ref, bi_ref, lam_ref, carry, hbuf, abuf, reverse=False, seq=seq)
    h_ref[...] = hbuf[...]


def _lru_bwd_out_kernel(xc_ref, wg_ref, br_ref, bi_ref, lam_ref, hf_ref, gg_ref, x_ref, wo_ref,
                        g_ref, b_ref, wr_ref, xo_ref, aff_ref, carry, hbuf, abuf, *, seq):
    _lru_gates_scan(xc_ref, wg_ref, br_ref, bi_ref, lam_ref, carry, hbuf, abuf, reverse=True, seq=seq)
    h = hf_ref[...] + hbuf[...]
    y = jnp.dot((h * gg_ref[...]).astype(BF16), wo_ref[...], preferred_element_type=F32)
    xn = _layer_norm(ALPHA * x_ref[...] + y, g_ref[...], b_ref[...])
    _store_token_layout(xo_ref, xn)
    aff_ref[0] = _router_affinity_t(xn, wr_ref[...])


LRU_TM = 256


def _lru_common_specs(tile_map):
    return [
        pl.BlockSpec((LRU_TM, D_MODEL), tile_map),
        pl.BlockSpec((LRU_BLOCKS, LRU_BLOCK_W, 2 * LRU_BLOCK_W), lambda b, i: (0, 0, 0)),
        pl.BlockSpec((1, D_MODEL), lambda b, i: (0, 0)),
        pl.BlockSpec((1, D_MODEL), lambda b, i: (0, 0)),
        pl.BlockSpec((1, D_MODEL), lambda b, i: (0, 0)),
    ]


def _lru_scratch():
    return [pltpu.VMEM((1, D_MODEL), F32), pltpu.VMEM((LRU_TM, D_MODEL), F32),
            pltpu.VMEM((LRU_TM, D_MODEL), F32)]


def _lru_forward(xc, wg, br, bi, lam, batch, seq):
    T = xc.shape[0]
    nt = seq // LRU_TM
    tile_map = lambda b, i: (b * nt + i, 0)
    return pl.pallas_call(
        functools.partial(_lru_fwd_kernel, seq=seq),
        grid=(batch, nt),
        in_specs=_lru_common_specs(tile_map),
        out_specs=pl.BlockSpec((LRU_TM, D_MODEL), tile_map),
        out_shape=jax.ShapeDtypeStruct((T, D_MODEL), F32),
        scratch_shapes=_lru_scratch(),
        compiler_params=pltpu.CompilerParams(dimension_semantics=("arbitrary", "arbitrary")),
        name="lru_forward_scan",
    )(xc, wg, br, bi, lam)


def _lru_backward_out(xc, wg, br, bi, lam, hf, gg, x2d, w_out, g, b, wr_t, batch, seq):
    T = xc.shape[0]
    nt = seq // LRU_TM
    tile_map = lambda b, i: (b * nt + nt - 1 - i, 0)
    row_spec = pl.BlockSpec((1, D_MODEL), lambda b, i: (0, 0))
    return pl.pallas_call(
        functools.partial(_lru_bwd_out_kernel, seq=seq),
        grid=(batch, nt),
        in_specs=_lru_common_specs(tile_map) + [
            pl.BlockSpec((LRU_TM, D_MODEL), tile_map),
            pl.BlockSpec((LRU_TM, D_MODEL), tile_map),
            pl.BlockSpec((LRU_TM, D_MODEL), tile_map),
            pl.BlockSpec((D_MODEL, D_MODEL), lambda b, i: (0, 0)),
            row_spec, row_spec,
            pl.BlockSpec((N_EXPERTS, D_MODEL), lambda b, i: (0, 0)),
        ],
        out_specs=[
            pl.BlockSpec((LRU_TM * TOK_ROWS, LANES), tile_map),
            pl.BlockSpec((1, N_EXPERTS, LRU_TM), lambda b, i: (b, 0, nt - 1 - i)),
        ],
        out_shape=[
            jax.ShapeDtypeStruct((T * TOK_ROWS, LANES), F32),
            jax.ShapeDtypeStruct((batch, N_EXPERTS, seq), F32),
        ],
        scratch_shapes=_lru_scratch(),
        compiler_params=pltpu.CompilerParams(dimension_semantics=("arbitrary", "arbitrary")),
        name="lru_backward_scan_out_ln_router",
    )(xc, wg, br, bi, lam, hf, gg, x2d, w_out, g, b, wr_t)


def _rotary_tables(seq):
    pos = jnp.arange(seq, dtype=F32)
    inv_freq = ROPE_THETA ** (-jnp.arange(0, ROT_DIM, 2, dtype=F32) / ROT_DIM)
    ang = pos[:, None] * inv_freq[None, :]
    cos, sin = jnp.cos(ang), jnp.sin(ang)
    pad = HEAD_DIM - ROT_DIM
    cos_h = jnp.concatenate([cos, cos, jnp.ones((seq, pad), F32)], axis=1)
    sin_h = jnp.concatenate([-sin, sin, jnp.zeros((seq, pad), F32)], axis=1)
    reps = LANES // HEAD_DIM
    return jnp.tile(cos_h, (1, reps)), jnp.tile(sin_h, (1, reps))


def _moe_block(x1t, aff_t, w_gate_up, w_down, ln_g, ln_b, layer, batch, seq):
    cap = CAPACITY_FACTOR * seq // N_EXPERTS
    idx, gate_t = _route(aff_t, cap)
    return _moe(x1t, idx, gate_t, w_gate_up, w_down, ln_g[layer].reshape(1, -1), ln_b[layer].reshape(1, -1),
                layer, batch, seq)


def kernel(x, attn_w_qkv, attn_w_o, attn_sink, lru_w_in, lru_conv_w, lru_conv_b, lru_w_rgate,
           lru_b_rgate, lru_w_igate, lru_b_igate, lru_lambda, lru_w_out, moe_w_router, moe_w_gate_up,
           moe_w_down, ln_mix_g, ln_mix_b, ln_ffn_g, ln_ffn_b):
    batch, seq, _ = x.shape
    x2d = x.reshape(batch * seq, D_MODEL)
    row = lambda v: v.reshape(1, -1)

    cos_t, sin_t = _rotary_tables(seq)
    q, kv = _qkv_proj(x2d, attn_w_qkv[0].astype(BF16), cos_t, sin_t, seq)
    o = _attention(q, kv, attn_sink[0], batch, seq)
    x1, aff_t = _proj_ln(o, attn_w_o[0].astype(BF16), x2d, row(ln_mix_g[0]), row(ln_mix_b[0]),
                         moe_w_router[0].T.astype(BF16), batch, seq)
    x2 = _moe_block(x1, aff_t, moe_w_gate_up, moe_w_down, ln_ffn_g, ln_ffn_b, 0, batch, seq)

    gg, xc = _lru_in(x2, lru_w_in[0].astype(BF16), lru_conv_w[0], row(lru_conv_b[0]), seq)
    wg = jnp.concatenate([lru_w_rgate[0], lru_w_igate[0]], axis=-1).astype(BF16)
    hf = _lru_forward(xc, wg[0], row(lru_b_rgate[0, 0]), row(lru_b_igate[0, 0]), row(lru_lambda[0, 0]),
                      batch, seq)
    x3, aff_t = _lru_backward_out(xc, wg[1], row(lru_b_rgate[0, 1]), row(lru_b_igate[0, 1]),
                                  row(lru_lambda[0, 1]), hf, gg, x2, lru_w_out[0].astype(BF16),
                                  row(ln_mix_g[1]), row(ln_mix_b[1]), moe_w_router[1].T.astype(BF16),
                                  batch, seq)
    x4 = _moe_block(x3, aff_t, moe_w_gate_up, moe_w_down, ln_ffn_g, ln_ffn_b, 1, batch, seq)
    return x4.reshape(batch, seq, D_MODEL)
```

```python
import functools

import jax
import jax.numpy as jnp
from jax import lax
from jax.experimental import pallas as pl
from jax.experimental.pallas import tpu as pltpu

F32 = jnp.float32
BF16 = jnp.bfloat16

D_MODEL = 1024
HEAD_DIM = 64
N_Q_HEADS = 16
N_KV_HEADS = 4
GQA_GROUP = 4
WINDOW = 128
BLOCK = 128
ROT_DIM = 16
ROPE_THETA = 500000.0
LRU_BLOCKS = 4
LRU_BLOCK_W = 256
CONV_WIDTH = 4
LRU_C = 8.0
N_EXPERTS = 16
EXPERT_FF = 1024
CAPACITY_FACTOR = 2
LN_EPS = 1e-5
DEPTH = 2
ALPHA = (2 * DEPTH) ** 0.25

LANES = 128
SUBLANES = 8
MXU_COLS = 256
TOK_ROWS = D_MODEL // LANES
VMEM_LIMIT_MOE = 61 * 1024 * 1024


def _layer_norm(v, g, b):
    mu = jnp.mean(v, axis=-1, keepdims=True)
    vc = v - mu
    var = jnp.mean(vc * vc, axis=-1, keepdims=True)
    return vc * lax.rsqrt(var + LN_EPS) * g + b


def _store_token_layout(dst_ref, val):
    n = val.shape[0]
    for s in range(TOK_ROWS):
        dst_ref[pl.ds(s, n, stride=TOK_ROWS), :] = val[:, s * LANES:(s + 1) * LANES]


def _load_token_layout(src_ref, first_token, n):
    base = first_token * TOK_ROWS
    return jnp.concatenate(
        [src_ref[pl.ds(base + s, n, stride=TOK_ROWS), :] for s in range(TOK_ROWS)], axis=1)


def _router_affinity_t(xn, wr_t):
    logits_t = lax.dot_general(wr_t, xn.astype(BF16), (((1,), (1,)), ((), ())),
                               preferred_element_type=F32)
    m = jnp.max(logits_t, axis=0, keepdims=True)
    p = jnp.exp(logits_t - m)
    return p / jnp.sum(p, axis=0, keepdims=True)


def _dot_by_columns(lhs, w_ref, first_col, n_cols):
    return [jnp.dot(lhs, w_ref[:, c:c + MXU_COLS], preferred_element_type=F32)
            for c in range(first_col, first_col + n_cols, MXU_COLS)]


LOG2E = 1.4426950408889634


def _qkv_kernel(x_ref, w_ref, cos_ref, sin_ref, q_ref, kv_ref):
    x = x_ref[...].astype(BF16)
    acc = jnp.dot(x, w_ref[...], preferred_element_type=F32)
    cosb = cos_ref[...]
    sinb = sin_ref[...]
    lane = lax.broadcasted_iota(jnp.int32, cosb.shape, 1)
    low = (lane % HEAD_DIM) < (ROT_DIM // 2)
    nq = N_Q_HEADS * HEAD_DIM
    nk = N_KV_HEADS * HEAD_DIM

    def rot(t):
        up = pltpu.roll(t, LANES - ROT_DIM // 2, axis=1)
        dn = pltpu.roll(t, ROT_DIM // 2, axis=1)
        return t * cosb + jnp.where(low, up, dn) * sinb

    qs = [rot(acc[:, c * LANES:(c + 1) * LANES]) * (LOG2E * HEAD_DIM ** -0.5) for c in range(nq // LANES)]
    q_ref[...] = jnp.concatenate(qs, axis=1).astype(BF16)
    ks = [rot(acc[:, nq + c * LANES:nq + (c + 1) * LANES]) for c in range(nk // LANES)]
    vs = [acc[:, nq + nk + c * LANES:nq + nk + (c + 1) * LANES] for c in range(nk // LANES)]
    swap = lambda t: pltpu.roll(t, HEAD_DIM, axis=1)
    kv_ref[...] = jnp.concatenate(
        ks + [swap(t) for t in ks] + vs + [swap(t) for t in vs], axis=1).astype(BF16)


def _qkv_proj(x2d, w_qkv, cos_t, sin_t, seq):
    T = x2d.shape[0]
    tm = min(512, seq)
    nq = N_Q_HEADS * HEAD_DIM
    nk = N_KV_HEADS * HEAD_DIM
    nts = seq // tm
    return pl.pallas_call(
        _qkv_kernel,
        grid=(T // tm,),
        in_specs=[
            pl.BlockSpec((tm, D_MODEL), lambda i: (i, 0)),
            pl.BlockSpec((D_MODEL, nq + 2 * nk), lambda i: (0, 0)),
            pl.BlockSpec((tm, LANES), lambda i: (i % nts, 0)),
            pl.BlockSpec((tm, LANES), lambda i: (i % nts, 0)),
        ],
        out_specs=[
            pl.BlockSpec((tm, nq), lambda i: (i, 0)),
            pl.BlockSpec((tm, 4 * nk), lambda i: (i, 0)),
        ],
        out_shape=[
            jax.ShapeDtypeStruct((T, nq), BF16),
            jax.ShapeDtypeStruct((T, 4 * nk), BF16),
        ],
        compiler_params=pltpu.CompilerParams(dimension_semantics=("arbitrary",)),
        name="qkv_proj",
    )(x2d, w_qkv, cos_t, sin_t)


QBLKS = 4


def _attn_kernel(sink_ref, q_ref, kvp_ref, kvc_ref, kvn_ref, o_ref, *, seq):
    i = pl.program_id(1)
    nk = N_KV_HEADS * HEAD_DIM
    kv = jnp.concatenate([kvp_ref[...], kvc_ref[...], kvn_ref[...]], axis=0)
    band = 3 * BLOCK
    lo = lax.broadcasted_iota(jnp.int32, (1, LANES), 1) < HEAD_DIM
    zero = jnp.zeros((), BF16)
    k_lo, k_hi, v_dup = [], [], []
    for h in range(N_KV_HEADS):
        grp, half = h // 2, h % 2

        def block(base, swapped):
            col = base + (nk if swapped else 0) + grp * LANES
            return kv[:, col:col + LANES]
        k_lo.append(jnp.where(lo, block(0, half == 1), zero))
        k_hi.append(jnp.where(lo, zero, block(0, half == 0)))
        v_dup.append(jnp.where(lo, block(2 * nk, half == 1), block(2 * nk, half == 0)))

    r = lax.broadcasted_iota(jnp.int32, (BLOCK, BLOCK), 0)
    c = lax.broadcasted_iota(jnp.int32, (BLOCK, BLOCK), 1)
    units = [(j, h) for j in range(QBLKS) for h in range(N_KV_HEADS)]

    def scores(j, h):
        rows = slice(j * BLOCK, (j + 1) * BLOCK)
        qp = jnp.concatenate([q_ref[rows, (2 * h) * LANES:(2 * h + 1) * LANES],
                              q_ref[rows, (2 * h + 1) * LANES:(2 * h + 2) * LANES]], axis=0)
        keys = jnp.concatenate([k_lo[h][j * BLOCK:j * BLOCK + band], k_hi[h][j * BLOCK:j * BLOCK + band]],
                               axis=0)
        return lax.dot_general(qp, keys, (((1,), (1,)), ((), ())), preferred_element_type=F32)

    s_next = scores(*units[0])
    for u, (j, h) in enumerate(units):
        s = s_next
        if u + 1 < len(units):
            s_next = scores(*units[u + 1])
        n = i * QBLKS + j
        ok_prev = jnp.concatenate([(c >= r) & (n > 0)] * 2, axis=0)
        ok_next = jnp.concatenate([(c <= r) & (n < seq // BLOCK - 1)] * 2, axis=0)
        vb = v_dup[h][j * BLOCK:j * BLOCK + band]
        halves = []
        for par in range(2):
            sp = s[:, par * band:(par + 1) * band]
            sp = jnp.concatenate([jnp.where(ok_prev, sp[:, :BLOCK], -jnp.inf), sp[:, BLOCK:2 * BLOCK],
                                  jnp.where(ok_next, sp[:, 2 * BLOCK:], -jnp.inf)], axis=1)
            sink_col = jnp.concatenate(
                [jnp.full((BLOCK, 1), sink_ref[h * GQA_GROUP + g] * LOG2E, F32) for g in (par, par + 2)],
                axis=0)
            m = jnp.maximum(jnp.max(sp, axis=1, keepdims=True), sink_col)
            p = jnp.exp2(sp - m)
            denom = jnp.sum(p, axis=1, keepdims=True) + jnp.exp2(sink_col - m)
            halves.append(jnp.dot(p.astype(BF16), vb, preferred_element_type=F32) / denom)
        out = jnp.where(lo, halves[0], halves[1])
        o_ref[j * BLOCK:(j + 1) * BLOCK, (2 * h) * LANES:(2 * h + 2) * LANES] = jnp.concatenate(
            [out[:BLOCK], out[BLOCK:]], axis=1).astype(BF16)


def _attention(q, kv, sink, batch, seq):
    T = q.shape[0]
    qt = QBLKS * BLOCK
    nq = seq // qt
    nb = seq // BLOCK
    wkv = kv.shape[1]
    cur = lambda b, i: (b * nq + i, 0)
    prev = lambda b, i: (b * nb + jnp.maximum(i * QBLKS - 1, 0), 0)
    nxt = lambda b, i: (b * nb + jnp.minimum(i * QBLKS + QBLKS, nb - 1), 0)
    return pl.pallas_call(
        functools.partial(_attn_kernel, seq=seq),
        grid=(batch, nq),
        in_specs=[
            pl.BlockSpec(memory_space=pltpu.SMEM),
            pl.BlockSpec((qt, D_MODEL), cur),
            pl.BlockSpec((BLOCK, wkv), prev),
            pl.BlockSpec((qt, wkv), cur),
            pl.BlockSpec((BLOCK, wkv), nxt),
        ],
        out_specs=pl.BlockSpec((qt, D_MODEL), cur),
        out_shape=jax.ShapeDtypeStruct((T, D_MODEL), BF16),
        compiler_params=pltpu.CompilerParams(dimension_semantics=("arbitrary", "arbitrary")),
        name="swa_attention",
    )(sink, q, kv, kv, kv)


def _proj_ln_kernel(a_ref, w_ref, x_ref, g_ref, b_ref, wr_ref, xo_ref, aff_ref):
    y = jnp.dot(a_ref[...], w_ref[...], preferred_element_type=F32)
    xn = _layer_norm(ALPHA * x_ref[...] + y, g_ref[...], b_ref[...])
    _store_token_layout(xo_ref, xn)
    aff_ref[0] = _router_affinity_t(xn, wr_ref[...])


def _proj_ln(a, w, x2d, g, b, wr_t, batch, seq):
    T = x2d.shape[0]
    tm = min(512, seq)
    nts = seq // tm
    return pl.pallas_call(
        _proj_ln_kernel,
        grid=(T // tm,),
        in_specs=[
            pl.BlockSpec((tm, D_MODEL), lambda i: (i, 0)),
            pl.BlockSpec((D_MODEL, D_MODEL), lambda i: (0, 0)),
            pl.BlockSpec((tm, D_MODEL), lambda i: (i, 0)),
            pl.BlockSpec((1, D_MODEL), lambda i: (0, 0)),
            pl.BlockSpec((1, D_MODEL), lambda i: (0, 0)),
            pl.BlockSpec((N_EXPERTS, D_MODEL), lambda i: (0, 0)),
        ],
        out_specs=[
            pl.BlockSpec((tm * TOK_ROWS, LANES), lambda i: (i, 0)),
            pl.BlockSpec((1, N_EXPERTS, tm), lambda i: (i // nts, 0, i % nts)),
        ],
        out_shape=[
            jax.ShapeDtypeStruct((T * TOK_ROWS, LANES), F32),
            jax.ShapeDtypeStruct((batch, N_EXPERTS, seq), F32),
        ],
        compiler_params=pltpu.CompilerParams(dimension_semantics=("arbitrary",)),
        name="proj_ln_router",
    )(a, w, x2d, g, b, wr_t)


SLOT_BLOCK = 64


def _lane_prefix(mask, tri, dst_ref):
    seq = mask.shape[1]
    off = jnp.zeros((mask.shape[0], 1), F32)
    mb = mask.astype(BF16)
    ends = []
    for c in range(seq // LANES):
        res = jnp.dot(mb[:, c * LANES:(c + 1) * LANES], tri, preferred_element_type=F32) + off
        dst_ref[:, c * LANES:(c + 1) * LANES] = res
        off = res[:, LANES - 1:LANES]
        ends.append(off)
    return ends


def _select_kernel(aff_ref, pos_ref, bnd_ref, *, cap):
    aff = aff_ref[0]
    as_float = lambda bits: pltpu.bitcast(bits, F32)
    cur = jnp.zeros((N_EXPERTS, 1), jnp.int32)
    for bit in range(30, -1, -1):
        cand = cur | (1 << bit)
        cnt = jnp.sum((aff >= as_float(cand)).astype(jnp.int32), axis=1, keepdims=True)
        cur = jnp.where(cnt >= cap, cand, cur)
    gt = aff >= as_float(cur + 1)
    eq = (aff >= as_float(cur)) & jnp.logical_not(gt)
    n_gt = jnp.sum(gt.astype(jnp.int32), axis=1, keepdims=True)
    ties_taken = (cap - n_gt).astype(F32)
    ri = lax.broadcasted_iota(jnp.int32, (LANES, LANES), 0)
    ci = lax.broadcasted_iota(jnp.int32, (LANES, LANES), 1)
    tri = (ri <= ci).astype(BF16)
    _lane_prefix(eq, tri, pos_ref.at[0])
    sel = gt | (eq & (pos_ref[0] <= ties_taken))
    ends = _lane_prefix(sel, tri, pos_ref.at[0])
    pos_ref[0] = jnp.where(sel, pos_ref[0], 0.0)
    cols = []
    for sb in range(cap // SLOT_BLOCK):
        for target in (sb * SLOT_BLOCK + 1, (sb + 1) * SLOT_BLOCK):
            blk = jnp.zeros((N_EXPERTS, 1), jnp.int32)
            for end in ends[:-1]:
                blk = blk + (end < target).astype(jnp.int32)
            cols.append(blk)
    bnd_ref[0] = jnp.concatenate(cols, axis=1)


def _compact_kernel(bnd_ref, pos_ref, aff_ref, idx_ref, gate_ref, ptok, pgate, *, cap):
    b = pl.program_id(0)
    n_sb = cap // SLOT_BLOCK
    groups = SLOT_BLOCK // SUBLANES
    lane = lax.broadcasted_iota(jnp.int32, (SUBLANES, LANES), 1).astype(F32)
    sub1 = (lax.broadcasted_iota(jnp.int32, (SUBLANES, 1), 0) + 1).astype(F32)
    lane_e = lax.broadcasted_iota(jnp.int32, (cap, N_EXPERTS), 1)
    idx_ref[0] = jnp.zeros((cap, N_EXPERTS), jnp.int32)
    gate_ref[0] = jnp.zeros((cap, N_EXPERTS), F32)

    def expert(e, carry):
        def slot_block(sb, carry2):
            base = ((b * N_EXPERTS + e) * n_sb + sb) * 2
            slot0 = jnp.asarray(sb * SLOT_BLOCK, F32)

            def lane_block(c, accs):
                off = pl.multiple_of(c * LANES, LANES)
                p = jnp.broadcast_to(pos_ref[0, e:e + 1, pl.ds(off, LANES)], (SUBLANES, LANES))
                a = jnp.broadcast_to(aff_ref[0, e:e + 1, pl.ds(off, LANES)], (SUBLANES, LANES))
                tokv = lane + jnp.asarray(c * LANES, F32)
                out = []
                for g in range(groups):
                    hit = p == (slot0 + (g * SUBLANES) + sub1)
                    out.append(accs[2 * g] + jnp.where(hit, tokv, 0.0))
                    out.append(accs[2 * g + 1] + jnp.where(hit, a, 0.0))
                return tuple(out)

            zeros = tuple(jnp.zeros((SUBLANES, LANES), F32) for _ in range(2 * groups))
            accs = lax.fori_loop(bnd_ref[base], bnd_ref[base + 1] + 1, lane_block, zeros)
            for g in range(groups):
                rows = pl.ds(pl.multiple_of(sb * SLOT_BLOCK + g * SUBLANES, SUBLANES), SUBLANES)
                ptok[rows, :] = accs[2 * g]
                pgate[rows, :] = accs[2 * g + 1]
            return carry2

        lax.fori_loop(0, n_sb, slot_block, 0)
        tok_col = jnp.sum(ptok[...], axis=1, keepdims=True).astype(jnp.int32)
        gate_col = jnp.sum(pgate[...], axis=1, keepdims=True)
        idx_ref[0] = jnp.where(lane_e == e, tok_col, idx_ref[0])
        gate_ref[0] = jnp.where(lane_e == e, gate_col, gate_ref[0])
        return carry

    for e in range(N_EXPERTS):
        expert(e, 0)


def _route(aff_t, cap):
    batch, _, seq = aff_t.shape
    n_bnd = 2 * (cap // SLOT_BLOCK)
    seq_spec = pl.BlockSpec((1, N_EXPERTS, seq), lambda b, *_: (b, 0, 0))
    pos, bnd = pl.pallas_call(
        functools.partial(_select_kernel, cap=cap),
        grid=(batch,),
        in_specs=[seq_spec],
        out_specs=[seq_spec, pl.BlockSpec((1, N_EXPERTS, n_bnd), lambda b: (b, 0, 0))],
        out_shape=[
            jax.ShapeDtypeStruct((batch, N_EXPERTS, seq), F32),
            jax.ShapeDtypeStruct((batch, N_EXPERTS, n_bnd), jnp.int32),
        ],
        compiler_params=pltpu.CompilerParams(dimension_semantics=("arbitrary",)),
        name="expert_choice_select",
    )(aff_t)
    slot_spec = pl.BlockSpec((1, cap, N_EXPERTS), lambda b, *_: (b, 0, 0))
    idx_t, gate_t = pl.pallas_call(
        functools.partial(_compact_kernel, cap=cap),
        grid_spec=pltpu.PrefetchScalarGridSpec(
            num_scalar_prefetch=1,
            grid=(batch,),
            in_specs=[seq_spec, seq_spec],
            out_specs=[slot_spec, slot_spec],
            scratch_shapes=[pltpu.VMEM((cap, LANES), F32), pltpu.VMEM((cap, LANES), F32)],
        ),
        out_shape=[
            jax.ShapeDtypeStruct((batch, cap, N_EXPERTS), jnp.int32),
            jax.ShapeDtypeStruct((batch, cap, N_EXPERTS), F32),
        ],
        compiler_params=pltpu.CompilerParams(dimension_semantics=("arbitrary",)),
        name="expert_choice_compact",
    )(bnd.reshape(-1), pos, aff_t)
    return jnp.swapaxes(idx_t, 1, 2), gate_t


FF_CHUNKS = 2
LN_CHUNK = 256


def _moe_kernel(idx_ref, x1t_hbm, gate_ref, wg_ref, wu_ref, wd_ref, lng_ref, lnb_ref, out_hbm,
                x1v, acc, xs3, y3, xs2d, yacc, stage, sem_in, sem_out, *, seq, cap):
    b = pl.program_id(0)
    e = pl.program_id(1)
    j = pl.program_id(2)
    tok_rows = seq * TOK_ROWS
    last_e = N_EXPERTS - 1

    def tok_slice(t):
        return pl.ds(pl.multiple_of(t * TOK_ROWS, TOK_ROWS), TOK_ROWS)

    def gather_rows(expert):
        base = (b * N_EXPERTS + expert) * cap
        for i in range(cap):
            xs3[i * TOK_ROWS:(i + 1) * TOK_ROWS, :] = x1v[tok_slice(idx_ref[base + i]), :]

    def scatter_rows(expert):
        base = (b * N_EXPERTS + expert) * cap
        for g0 in range(0, cap, SUBLANES):
            toks = [idx_ref[base + g0 + k] for k in range(SUBLANES)]
            vals = [acc[tok_slice(toks[k]), :] + y3[(g0 + k) * TOK_ROWS:(g0 + k + 1) * TOK_ROWS, :]
                    for k in range(SUBLANES)]
            for k in range(SUBLANES):
                acc[tok_slice(toks[k]), :] = vals[k]

    def expert_chunk(slot):
        xs = xs2d[slot]
        g = jnp.dot(xs, wg_ref[0, 0].astype(BF16), preferred_element_type=F32)
        u = jnp.dot(xs, wu_ref[0, 0].astype(BF16), preferred_element_type=F32)
        act = (g * jax.nn.sigmoid(g) * u).astype(BF16)
        return jnp.dot(act, wd_ref[0, 0].astype(BF16), preferred_element_type=F32)

    @pl.when((e == 0) & (j == 0))
    def _start_sequence():
        cp = pltpu.make_async_copy(x1t_hbm.at[pl.ds(b * tok_rows, tok_rows)], x1v, sem_in)
        cp.start()
        acc[...] = jnp.zeros_like(acc)
        y3[...] = jnp.zeros_like(y3)
        cp.wait()

        def body(i, carry):
            xs3[tok_slice(i), :] = x1v[tok_slice(idx_ref[b * N_EXPERTS * cap + i]), :]
            return carry
        lax.fori_loop(0, cap, body, 0, unroll=8)
        xs2d[0] = _load_token_layout(xs3, 0, cap).astype(BF16)

    @pl.when(j == 0)
    def _chunk0():
        scatter_rows(jnp.maximum(e - 1, 0))
        yacc[...] = expert_chunk(e % 2)

    @pl.when(j == 1)
    def _chunk1():
        gather_rows(jnp.minimum(e + 1, last_e))
        y = yacc[...] + expert_chunk(e % 2)
        gates = gate_ref[0]
        lane_e = lax.broadcasted_iota(jnp.int32, gates.shape, 1)
        gcol = jnp.sum(jnp.where(lane_e == e, gates, 0.0), axis=1, keepdims=True)
        _store_token_layout(y3, y * gcol)
        xs2d[(e + 1) % 2] = _load_token_layout(xs3, 0, cap).astype(BF16)

    @pl.when((e == last_e) & (j == 1))
    def _finalize():
        def body(gi, carry):
            base = (b * N_EXPERTS + last_e) * cap + gi * SUBLANES
            toks = [idx_ref[base + k] for k in range(SUBLANES)]
            vals = [acc[tok_slice(toks[k]), :] + y3[tok_slice(gi * SUBLANES + k), :]
                    for k in range(SUBLANES)]
            for k in range(SUBLANES):
                acc[tok_slice(toks[k]), :] = vals[k]
            return carry
        lax.fori_loop(0, cap // SUBLANES, body, 0)

        n_chunks = seq // LN_CHUNK

        def out_copy(c, slot):
            return pltpu.make_async_copy(
                stage.at[slot], out_hbm.at[pl.ds(b * seq + c * LN_CHUNK, LN_CHUNK)], sem_out.at[slot])

        def ln_body(c, carry):
            slot = c % 2
            first = c * LN_CHUNK
            res = _layer_norm(ALPHA * _load_token_layout(x1v, first, LN_CHUNK)
                              + _load_token_layout(acc, first, LN_CHUNK), lng_ref[...], lnb_ref[...])

            @pl.when(c >= 2)
            def _():
                out_copy(c - 2, slot).wait()
            stage[slot] = res
            out_copy(c, slot).start()
            return carry
        lax.fori_loop(0, n_chunks, ln_body, 0)
        out_copy(n_chunks - 2, (n_chunks - 2) % 2).wait()
        out_copy(n_chunks - 1, (n_chunks - 1) % 2).wait()


def _moe(x1t, idx, gate_t, w_gate_up, w_down, ln_g, ln_b, layer, batch, seq):
    cap = idx.shape[2]
    T = batch * seq
    fc = EXPERT_FF // FF_CHUNKS
    assert FF_CHUNKS == 2
    idx_flat = idx.reshape(-1)
    grid_spec = pltpu.PrefetchScalarGridSpec(
        num_scalar_prefetch=1,
        grid=(batch, N_EXPERTS, FF_CHUNKS),
        in_specs=[
            pl.BlockSpec(memory_space=pl.ANY),
            pl.BlockSpec((1, cap, N_EXPERTS), lambda b, e, j, idx: (b, 0, 0)),
            pl.BlockSpec((1, 1, D_MODEL, fc), lambda b, e, j, idx: (layer, e, 0, j)),
            pl.BlockSpec((1, 1, D_MODEL, fc), lambda b, e, j, idx: (layer, e, 0, FF_CHUNKS + j)),
            pl.BlockSpec((1, 1, fc, D_MODEL), lambda b, e, j, idx: (layer, e, j, 0)),
            pl.BlockSpec((1, D_MODEL), lambda b, e, j, idx: (0, 0)),
            pl.BlockSpec((1, D_MODEL), lambda b, e, j, idx: (0, 0)),
        ],
        out_specs=pl.BlockSpec(memory_space=pl.ANY),
        scratch_shapes=[
            pltpu.VMEM((seq * TOK_ROWS, LANES), F32),
            pltpu.VMEM((seq * TOK_ROWS, LANES), F32),
            pltpu.VMEM((cap * TOK_ROWS, LANES), F32),
            pltpu.VMEM((cap * TOK_ROWS, LANES), F32),
            pltpu.VMEM((2, cap, D_MODEL), BF16),
            pltpu.VMEM((cap, D_MODEL), F32),
            pltpu.VMEM((2, LN_CHUNK, D_MODEL), F32),
            pltpu.SemaphoreType.DMA(()),
            pltpu.SemaphoreType.DMA((2,)),
        ],
    )
    return pl.pallas_call(
        functools.partial(_moe_kernel, seq=seq, cap=cap),
        grid_spec=grid_spec,
        out_shape=jax.ShapeDtypeStruct((T, D_MODEL), F32),
        compiler_params=pltpu.CompilerParams(
            dimension_semantics=("arbitrary", "arbitrary", "arbitrary"),
            vmem_limit_bytes=VMEM_LIMIT_MOE),
        name="expert_ffn_combine_ln",
    )(idx_flat, x1t, gate_t, w_gate_up, w_gate_up, w_down, ln_g, ln_b)


HALO = SUBLANES


def _gelu_tanh(x):
    return 0.5 * x * (1.0 + jnp.tanh(0.7978845608028654 * (x + 0.044715 * (x * x * x))))


def _lru_in_kernel(x_ref, xp_ref, xn_ref, w_ref, cw_ref, cb_ref, gg_ref, xc_ref, *, nts):
    i = pl.program_id(0)
    tm = x_ref.shape[0]
    x = x_ref[...].astype(BF16)
    xp = xp_ref[...].astype(BF16)
    xn = xn_ref[...].astype(BF16)
    first = (i % nts) == 0
    last = (i % nts) == nts - 1
    for k, gate in enumerate(_dot_by_columns(x, w_ref, 0, D_MODEL)):
        gg_ref[:, k * MXU_COLS:(k + 1) * MXU_COLS] = _gelu_tanh(gate)
    n_ext = tm + 2 * HALO
    for k in range(D_MODEL // MXU_COLS):
        cols = slice(k * MXU_COLS, (k + 1) * MXU_COLS)
        xb, = _dot_by_columns(x, w_ref, D_MODEL + k * MXU_COLS, MXU_COLS)
        xbp, = _dot_by_columns(xp, w_ref, D_MODEL + k * MXU_COLS, MXU_COLS)
        xbn, = _dot_by_columns(xn, w_ref, D_MODEL + k * MXU_COLS, MXU_COLS)
        ext = jnp.concatenate([jnp.where(first, 0.0, xbp), xb, jnp.where(last, 0.0, xbn)], axis=0)
        xc = cb_ref[:, cols]
        for tap in range(CONV_WIDTH):
            shift = (2 - tap) % n_ext
            src = ext if shift == 0 else pltpu.roll(ext, shift, axis=0)
            xc = xc + cw_ref[tap:tap + 1, cols] * src[HALO:HALO + tm, :]
        xc_ref[:, cols] = xc


def _lru_in(x2d, w_in, conv_w, conv_b, seq):
    T = x2d.shape[0]
    tm = min(512, seq)
    nts = seq // tm
    hb = tm // HALO
    nblk = T // HALO
    return pl.pallas_call(
        functools.partial(_lru_in_kernel, nts=nts),
        grid=(T // tm,),
        in_specs=[
            pl.BlockSpec((tm, D_MODEL), lambda i: (i, 0)),
            pl.BlockSpec((HALO, D_MODEL), lambda i: (jnp.maximum(i * hb - 1, 0), 0)),
            pl.BlockSpec((HALO, D_MODEL), lambda i: (jnp.minimum((i + 1) * hb, nblk - 1), 0)),
            pl.BlockSpec((D_MODEL, 2 * D_MODEL), lambda i: (0, 0)),
            pl.BlockSpec((CONV_WIDTH, D_MODEL), lambda i: (0, 0)),
            pl.BlockSpec((1, D_MODEL), lambda i: (0, 0)),
        ],
        out_specs=[
            pl.BlockSpec((tm, D_MODEL), lambda i: (i, 0)),
            pl.BlockSpec((tm, D_MODEL), lambda i: (i, 0)),
        ],
        out_shape=[
            jax.ShapeDtypeStruct((T, D_MODEL), F32),
            jax.ShapeDtypeStruct((T, D_MODEL), F32),
        ],
        compiler_params=pltpu.CompilerParams(dimension_semantics=("arbitrary",)),
        name="lru_in_conv",
    )(x2d, x2d, x2d, w_in, conv_w, conv_b)


SCAN_TM = 128
CHAIN_PITCH = SCAN_TM + SUBLANES
PAIRS = D_MODEL // (2 * LANES)


def _lru_chain_scan(xc_ref, wg_ref, br_ref, bi_ref, lam_ref, carry, abuf, ubuf, hbuf, *, reverse, seq):
    i = pl.program_id(0)
    nseq, tm = xc_ref.shape[0], xc_ref.shape[1]
    assert 2 * nseq == SUBLANES and tm == SCAN_TM
    nt = seq // tm
    tile = (nt - 1 - i) if reverse else i

    @pl.when(i == 0)
    def _():
        carry[...] = jnp.zeros_like(carry)

    half_nsp = (-0.5 * LRU_C) * jax.nn.softplus(-lam_ref[...])
    row = lax.broadcasted_iota(jnp.int32, (tm, 1), 0)
    is_start = (tile * tm + row) == (seq - 1 if reverse else 0)
    for b in range(nseq):
        xc = xc_ref[b]
        xcb = xc.astype(BF16)
        r_parts, i_parts = [], []
        for n in range(LRU_BLOCKS):
            res = jnp.dot(xcb[:, n * LRU_BLOCK_W:(n + 1) * LRU_BLOCK_W], wg_ref[n],
                          preferred_element_type=F32)
            r_parts.append(res[:, :LRU_BLOCK_W])
            i_parts.append(res[:, LRU_BLOCK_W:])
        tr = jnp.tanh(jnp.concatenate(r_parts, axis=1) + br_ref[...])
        ig = 0.5 * jnp.tanh(jnp.concatenate(i_parts, axis=1) + bi_ref[...]) + 0.5
        log_a = tr * half_nsp + half_nsp
        a = jnp.exp(log_a)
        z = -jnp.tanh(log_a) * (1.0 + a * a)
        root = jnp.where(z > 0.0, z * lax.rsqrt(z), 0.0)
        u = jnp.where(is_start, 1.0, root) * ig * xc
        for lb in range(2 * PAIRS):
            rows = slice((2 * b + lb % 2) * CHAIN_PITCH, (2 * b + lb % 2) * CHAIN_PITCH + tm)
            abuf[lb // 2, rows, :] = a[:, lb * LANES:(lb + 1) * LANES]
            ubuf[lb // 2, rows, :] = u[:, lb * LANES:(lb + 1) * LANES]

    def step(k, hs):
        t = (tm - 1 - k) if reverse else k
        chains = pl.ds(t, SUBLANES, stride=CHAIN_PITCH)
        out = []
        for m in range(PAIRS):
            h = abuf[m, chains, :] * hs[m] + ubuf[m, chains, :]
            hbuf[m, chains, :] = h
            out.append(h)
        return tuple(out)

    hs = lax.fori_loop(0, tm, step, tuple(carry[m] for m in range(PAIRS)), unroll=8)
    for m in range(PAIRS):
        carry[m] = hs[m]


def _chain_tile(hbuf, b, tm):
    return jnp.concatenate(
        [hbuf[lb // 2, (2 * b + lb % 2) * CHAIN_PITCH:(2 * b + lb % 2) * CHAIN_PITCH + tm, :]
         for lb in range(2 * PAIRS)], axis=1)


def _lru_fwd_kernel(xc_ref, wg_ref, br_ref, bi_ref, lam_ref, h_ref, carry, abuf, ubuf, hbuf, *, seq):
    _lru_chain_scan(xc_ref, wg_ref, br_ref, bi_ref, lam_ref, carry, abuf, ubuf, hbuf,
                    reverse=False, seq=seq)
    for b in range(h_ref.shape[0]):
        h_ref[b] = _chain_tile(hbuf, b, h_ref.shape[1])


def _lru_bwd_kernel(xc_ref, wg_ref, br_ref, bi_ref, lam_ref, hf_ref, gg_ref, x_ref, wo_ref,
                    g_ref, b_ref, wr_ref, xo_ref, aff_ref, carry, abuf, ubuf, hbuf, *, seq):
    _lru_chain_scan(xc_ref, wg_ref, br_ref, bi_ref, lam_ref, carry, abuf, ubuf, hbuf,
                    reverse=True, seq=seq)
    tm = x_ref.shape[1]
    for b in range(x_ref.shape[0]):
        h = hf_ref[b] + _chain_tile(hbuf, b, tm)
        y = jnp.dot((h * gg_ref[b]).astype(BF16), wo_ref[...], preferred_element_type=F32)
        xn = _layer_norm(ALPHA * x_ref[b] + y, g_ref[...], b_ref[...])
        _store_token_layout(xo_ref.at[b], xn)
        aff_ref[b] = _router_affinity_t(xn, wr_ref[...])


def _lru_specs(batch, tile_map):
    const2 = lambda i: (0, 0)
    return [
        pl.BlockSpec((batch, SCAN_TM, D_MODEL), tile_map),
        pl.BlockSpec((LRU_BLOCKS, LRU_BLOCK_W, 2 * LRU_BLOCK_W), lambda i: (0, 0, 0)),
        pl.BlockSpec((1, D_MODEL), const2),
        pl.BlockSpec((1, D_MODEL), const2),
        pl.BlockSpec((1, D_MODEL), const2),
    ]


def _lru_scratch():
    buf = pltpu.VMEM((PAIRS, SUBLANES * CHAIN_PITCH, LANES), F32)
    return [pltpu.VMEM((PAIRS, SUBLANES, LANES), F32), buf, buf, buf]


def _lru_forward(xc3, wg, br, bi, lam):
    batch, seq, _ = xc3.shape
    tile_map = lambda i: (0, i, 0)
    return pl.pallas_call(
        functools.partial(_lru_fwd_kernel, seq=seq),
        grid=(seq // SCAN_TM,),
        in_specs=_lru_specs(batch, tile_map),
        out_specs=pl.BlockSpec((batch, SCAN_TM, D_MODEL), tile_map),
        out_shape=jax.ShapeDtypeStruct((batch, seq, D_MODEL), F32),
        scratch_shapes=_lru_scratch(),
        compiler_params=pltpu.CompilerParams(dimension_semantics=("arbitrary",)),
        name="lru_forward_scan",
    )(xc3, wg, br, bi, lam)


def _lru_backward_out(xc3, wg, br, bi, lam, hf3, gg3, x3d, w_out, g, b, wr_t):
    batch, seq, _ = xc3.shape
    nt = seq // SCAN_TM
    tile_map = lambda i: (0, nt - 1 - i, 0)
    const2 = lambda i: (0, 0)
    tile = pl.BlockSpec((batch, SCAN_TM, D_MODEL), tile_map)
    return pl.pallas_call(
        functools.partial(_lru_bwd_kernel, seq=seq),
        grid=(nt,),
        in_specs=_lru_specs(batch, tile_map) + [
            tile, tile, tile,
            pl.BlockSpec((D_MODEL, D_MODEL), const2),
            pl.BlockSpec((1, D_MODEL), const2),
            pl.BlockSpec((1, D_MODEL), const2),
            pl.BlockSpec((N_EXPERTS, D_MODEL), const2),
        ],
        out_specs=[
            pl.BlockSpec((batch, SCAN_TM * TOK_ROWS, LANES), tile_map),
            pl.BlockSpec((batch, N_EXPERTS, SCAN_TM), lambda i: (0, 0, nt - 1 - i)),
        ],
        out_shape=[
            jax.ShapeDtypeStruct((batch, seq * TOK_ROWS, LANES), F32),
            jax.ShapeDtypeStruct((batch, N_EXPERTS, seq), F32),
        ],
        scratch_shapes=_lru_scratch(),
        compiler_params=pltpu.CompilerParams(dimension_semantics=("arbitrary",)),
        name="lru_backward_scan_out_ln_router",
    )(xc3, wg, br, bi, lam, hf3, gg3, x3d, w_out, g, b, wr_t)


def _rotary_tables(seq):
    pos = jnp.arange(seq, dtype=F32)
    inv_freq = ROPE_THETA ** (-jnp.arange(0, ROT_DIM, 2, dtype=F32) / ROT_DIM)
    ang = pos[:, None] * inv_freq[None, :]
    cos, sin = jnp.cos(ang), jnp.sin(ang)
    pad = HEAD_DIM - ROT_DIM
    cos_h = jnp.concatenate([cos, cos, jnp.ones((seq, pad), F32)], axis=1)
    sin_h = jnp.concatenate([-sin, sin, jnp.zeros((seq, pad), F32)], axis=1)
    reps = LANES // HEAD_DIM
    return jnp.tile(cos_h, (1, reps)), jnp.tile(sin_h, (1, reps))


def _moe_block(x1t, aff_t, w_gate_up, w_down, ln_g, ln_b, layer, batch, seq):
    cap = CAPACITY_FACTOR * seq // N_EXPERTS
    idx, gate_t = _route(aff_t, cap)
    return _moe(x1t, idx, gate_t, w_gate_up, w_down, ln_g[layer].reshape(1, -1), ln_b[layer].reshape(1, -1),
                layer, batch, seq)


def kernel(x, attn_w_qkv, attn_w_o, attn_sink, lru_w_in, lru_conv_w, lru_conv_b, lru_w_rgate,
           lru_b_rgate, lru_w_igate, lru_b_igate, lru_lambda, lru_w_out, moe_w_router, moe_w_gate_up,
           moe_w_down, ln_mix_g, ln_mix_b, ln_ffn_g, ln_ffn_b):
    batch, seq, _ = x.shape
    x2d = x.reshape(batch * seq, D_MODEL)
    row = lambda v: v.reshape(1, -1)

    cos_t, sin_t = _rotary_tables(seq)
    q, kv = _qkv_proj(x2d, attn_w_qkv[0].astype(BF16), cos_t, sin_t, seq)
    o = _attention(q, kv, attn_sink[0], batch, seq)
    x1, aff_t = _proj_ln(o, attn_w_o[0].astype(BF16), x2d, row(ln_mix_g[0]), row(ln_mix_b[0]),
                         moe_w_router[0].T.astype(BF16), batch, seq)
    x2 = _moe_block(x1, aff_t, moe_w_gate_up, moe_w_down, ln_ffn_g, ln_ffn_b, 0, batch, seq)

    gg, xc = _lru_in(x2, lru_w_in[0].astype(BF16), lru_conv_w[0], row(lru_conv_b[0]), seq)
    wg = (0.5 * jnp.concatenate([lru_w_rgate[0], lru_w_igate[0]], axis=-1)).astype(BF16)
    half_row = lambda v: 0.5 * v.reshape(1, -1)
    per_seq = lambda t: t.reshape(batch, seq, D_MODEL)
    xc3 = per_seq(xc)
    hf3 = _lru_forward(xc3, wg[0], half_row(lru_b_rgate[0, 0]), half_row(lru_b_igate[0, 0]),
                       row(lru_lambda[0, 0]))
    x3, aff_t = _lru_backward_out(xc3, wg[1], half_row(lru_b_rgate[0, 1]), half_row(lru_b_igate[0, 1]),
                                  row(lru_lambda[0, 1]), hf3, per_seq(gg), per_seq(x2),
                                  lru_w_out[0].astype(BF16), row(ln_mix_g[1]), row(ln_mix_b[1]),
                                  moe_w_router[1].T.astype(BF16))
    x3 = x3.reshape(batch * seq * TOK_ROWS, LANES)
    x4 = _moe_block(x3, aff_t, moe_w_gate_up, moe_w_down, ln_ffn_g, ln_ffn_b, 1, batch, seq)
    return x4.reshape(batch, seq, D_MODEL)
```

```python
import functools

import jax
import jax.numpy as jnp
from jax import lax
from jax.experimental import pallas as pl
from jax.experimental.pallas import tpu as pltpu

F32 = jnp.float32
BF16 = jnp.bfloat16

D_MODEL = 1024
HEAD_DIM = 64
N_Q_HEADS = 16
N_KV_HEADS = 4
GQA_GROUP = 4
WINDOW = 128
BLOCK = 128
ROT_DIM = 16
ROPE_THETA = 500000.0
LRU_BLOCKS = 4
LRU_BLOCK_W = 256
CONV_WIDTH = 4
LRU_C = 8.0
N_EXPERTS = 16
EXPERT_FF = 1024
CAPACITY_FACTOR = 2
LN_EPS = 1e-5
DEPTH = 2
ALPHA = (2 * DEPTH) ** 0.25

LANES = 128
SUBLANES = 8
MXU_COLS = 256
TOK_ROWS = D_MODEL // LANES
VMEM_LIMIT_MOE = 61 * 1024 * 1024


def _layer_norm(v, g, b):
    mu = jnp.mean(v, axis=-1, keepdims=True)
    vc = v - mu
    var = jnp.mean(vc * vc, axis=-1, keepdims=True)
    return vc * lax.rsqrt(var + LN_EPS) * g + b


def _store_token_layout(dst_ref, val):
    n = val.shape[0]
    for s in range(TOK_ROWS):
        dst_ref[pl.ds(s, n, stride=TOK_ROWS), :] = val[:, s * LANES:(s + 1) * LANES]


def _load_token_layout(src_ref, first_token, n):
    base = first_token * TOK_ROWS
    return jnp.concatenate(
        [src_ref[pl.ds(base + s, n, stride=TOK_ROWS), :] for s in range(TOK_ROWS)], axis=1)


def _router_affinity_t(xn, wr_t):
    logits_t = lax.dot_general(wr_t, xn.astype(BF16), (((1,), (1,)), ((), ())),
                               preferred_element_type=F32)
    m = jnp.max(logits_t, axis=0, keepdims=True)
    p = jnp.exp(logits_t - m)
    return p / jnp.sum(p, axis=0, keepdims=True)


def _dot_by_columns(lhs, w_ref, first_col, n_cols):
    return [jnp.dot(lhs, w_ref[:, c:c + MXU_COLS], preferred_element_type=F32)
            for c in range(first_col, first_col + n_cols, MXU_COLS)]


LOG2E = 1.4426950408889634


def _qkv_kernel(x_ref, w_ref, cos_ref, sin_ref, q_ref, kv_ref):
    x = x_ref[...].astype(BF16)
    acc = jnp.dot(x, w_ref[...], preferred_element_type=F32)
    cosb = cos_ref[...]
    sinb = sin_ref[...]
    lane = lax.broadcasted_iota(jnp.int32, cosb.shape, 1)
    low = (lane % HEAD_DIM) < (ROT_DIM // 2)
    nq = N_Q_HEADS * HEAD_DIM
    nk = N_KV_HEADS * HEAD_DIM

    def rot(t):
        up = pltpu.roll(t, LANES - ROT_DIM // 2, axis=1)
        dn = pltpu.roll(t, ROT_DIM // 2, axis=1)
        return t * cosb + jnp.where(low, up, dn) * sinb

    qs = [rot(acc[:, c * LANES:(c + 1) * LANES]) * (LOG2E * HEAD_DIM ** -0.5) for c in range(nq // LANES)]
    q_ref[...] = jnp.concatenate(qs, axis=1).astype(BF16)
    ks = [rot(acc[:, nq + c * LANES:nq + (c + 1) * LANES]) for c in range(nk // LANES)]
    vs = [acc[:, nq + nk + c * LANES:nq + nk + (c + 1) * LANES] for c in range(nk // LANES)]
    swap = lambda t: pltpu.roll(t, HEAD_DIM, axis=1)
    kv_ref[...] = jnp.concatenate(
        ks + [swap(t) for t in ks] + vs + [swap(t) for t in vs], axis=1).astype(BF16)


def _qkv_proj(x2d, w_qkv, cos_t, sin_t, seq):
    T = x2d.shape[0]
    tm = min(512, seq)
    nq = N_Q_HEADS * HEAD_DIM
    nk = N_KV_HEADS * HEAD_DIM
    nts = seq // tm
    return pl.pallas_call(
        _qkv_kernel,
        grid=(T // tm,),
        in_specs=[
            pl.BlockSpec((tm, D_MODEL), lambda i: (i, 0)),
            pl.BlockSpec((D_MODEL, nq + 2 * nk), lambda i: (0, 0)),
            pl.BlockSpec((tm, LANES), lambda i: (i % nts, 0)),
            pl.BlockSpec((tm, LANES), lambda i: (i % nts, 0)),
        ],
        out_specs=[
            pl.BlockSpec((tm, nq), lambda i: (i, 0)),
            pl.BlockSpec((tm, 4 * nk), lambda i: (i, 0)),
        ],
        out_shape=[
            jax.ShapeDtypeStruct((T, nq), BF16),
            jax.ShapeDtypeStruct((T, 4 * nk), BF16),
        ],
        compiler_params=pltpu.CompilerParams(dimension_semantics=("arbitrary",)),
        name="qkv_proj",
    )(x2d, w_qkv, cos_t, sin_t)


QBLKS = 4


def _attn_kernel(sink_ref, q_ref, kvp_ref, kvc_ref, kvn_ref, o_ref, *, seq):
    i = pl.program_id(1)
    nk = N_KV_HEADS * HEAD_DIM
    kv = jnp.concatenate([kvp_ref[...], kvc_ref[...], kvn_ref[...]], axis=0)
    band = 3 * BLOCK
    lo = lax.broadcasted_iota(jnp.int32, (1, LANES), 1) < HEAD_DIM
    zero = jnp.zeros((), BF16)
    k_lo, k_hi, v_dup_t = [], [], []
    for h in range(N_KV_HEADS):
        grp, half = h // 2, h % 2

        def block(base, swapped):
            col = base + (nk if swapped else 0) + grp * LANES
            return kv[:, col:col + LANES]
        k_lo.append(jnp.where(lo, block(0, half == 1), zero))
        k_hi.append(jnp.where(lo, zero, block(0, half == 0)))
        v_dup = jnp.where(lo, block(2 * nk, half == 1), block(2 * nk, half == 0))
        v_dup_t.append(v_dup.astype(F32).T.astype(BF16))

    key = lax.broadcasted_iota(jnp.int32, (BLOCK, BLOCK), 0)
    qry = lax.broadcasted_iota(jnp.int32, (BLOCK, BLOCK), 1)
    feat_lo = lax.broadcasted_iota(jnp.int32, (LANES, 1), 0) < HEAD_DIM
    units = [(j, h) for j in range(QBLKS) for h in range(N_KV_HEADS)]

    def scores_t(j, h):
        rows = slice(j * BLOCK, (j + 1) * BLOCK)
        qp = jnp.concatenate([q_ref[rows, (2 * h) * LANES:(2 * h + 1) * LANES],
                              q_ref[rows, (2 * h + 1) * LANES:(2 * h + 2) * LANES]], axis=0)
        keys = jnp.concatenate([k_lo[h][j * BLOCK:j * BLOCK + band], k_hi[h][j * BLOCK:j * BLOCK + band]],
                               axis=0)
        return lax.dot_general(keys, qp, (((1,), (1,)), ((), ())), preferred_element_type=F32)

    s_next = scores_t(*units[0])
    for u, (j, h) in enumerate(units):
        s = s_next
        if u + 1 < len(units):
            s_next = scores_t(*units[u + 1])
        n = i * QBLKS + j
        ok_prev = jnp.concatenate([(key >= qry) & (n > 0)] * 2, axis=1)
        ok_next = jnp.concatenate([(key <= qry) & (n < seq // BLOCK - 1)] * 2, axis=1)
        vt = v_dup_t[h][:, j * BLOCK:j * BLOCK + band]
        halves = []
        for par in range(2):
            sp = s[par * band:(par + 1) * band, :]
            sp = jnp.concatenate([jnp.where(ok_prev, sp[:BLOCK], -jnp.inf), sp[BLOCK:2 * BLOCK],
                                  jnp.where(ok_next, sp[2 * BLOCK:], -jnp.inf)], axis=0)
            sink_row = jnp.concatenate(
                [jnp.full((1, BLOCK), sink_ref[h * GQA_GROUP + g] * LOG2E, F32) for g in (par, par + 2)],
                axis=1)
            m = jnp.maximum(jnp.max(sp, axis=0, keepdims=True), sink_row)
            p = jnp.exp2(sp - m)
            denom = jnp.sum(p, axis=0, keepdims=True) + jnp.exp2(sink_row - m)
            halves.append(jnp.dot(vt, p.astype(BF16), preferred_element_type=F32) / denom)
        out_t = jnp.where(feat_lo, halves[0], halves[1])
        o_ref[j * BLOCK:(j + 1) * BLOCK, (2 * h) * LANES:(2 * h + 2) * LANES] = jnp.concatenate(
            [out_t[:, :BLOCK].T, out_t[:, BLOCK:].T], axis=1).astype(BF16)


def _attention(q, kv, sink, batch, seq):
    T = q.shape[0]
    qt = QBLKS * BLOCK
    nq = seq // qt
    nb = seq // BLOCK
    wkv = kv.shape[1]
    cur = lambda b, i: (b * nq + i, 0)
    prev = lambda b, i: (b * nb + jnp.maximum(i * QBLKS - 1, 0), 0)
    nxt = lambda b, i: (b * nb + jnp.minimum(i * QBLKS + QBLKS, nb - 1), 0)
    return pl.pallas_call(
        functools.partial(_attn_kernel, seq=seq),
        grid=(batch, nq),
        in_specs=[
            pl.BlockSpec(memory_space=pltpu.SMEM),
            pl.BlockSpec((qt, D_MODEL), cur),
            pl.BlockSpec((BLOCK, wkv), prev),
            pl.BlockSpec((qt, wkv), cur),
            pl.BlockSpec((BLOCK, wkv), nxt),
        ],
        out_specs=pl.BlockSpec((qt, D_MODEL), cur),
        out_shape=jax.ShapeDtypeStruct((T, D_MODEL), BF16),
        compiler_params=pltpu.CompilerParams(dimension_semantics=("arbitrary", "arbitrary")),
        name="swa_attention",
    )(sink, q, kv, kv, kv)


def _proj_ln_kernel(a_ref, w_ref, x_ref, g_ref, b_ref, wr_ref, xo_ref, aff_ref):
    y = jnp.dot(a_ref[...], w_ref[...], preferred_element_type=F32)
    xn = _layer_norm(ALPHA * x_ref[...] + y, g_ref[...], b_ref[...])
    _store_token_layout(xo_ref, xn)
    aff_ref[0] = _router_affinity_t(xn, wr_ref[...])


def _proj_ln(a, w, x2d, g, b, wr_t, batch, seq):
    T = x2d.shape[0]
    tm = min(512, seq)
    nts = seq // tm
    return pl.pallas_call(
        _proj_ln_kernel,
        grid=(T // tm,),
        in_specs=[
            pl.BlockSpec((tm, D_MODEL), lambda i: (i, 0)),
            pl.BlockSpec((D_MODEL, D_MODEL), lambda i: (0, 0)),
            pl.BlockSpec((tm, D_MODEL), lambda i: (i, 0)),
            pl.BlockSpec((1, D_MODEL), lambda i: (0, 0)),
            pl.BlockSpec((1, D_MODEL), lambda i: (0, 0)),
            pl.BlockSpec((N_EXPERTS, D_MODEL), lambda i: (0, 0)),
        ],
        out_specs=[
            pl.BlockSpec((tm * TOK_ROWS, LANES), lambda i: (i, 0)),
            pl.BlockSpec((1, N_EXPERTS, tm), lambda i: (i // nts, 0, i % nts)),
        ],
        out_shape=[
            jax.ShapeDtypeStruct((T * TOK_ROWS, LANES), F32),
            jax.ShapeDtypeStruct((batch, N_EXPERTS, seq), F32),
        ],
        compiler_params=pltpu.CompilerParams(dimension_semantics=("arbitrary",)),
        name="proj_ln_router",
    )(a, w, x2d, g, b, wr_t)


SLOT_BLOCK = 64


def _lane_prefix(mask, tri, dst_ref):
    seq = mask.shape[1]
    off = jnp.zeros((mask.shape[0], 1), F32)
    mb = mask.astype(BF16)
    ends = []
    for c in range(seq // LANES):
        res = jnp.dot(mb[:, c * LANES:(c + 1) * LANES], tri, preferred_element_type=F32) + off
        dst_ref[:, c * LANES:(c + 1) * LANES] = res
        off = res[:, LANES - 1:LANES]
        ends.append(off)
    return ends


def _select_kernel(aff_ref, pos_ref, bnd_ref, *, cap):
    aff = aff_ref[0]
    as_float = lambda bits: pltpu.bitcast(bits, F32)
    cur = jnp.zeros((N_EXPERTS, 1), jnp.int32)
    for bit in range(30, -1, -1):
        cand = cur | (1 << bit)
        cnt = jnp.sum((aff >= as_float(cand)).astype(jnp.int32), axis=1, keepdims=True)
        cur = jnp.where(cnt >= cap, cand, cur)
    gt = aff >= as_float(cur + 1)
    eq = (aff >= as_float(cur)) & jnp.logical_not(gt)
    n_gt = jnp.sum(gt.astype(jnp.int32), axis=1, keepdims=True)
    ties_taken = (cap - n_gt).astype(F32)
    ri = lax.broadcasted_iota(jnp.int32, (LANES, LANES), 0)
    ci = lax.broadcasted_iota(jnp.int32, (LANES, LANES), 1)
    tri = (ri <= ci).astype(BF16)
    _lane_prefix(eq, tri, pos_ref.at[0])
    sel = gt | (eq & (pos_ref[0] <= ties_taken))
    ends = _lane_prefix(sel, tri, pos_ref.at[0])
    pos_ref[0] = jnp.where(sel, pos_ref[0], 0.0)
    cols = []
    for sb in range(cap // SLOT_BLOCK):
        for target in (sb * SLOT_BLOCK + 1, (sb + 1) * SLOT_BLOCK):
            blk = jnp.zeros((N_EXPERTS, 1), jnp.int32)
            for end in ends[:-1]:
                blk = blk + (end < target).astype(jnp.int32)
            cols.append(blk)
    bnd_ref[0] = jnp.concatenate(cols, axis=1)


def _compact_kernel(bnd_ref, pos_ref, aff_ref, idx_ref, gate_ref, ptok, pgate, *, cap):
    b = pl.program_id(0)
    n_sb = cap // SLOT_BLOCK
    groups = SLOT_BLOCK // SUBLANES
    lane = lax.broadcasted_iota(jnp.int32, (SUBLANES, LANES), 1).astype(F32)
    sub1 = (lax.broadcasted_iota(jnp.int32, (SUBLANES, 1), 0) + 1).astype(F32)
    lane_e = lax.broadcasted_iota(jnp.int32, (cap, N_EXPERTS), 1)
    idx_ref[0] = jnp.zeros((cap, N_EXPERTS), jnp.int32)
    gate_ref[0] = jnp.zeros((cap, N_EXPERTS), F32)

    def expert(e, carry):
        def slot_block(sb, carry2):
            base = ((b * N_EXPERTS + e) * n_sb + sb) * 2
            slot0 = jnp.asarray(sb * SLOT_BLOCK, F32)

            def lane_block(c, accs):
                off = pl.multiple_of(c * LANES, LANES)
                p = jnp.broadcast_to(pos_ref[0, e:e + 1, pl.ds(off, LANES)], (SUBLANES, LANES))
                a = jnp.broadcast_to(aff_ref[0, e:e + 1, pl.ds(off, LANES)], (SUBLANES, LANES))
                tokv = lane + jnp.asarray(c * LANES, F32)
                out = []
                for g in range(groups):
                    hit = p == (slot0 + (g * SUBLANES) + sub1)
                    out.append(accs[2 * g] + jnp.where(hit, tokv, 0.0))
                    out.append(accs[2 * g + 1] + jnp.where(hit, a, 0.0))
                return tuple(out)

            zeros = tuple(jnp.zeros((SUBLANES, LANES), F32) for _ in range(2 * groups))
            accs = lax.fori_loop(bnd_ref[base], bnd_ref[base + 1] + 1, lane_block, zeros)
            for g in range(groups):
                rows = pl.ds(pl.multiple_of(sb * SLOT_BLOCK + g * SUBLANES, SUBLANES), SUBLANES)
                ptok[rows, :] = accs[2 * g]
                pgate[rows, :] = accs[2 * g + 1]
            return carry2

        lax.fori_loop(0, n_sb, slot_block, 0)
        tok_col = jnp.sum(ptok[...], axis=1, keepdims=True).astype(jnp.int32)
        gate_col = jnp.sum(pgate[...], axis=1, keepdims=True)
        idx_ref[0] = jnp.where(lane_e == e, tok_col, idx_ref[0])
        gate_ref[0] = jnp.where(lane_e == e, gate_col, gate_ref[0])
        return carry

    for e in range(N_EXPERTS):
        expert(e, 0)


def _route(aff_t, cap):
    batch, _, seq = aff_t.shape
    n_bnd = 2 * (cap // SLOT_BLOCK)
    seq_spec = pl.BlockSpec((1, N_EXPERTS, seq), lambda b, *_: (b, 0, 0))
    pos, bnd = pl.pallas_call(
        functools.partial(_select_kernel, cap=cap),
        grid=(batch,),
        in_specs=[seq_spec],
        out_specs=[seq_spec, pl.BlockSpec((1, N_EXPERTS, n_bnd), lambda b: (b, 0, 0))],
        out_shape=[
            jax.ShapeDtypeStruct((batch, N_EXPERTS, seq), F32),
            jax.ShapeDtypeStruct((batch, N_EXPERTS, n_bnd), jnp.int32),
        ],
        compiler_params=pltpu.CompilerParams(dimension_semantics=("arbitrary",)),
        name="expert_choice_select",
    )(aff_t)
    slot_spec = pl.BlockSpec((1, cap, N_EXPERTS), lambda b, *_: (b, 0, 0))
    idx_t, gate_t = pl.pallas_call(
        functools.partial(_compact_kernel, cap=cap),
        grid_spec=pltpu.PrefetchScalarGridSpec(
            num_scalar_prefetch=1,
            grid=(batch,),
            in_specs=[seq_spec, seq_spec],
            out_specs=[slot_spec, slot_spec],
            scratch_shapes=[pltpu.VMEM((cap, LANES), F32), pltpu.VMEM((cap, LANES), F32)],
        ),
        out_shape=[
            jax.ShapeDtypeStruct((batch, cap, N_EXPERTS), jnp.int32),
            jax.ShapeDtypeStruct((batch, cap, N_EXPERTS), F32),
        ],
        compiler_params=pltpu.CompilerParams(dimension_semantics=("arbitrary",)),
        name="expert_choice_compact",
    )(bnd.reshape(-1), pos, aff_t)
    return jnp.swapaxes(idx_t, 1, 2), gate_t


FF_CHUNKS = 2
LN_CHUNK = 256
SCATTER_GROUP = 2


def _moe_kernel(idx_ref, x1t_hbm, gate_ref, wg_ref, wu_ref, wd_ref, lng_ref, lnb_ref, out_hbm,
                x1v, acc, xs3, y3, xs2d, yacc, stage, sem_in, sem_out, *, seq, cap):
    b = pl.program_id(0)
    e = pl.program_id(1)
    j = pl.program_id(2)
    tok_rows = seq * TOK_ROWS
    last_e = N_EXPERTS - 1

    def tok_slice(t):
        return pl.ds(pl.multiple_of(t * TOK_ROWS, TOK_ROWS), TOK_ROWS)

    def gather_rows(expert):
        base = (b * N_EXPERTS + expert) * cap
        for i in range(cap):
            xs3[i * TOK_ROWS:(i + 1) * TOK_ROWS, :] = x1v[tok_slice(idx_ref[base + i]), :]

    def scatter_rows(expert):
        base = (b * N_EXPERTS + expert) * cap
        for g0 in range(0, cap, SCATTER_GROUP):
            toks = [idx_ref[base + g0 + k] for k in range(SCATTER_GROUP)]
            vals = [acc[tok_slice(toks[k]), :] + y3[(g0 + k) * TOK_ROWS:(g0 + k + 1) * TOK_ROWS, :]
                    for k in range(SCATTER_GROUP)]
            for k in range(SCATTER_GROUP):
                acc[tok_slice(toks[k]), :] = vals[k]

    def expert_chunk(slot):
        xs = xs2d[slot]
        g = jnp.dot(xs, wg_ref[0, 0].astype(BF16), preferred_element_type=F32)
        u = jnp.dot(xs, wu_ref[0, 0].astype(BF16), preferred_element_type=F32)
        act = (g * jax.nn.sigmoid(g) * u).astype(BF16)
        return jnp.dot(act, wd_ref[0, 0].astype(BF16), preferred_element_type=F32)

    @pl.when((e == 0) & (j == 0))
    def _start_sequence():
        cp = pltpu.make_async_copy(x1t_hbm.at[pl.ds(b * tok_rows, tok_rows)], x1v, sem_in)
        cp.start()
        acc[...] = jnp.zeros_like(acc)
        y3[...] = jnp.zeros_like(y3)
        cp.wait()

        def body(i, carry):
            xs3[tok_slice(i), :] = x1v[tok_slice(idx_ref[b * N_EXPERTS * cap + i]), :]
            return carry
        lax.fori_loop(0, cap, body, 0, unroll=8)
        xs2d[0] = _load_token_layout(xs3, 0, cap).astype(BF16)

    @pl.when(j == 0)
    def _chunk0():
        scatter_rows(jnp.maximum(e - 1, 0))
        yacc[...] = expert_chunk(e % 2)

    @pl.when(j == 1)
    def _chunk1():
        gather_rows(jnp.minimum(e + 1, last_e))
        y = yacc[...] + expert_chunk(e % 2)
        gates = gate_ref[0]
        lane_e = lax.broadcasted_iota(jnp.int32, gates.shape, 1)
        gcol = jnp.sum(jnp.where(lane_e == e, gates, 0.0), axis=1, keepdims=True)
        _store_token_layout(y3, y * gcol)
        xs2d[(e + 1) % 2] = _load_token_layout(xs3, 0, cap).astype(BF16)

    @pl.when((e == last_e) & (j == 1))
    def _finalize():
        def body(gi, carry):
            base = (b * N_EXPERTS + last_e) * cap + gi * SUBLANES
            toks = [idx_ref[base + k] for k in range(SUBLANES)]
            vals = [acc[tok_slice(toks[k]), :] + y3[tok_slice(gi * SUBLANES + k), :]
                    for k in range(SUBLANES)]
            for k in range(SUBLANES):
                acc[tok_slice(toks[k]), :] = vals[k]
            return carry
        lax.fori_loop(0, cap // SUBLANES, body, 0)

        n_chunks = seq // LN_CHUNK

        def out_copy(c, slot):
            return pltpu.make_async_copy(
                stage.at[slot], out_hbm.at[pl.ds(b * seq + c * LN_CHUNK, LN_CHUNK)], sem_out.at[slot])

        def ln_body(c, carry):
            slot = c % 2
            first = c * LN_CHUNK
            res = _layer_norm(ALPHA * _load_token_layout(x1v, first, LN_CHUNK)
                              + _load_token_layout(acc, first, LN_CHUNK), lng_ref[...], lnb_ref[...])

            @pl.when(c >= 2)
            def _():
                out_copy(c - 2, slot).wait()
            stage[slot] = res
            out_copy(c, slot).start()
            return carry
        lax.fori_loop(0, n_chunks, ln_body, 0)
        out_copy(n_chunks - 2, (n_chunks - 2) % 2).wait()
        out_copy(n_chunks - 1, (n_chunks - 1) % 2).wait()


def _moe(x1t, idx, gate_t, w_gate_up, w_down, ln_g, ln_b, layer, batch, seq):
    cap = idx.shape[2]
    T = batch * seq
    fc = EXPERT_FF // FF_CHUNKS
    assert FF_CHUNKS == 2
    idx_flat = idx.reshape(-1)
    grid_spec = pltpu.PrefetchScalarGridSpec(
        num_scalar_prefetch=1,
        grid=(batch, N_EXPERTS, FF_CHUNKS),
        in_specs=[
            pl.BlockSpec(memory_space=pl.ANY),
            pl.BlockSpec((1, cap, N_EXPERTS), lambda b, e, j, idx: (b, 0, 0)),
            pl.BlockSpec((1, 1, D_MODEL, fc), lambda b, e, j, idx: (layer, e, 0, j)),
            pl.BlockSpec((1, 1, D_MODEL, fc), lambda b, e, j, idx: (layer, e, 0, FF_CHUNKS + j)),
            pl.BlockSpec((1, 1, fc, D_MODEL), lambda b, e, j, idx: (layer, e, j, 0)),
            pl.BlockSpec((1, D_MODEL), lambda b, e, j, idx: (0, 0)),
            pl.BlockSpec((1, D_MODEL), lambda b, e, j, idx: (0, 0)),
        ],
        out_specs=pl.BlockSpec(memory_space=pl.ANY),
        scratch_shapes=[
            pltpu.VMEM((seq * TOK_ROWS, LANES), F32),
            pltpu.VMEM((seq * TOK_ROWS, LANES), F32),
            pltpu.VMEM((cap * TOK_ROWS, LANES), F32),
            pltpu.VMEM((cap * TOK_ROWS, LANES), F32),
            pltpu.VMEM((2, cap, D_MODEL), BF16),
            pltpu.VMEM((cap, D_MODEL), F32),
            pltpu.VMEM((2, LN_CHUNK, D_MODEL), F32),
            pltpu.SemaphoreType.DMA(()),
            pltpu.SemaphoreType.DMA((2,)),
        ],
    )
    return pl.pallas_call(
        functools.partial(_moe_kernel, seq=seq, cap=cap),
        grid_spec=grid_spec,
        out_shape=jax.ShapeDtypeStruct((T, D_MODEL), F32),
        compiler_params=pltpu.CompilerParams(
            dimension_semantics=("arbitrary", "arbitrary", "arbitrary"),
            vmem_limit_bytes=VMEM_LIMIT_MOE),
        name="expert_ffn_combine_ln",
    )(idx_flat, x1t, gate_t, w_gate_up, w_gate_up, w_down, ln_g, ln_b)


HALO = SUBLANES


def _gelu_tanh(x):
    return 0.5 * x * (1.0 + jnp.tanh(0.7978845608028654 * (x + 0.044715 * (x * x * x))))


def _lru_in_kernel(x_ref, xp_ref, xn_ref, w_ref, cw_ref, cb_ref, gg_ref, xc_ref, *, nts):
    i = pl.program_id(0)
    tm = x_ref.shape[0]
    x = x_ref[...].astype(BF16)
    xp = xp_ref[...].astype(BF16)
    xn = xn_ref[...].astype(BF16)
    first = (i % nts) == 0
    last = (i % nts) == nts - 1
    for k, gate in enumerate(_dot_by_columns(x, w_ref, 0, D_MODEL)):
        gg_ref[:, k * MXU_COLS:(k + 1) * MXU_COLS] = _gelu_tanh(gate)
    n_ext = tm + 2 * HALO
    for k in range(D_MODEL // MXU_COLS):
        cols = slice(k * MXU_COLS, (k + 1) * MXU_COLS)
        xb, = _dot_by_columns(x, w_ref, D_MODEL + k * MXU_COLS, MXU_COLS)
        xbp, = _dot_by_columns(xp, w_ref, D_MODEL + k * MXU_COLS, MXU_COLS)
        xbn, = _dot_by_columns(xn, w_ref, D_MODEL + k * MXU_COLS, MXU_COLS)
        ext = jnp.concatenate([jnp.where(first, 0.0, xbp), xb, jnp.where(last, 0.0, xbn)], axis=0)
        xc = cb_ref[:, cols]
        for tap in range(CONV_WIDTH):
            shift = (2 - tap) % n_ext
            src = ext if shift == 0 else pltpu.roll(ext, shift, axis=0)
            xc = xc + cw_ref[tap:tap + 1, cols] * src[HALO:HALO + tm, :]
        xc_ref[:, cols] = xc


def _lru_in(x2d, w_in, conv_w, conv_b, seq):
    T = x2d.shape[0]
    tm = min(512, seq)
    nts = seq // tm
    hb = tm // HALO
    nblk = T // HALO
    return pl.pallas_call(
        functools.partial(_lru_in_kernel, nts=nts),
        grid=(T // tm,),
        in_specs=[
            pl.BlockSpec((tm, D_MODEL), lambda i: (i, 0)),
            pl.BlockSpec((HALO, D_MODEL), lambda i: (jnp.maximum(i * hb - 1, 0), 0)),
            pl.BlockSpec((HALO, D_MODEL), lambda i: (jnp.minimum((i + 1) * hb, nblk - 1), 0)),
            pl.BlockSpec((D_MODEL, 2 * D_MODEL), lambda i: (0, 0)),
            pl.BlockSpec((CONV_WIDTH, D_MODEL), lambda i: (0, 0)),
            pl.BlockSpec((1, D_MODEL), lambda i: (0, 0)),
        ],
        out_specs=[
            pl.BlockSpec((tm, D_MODEL), lambda i: (i, 0)),
            pl.BlockSpec((tm, D_MODEL), lambda i: (i, 0)),
        ],
        out_shape=[
            jax.ShapeDtypeStruct((T, D_MODEL), F32),
            jax.ShapeDtypeStruct((T, D_MODEL), F32),
        ],
        compiler_params=pltpu.CompilerParams(dimension_semantics=("arbitrary",)),
        name="lru_in_conv",
    )(x2d, x2d, x2d, w_in, conv_w, conv_b)


SCAN_TM = 128
CHAIN_PITCH = SCAN_TM + SUBLANES
PAIRS = D_MODEL // (2 * LANES)


def _lru_chain_scan(xc_ref, wg_ref, br_ref, bi_ref, lam_ref, carry, abuf, ubuf, hbuf, *, reverse, seq):
    i = pl.program_id(0)
    nseq, tm = xc_ref.shape[0], xc_ref.shape[1]
    assert 2 * nseq == SUBLANES and tm == SCAN_TM
    nt = seq // tm
    tile = (nt - 1 - i) if reverse else i

    @pl.when(i == 0)
    def _():
        carry[...] = jnp.zeros_like(carry)

    half_nsp = (-0.5 * LRU_C) * jax.nn.softplus(-lam_ref[...])
    row = lax.broadcasted_iota(jnp.int32, (tm, 1), 0)
    is_start = (tile * tm + row) == (seq - 1 if reverse else 0)
    for b in range(nseq):
        xc = xc_ref[b]
        xcb = xc.astype(BF16)
        r_parts, i_parts = [], []
        for n in range(LRU_BLOCKS):
            res = jnp.dot(xcb[:, n * LRU_BLOCK_W:(n + 1) * LRU_BLOCK_W], wg_ref[n],
                          preferred_element_type=F32)
            r_parts.append(res[:, :LRU_BLOCK_W])
            i_parts.append(res[:, LRU_BLOCK_W:])
        tr = jnp.tanh(jnp.concatenate(r_parts, axis=1) + br_ref[...])
        ig = 0.5 * jnp.tanh(jnp.concatenate(i_parts, axis=1) + bi_ref[...]) + 0.5
        log_a = tr * half_nsp + half_nsp
        a = jnp.exp(log_a)
        z = -jnp.tanh(log_a) * (1.0 + a * a)
        root = jnp.where(z > 0.0, z * lax.rsqrt(z), 0.0)
        u = jnp.where(is_start, 1.0, root) * ig * xc
        for lb in range(2 * PAIRS):
            rows = slice((2 * b + lb % 2) * CHAIN_PITCH, (2 * b + lb % 2) * CHAIN_PITCH + tm)
            abuf[lb // 2, rows, :] = a[:, lb * LANES:(lb + 1) * LANES]
            ubuf[lb // 2, rows, :] = u[:, lb * LANES:(lb + 1) * LANES]

    def step(k, hs):
        t = (tm - 1 - k) if reverse else k
        chains = pl.ds(t, SUBLANES, stride=CHAIN_PITCH)
        out = []
        for m in range(PAIRS):
            h = abuf[m, chains, :] * hs[m] + ubuf[m, chains, :]
            hbuf[m, chains, :] = h
            out.append(h)
        return tuple(out)

    hs = lax.fori_loop(0, tm, step, tuple(carry[m] for m in range(PAIRS)), unroll=8)
    for m in range(PAIRS):
        carry[m] = hs[m]


def _chain_tile(hbuf, b, tm):
    return jnp.concatenate(
        [hbuf[lb // 2, (2 * b + lb % 2) * CHAIN_PITCH:(2 * b + lb % 2) * CHAIN_PITCH + tm, :]
         for lb in range(2 * PAIRS)], axis=1)


def _lru_fwd_kernel(xc_ref, wg_ref, br_ref, bi_ref, lam_ref, h_ref, carry, abuf, ubuf, hbuf, *, seq):
    _lru_chain_scan(xc_ref, wg_ref, br_ref, bi_ref, lam_ref, carry, abuf, ubuf, hbuf,
                    reverse=False, seq=seq)
    for b in range(h_ref.shape[0]):
        h_ref[b] = _chain_tile(hbuf, b, h_ref.shape[1])


def _lru_bwd_kernel(xc_ref, wg_ref, br_ref, bi_ref, lam_ref, hf_ref, gg_ref, x_ref, wo_ref,
                    g_ref, b_ref, wr_ref, xo_ref, aff_ref, carry, abuf, ubuf, hbuf, *, seq):
    _lru_chain_scan(xc_ref, wg_ref, br_ref, bi_ref, lam_ref, carry, abuf, ubuf, hbuf,
                    reverse=True, seq=seq)
    tm = x_ref.shape[1]
    for b in range(x_ref.shape[0]):
        h = hf_ref[b] + _chain_tile(hbuf, b, tm)
        y = jnp.dot((h * gg_ref[b]).astype(BF16), wo_ref[...], preferred_element_type=F32)
        xn = _layer_norm(ALPHA * x_ref[b] + y, g_ref[...], b_ref[...])
        _store_token_layout(xo_ref.at[b], xn)
        aff_ref[b] = _router_affinity_t(xn, wr_ref[...])


def _lru_specs(batch, tile_map):
    const2 = lambda i: (0, 0)
    return [
        pl.BlockSpec((batch, SCAN_TM, D_MODEL), tile_map),
        pl.BlockSpec((LRU_BLOCKS, LRU_BLOCK_W, 2 * LRU_BLOCK_W), lambda i: (0, 0, 0)),
        pl.BlockSpec((1, D_MODEL), const2),
        pl.BlockSpec((1, D_MODEL), const2),
        pl.BlockSpec((1, D_MODEL), const2),
    ]


def _lru_scratch():
    buf = pltpu.VMEM((PAIRS, SUBLANES * CHAIN_PITCH, LANES), F32)
    return [pltpu.VMEM((PAIRS, SUBLANES, LANES), F32), buf, buf, buf]


def _lru_forward(xc3, wg, br, bi, lam):
    batch, seq, _ = xc3.shape
    tile_map = lambda i: (0, i, 0)
    return pl.pallas_call(
        functools.partial(_lru_fwd_kernel, seq=seq),
        grid=(seq // SCAN_TM,),
        in_specs=_lru_specs(batch, tile_map),
        out_specs=pl.BlockSpec((batch, SCAN_TM, D_MODEL), tile_map),
        out_shape=jax.ShapeDtypeStruct((batch, seq, D_MODEL), F32),
        scratch_shapes=_lru_scratch(),
        compiler_params=pltpu.CompilerParams(dimension_semantics=("arbitrary",)),
        name="lru_forward_scan",
    )(xc3, wg, br, bi, lam)


def _lru_backward_out(xc3, wg, br, bi, lam, hf3, gg3, x3d, w_out, g, b, wr_t):
    batch, seq, _ = xc3.shape
    nt = seq // SCAN_TM
    tile_map = lambda i: (0, nt - 1 - i, 0)
    const2 = lambda i: (0, 0)
    tile = pl.BlockSpec((batch, SCAN_TM, D_MODEL), tile_map)
    return pl.pallas_call(
        functools.partial(_lru_bwd_kernel, seq=seq),
        grid=(nt,),
        in_specs=_lru_specs(batch, tile_map) + [
            tile, tile, tile,
            pl.BlockSpec((D_MODEL, D_MODEL), const2),
            pl.BlockSpec((1, D_MODEL), const2),
            pl.BlockSpec((1, D_MODEL), const2),
            pl.BlockSpec((N_EXPERTS, D_MODEL), const2),
        ],
        out_specs=[
            pl.BlockSpec((batch, SCAN_TM * TOK_ROWS, LANES), tile_map),
            pl.BlockSpec((batch, N_EXPERTS, SCAN_TM), lambda i: (0, 0, nt - 1 - i)),
        ],
        out_shape=[
            jax.ShapeDtypeStruct((batch, seq * TOK_ROWS, LANES), F32),
            jax.ShapeDtypeStruct((batch, N_EXPERTS, seq), F32),
        ],
        scratch_shapes=_lru_scratch(),
        compiler_params=pltpu.CompilerParams(dimension_semantics=("arbitrary",)),
        name="lru_backward_scan_out_ln_router",
    )(xc3, wg, br, bi, lam, hf3, gg3, x3d, w_out, g, b, wr_t)


def _rotary_tables(seq):
    pos = jnp.arange(seq, dtype=F32)
    inv_freq = ROPE_THETA ** (-jnp.arange(0, ROT_DIM, 2, dtype=F32) / ROT_DIM)
    ang = pos[:, None] * inv_freq[None, :]
    cos, sin = jnp.cos(ang), jnp.sin(ang)
    pad = HEAD_DIM - ROT_DIM
    cos_h = jnp.concatenate([cos, cos, jnp.ones((seq, pad), F32)], axis=1)
    sin_h = jnp.concatenate([-sin, sin, jnp.zeros((seq, pad), F32)], axis=1)
    reps = LANES // HEAD_DIM
    return jnp.tile(cos_h, (1, reps)), jnp.tile(sin_h, (1, reps))


def _moe_block(x1t, aff_t, w_gate_up, w_down, ln_g, ln_b, layer, batch, seq):
    cap = CAPACITY_FACTOR * seq // N_EXPERTS
    idx, gate_t = _route(aff_t, cap)
    return _moe(x1t, idx, gate_t, w_gate_up, w_down, ln_g[layer].reshape(1, -1), ln_b[layer].reshape(1, -1),
                layer, batch, seq)


def kernel(x, attn_w_qkv, attn_w_o, attn_sink, lru_w_in, lru_conv_w, lru_conv_b, lru_w_rgate,
           lru_b_rgate, lru_w_igate, lru_b_igate, lru_lambda, lru_w_out, moe_w_router, moe_w_gate_up,
           moe_w_down, ln_mix_g, ln_mix_b, ln_ffn_g, ln_ffn_b):
    batch, seq, _ = x.shape
    x2d = x.reshape(batch * seq, D_MODEL)
    row = lambda v: v.reshape(1, -1)

    cos_t, sin_t = _rotary_tables(seq)
    q, kv = _qkv_proj(x2d, attn_w_qkv[0].astype(BF16), cos_t, sin_t, seq)
    o = _attention(q, kv, attn_sink[0], batch, seq)
    x1, aff_t = _proj_ln(o, attn_w_o[0].astype(BF16), x2d, row(ln_mix_g[0]), row(ln_mix_b[0]),
                         moe_w_router[0].T.astype(BF16), batch, seq)
    x2 = _moe_block(x1, aff_t, moe_w_gate_up, moe_w_down, ln_ffn_g, ln_ffn_b, 0, batch, seq)

    gg, xc = _lru_in(x2, lru_w_in[0].astype(BF16), lru_conv_w[0], row(lru_conv_b[0]), seq)
    wg = (0.5 * jnp.concatenate([lru_w_rgate[0], lru_w_igate[0]], axis=-1)).astype(BF16)
    half_row = lambda v: 0.5 * v.reshape(1, -1)
    per_seq = lambda t: t.reshape(batch, seq, D_MODEL)
    xc3 = per_seq(xc)
    hf3 = _lru_forward(xc3, wg[0], half_row(lru_b_rgate[0, 0]), half_row(lru_b_igate[0, 0]),
                       row(lru_lambda[0, 0]))
    x3, aff_t = _lru_backward_out(xc3, wg[1], half_row(lru_b_rgate[0, 1]), half_row(lru_b_igate[0, 1]),
                                  row(lru_lambda[0, 1]), hf3, per_seq(gg), per_seq(x2),
                                  lru_w_out[0].astype(BF16), row(ln_mix_g[1]), row(ln_mix_b[1]),
                                  moe_w_router[1].T.astype(BF16))
    x3 = x3.reshape(batch * seq * TOK_ROWS, LANES)
    x4 = _moe_block(x3, aff_t, moe_w_gate_up, moe_w_down, ln_ffn_g, ln_ffn_b, 1, batch, seq)
    return x4.reshape(batch, seq, D_MODEL)
```

```python
import functools

import jax
import jax.numpy as jnp
from jax import lax
from jax.experimental import pallas as pl
from jax.experimental.pallas import tpu as pltpu

F32 = jnp.float32
BF16 = jnp.bfloat16

D_MODEL = 1024
HEAD_DIM = 64
N_Q_HEADS = 16
N_KV_HEADS = 4
GQA_GROUP = 4
WINDOW = 128
BLOCK = 128
ROT_DIM = 16
ROPE_THETA = 500000.0
LRU_BLOCKS = 4
LRU_BLOCK_W = 256
CONV_WIDTH = 4
LRU_C = 8.0
N_EXPERTS = 16
EXPERT_FF = 1024
CAPACITY_FACTOR = 2
LN_EPS = 1e-5
DEPTH = 2
ALPHA = (2 * DEPTH) ** 0.25

LANES = 128
SUBLANES = 8
MXU_COLS = 256
TOK_ROWS = D_MODEL // LANES
VMEM_LIMIT_MOE = 61 * 1024 * 1024


def _layer_norm(v, g, b):
    mu = jnp.mean(v, axis=-1, keepdims=True)
    vc = v - mu
    var = jnp.mean(vc * vc, axis=-1, keepdims=True)
    return vc * lax.rsqrt(var + LN_EPS) * g + b


def _store_token_layout(dst_ref, val):
    n = val.shape[0]
    for s in range(TOK_ROWS):
        dst_ref[pl.ds(s, n, stride=TOK_ROWS), :] = val[:, s * LANES:(s + 1) * LANES]


def _load_token_layout(src_ref, first_token, n):
    base = first_token * TOK_ROWS
    return jnp.concatenate(
        [src_ref[pl.ds(base + s, n, stride=TOK_ROWS), :] for s in range(TOK_ROWS)], axis=1)


def _router_affinity_t(xn, wr_t):
    logits_t = lax.dot_general(wr_t, xn.astype(BF16), (((1,), (1,)), ((), ())),
                               preferred_element_type=F32)
    m = jnp.max(logits_t, axis=0, keepdims=True)
    p = jnp.exp(logits_t - m)
    return p / jnp.sum(p, axis=0, keepdims=True)


def _dot_by_columns(lhs, w_ref, first_col, n_cols):
    return [jnp.dot(lhs, w_ref[:, c:c + MXU_COLS], preferred_element_type=F32)
            for c in range(first_col, first_col + n_cols, MXU_COLS)]


LOG2E = 1.4426950408889634


def _qkv_kernel(x_ref, w_ref, cos_ref, sin_ref, q_ref, kv_ref):
    x = x_ref[...].astype(BF16)
    acc = jnp.dot(x, w_ref[...], preferred_element_type=F32)
    cosb = cos_ref[...]
    sinb = sin_ref[...]
    lane = lax.broadcasted_iota(jnp.int32, cosb.shape, 1)
    low = (lane % HEAD_DIM) < (ROT_DIM // 2)
    nq = N_Q_HEADS * HEAD_DIM
    nk = N_KV_HEADS * HEAD_DIM

    def rot(t):
        up = pltpu.roll(t, LANES - ROT_DIM // 2, axis=1)
        dn = pltpu.roll(t, ROT_DIM // 2, axis=1)
        return t * cosb + jnp.where(low, up, dn) * sinb

    qs = [rot(acc[:, c * LANES:(c + 1) * LANES]) * (LOG2E * HEAD_DIM ** -0.5) for c in range(nq // LANES)]
    q_ref[...] = jnp.concatenate(qs, axis=1).astype(BF16)
    ks = [rot(acc[:, nq + c * LANES:nq + (c + 1) * LANES]) for c in range(nk // LANES)]
    vs = [acc[:, nq + nk + c * LANES:nq + nk + (c + 1) * LANES] for c in range(nk // LANES)]
    swap = lambda t: pltpu.roll(t, HEAD_DIM, axis=1)
    kv_ref[...] = jnp.concatenate(
        ks + [swap(t) for t in ks] + vs + [swap(t) for t in vs], axis=1).astype(BF16)


def _qkv_proj(x2d, w_qkv, cos_t, sin_t, seq):
    T = x2d.shape[0]
    tm = min(512, seq)
    nq = N_Q_HEADS * HEAD_DIM
    nk = N_KV_HEADS * HEAD_DIM
    nts = seq // tm
    return pl.pallas_call(
        _qkv_kernel,
        grid=(T // tm,),
        in_specs=[
            pl.BlockSpec((tm, D_MODEL), lambda i: (i, 0)),
            pl.BlockSpec((D_MODEL, nq + 2 * nk), lambda i: (0, 0)),
            pl.BlockSpec((tm, LANES), lambda i: (i % nts, 0)),
            pl.BlockSpec((tm, LANES), lambda i: (i % nts, 0)),
        ],
        out_specs=[
            pl.BlockSpec((tm, nq), lambda i: (i, 0)),
            pl.BlockSpec((tm, 4 * nk), lambda i: (i, 0)),
        ],
        out_shape=[
            jax.ShapeDtypeStruct((T, nq), BF16),
            jax.ShapeDtypeStruct((T, 4 * nk), BF16),
        ],
        compiler_params=pltpu.CompilerParams(dimension_semantics=("arbitrary",)),
        name="qkv_proj",
    )(x2d, w_qkv, cos_t, sin_t)


QBLKS = 4


def _attn_kernel(sink_ref, q_ref, kvp_ref, kvc_ref, kvn_ref, o_ref, *, seq):
    i = pl.program_id(1)
    nk = N_KV_HEADS * HEAD_DIM
    kv = jnp.concatenate([kvp_ref[...], kvc_ref[...], kvn_ref[...]], axis=0)
    band = 3 * BLOCK
    lo = lax.broadcasted_iota(jnp.int32, (1, LANES), 1) < HEAD_DIM
    zero = jnp.zeros((), BF16)
    k_lo, k_hi, v_dup_t = [], [], []
    for h in range(N_KV_HEADS):
        grp, half = h // 2, h % 2

        def block(base, swapped):
            col = base + (nk if swapped else 0) + grp * LANES
            return kv[:, col:col + LANES]
        k_lo.append(jnp.where(lo, block(0, half == 1), zero))
        k_hi.append(jnp.where(lo, zero, block(0, half == 0)))
        v_dup = jnp.where(lo, block(2 * nk, half == 1), block(2 * nk, half == 0))
        v_dup_t.append(v_dup.astype(F32).T.astype(BF16))

    key = lax.broadcasted_iota(jnp.int32, (BLOCK, BLOCK), 0)
    qry = lax.broadcasted_iota(jnp.int32, (BLOCK, BLOCK), 1)
    feat_lo = lax.broadcasted_iota(jnp.int32, (LANES, 1), 0) < HEAD_DIM
    units = [(j, h) for j in range(QBLKS) for h in range(N_KV_HEADS)]

    def scores_t(j, h):
        rows = slice(j * BLOCK, (j + 1) * BLOCK)
        qp = jnp.concatenate([q_ref[rows, (2 * h) * LANES:(2 * h + 1) * LANES],
                              q_ref[rows, (2 * h + 1) * LANES:(2 * h + 2) * LANES]], axis=0)
        keys = jnp.concatenate([k_lo[h][j * BLOCK:j * BLOCK + band], k_hi[h][j * BLOCK:j * BLOCK + band]],
                               axis=0)
        return lax.dot_general(keys, qp, (((1,), (1,)), ((), ())), preferred_element_type=F32)

    s_next = scores_t(*units[0])
    for u, (j, h) in enumerate(units):
        s = s_next
        if u + 1 < len(units):
            s_next = scores_t(*units[u + 1])
        n = i * QBLKS + j
        ok_prev = jnp.concatenate([(key >= qry) & (n > 0)] * 2, axis=1)
        ok_next = jnp.concatenate([(key <= qry) & (n < seq // BLOCK - 1)] * 2, axis=1)
        vt = v_dup_t[h][:, j * BLOCK:j * BLOCK + band]
        halves = []
        for par in range(2):
            sp = s[par * band:(par + 1) * band, :]
            sp = jnp.concatenate([jnp.where(ok_prev, sp[:BLOCK], -jnp.inf), sp[BLOCK:2 * BLOCK],
                                  jnp.where(ok_next, sp[2 * BLOCK:], -jnp.inf)], axis=0)
            sink_row = jnp.concatenate(
                [jnp.full((1, BLOCK), sink_ref[h * GQA_GROUP + g] * LOG2E, F32) for g in (par, par + 2)],
                axis=1)
            m = jnp.maximum(jnp.max(sp, axis=0, keepdims=True), sink_row)
            p = jnp.exp2(sp - m)
            denom = jnp.sum(p, axis=0, keepdims=True) + jnp.exp2(sink_row - m)
            halves.append(jnp.dot(vt, p.astype(BF16), preferred_element_type=F32) / denom)
        out_t = jnp.where(feat_lo, halves[0], halves[1])
        o_ref[j * BLOCK:(j + 1) * BLOCK, (2 * h) * LANES:(2 * h + 2) * LANES] = jnp.concatenate(
            [out_t[:, :BLOCK].T, out_t[:, BLOCK:].T], axis=1).astype(BF16)


def _attention(q, kv, sink, batch, seq):
    T = q.shape[0]
    qt = QBLKS * BLOCK
    nq = seq // qt
    nb = seq // BLOCK
    wkv = kv.shape[1]
    cur = lambda b, i: (b * nq + i, 0)
    prev = lambda b, i: (b * nb + jnp.maximum(i * QBLKS - 1, 0), 0)
    nxt = lambda b, i: (b * nb + jnp.minimum(i * QBLKS + QBLKS, nb - 1), 0)
    return pl.pallas_call(
        functools.partial(_attn_kernel, seq=seq),
        grid=(batch, nq),
        in_specs=[
            pl.BlockSpec(memory_space=pltpu.SMEM),
            pl.BlockSpec((qt, D_MODEL), cur),
            pl.BlockSpec((BLOCK, wkv), prev),
            pl.BlockSpec((qt, wkv), cur),
            pl.BlockSpec((BLOCK, wkv), nxt),
        ],
        out_specs=pl.BlockSpec((qt, D_MODEL), cur),
        out_shape=jax.ShapeDtypeStruct((T, D_MODEL), BF16),
        compiler_params=pltpu.CompilerParams(dimension_semantics=("arbitrary", "arbitrary")),
        name="swa_attention",
    )(sink, q, kv, kv, kv)


def _proj_ln_kernel(a_ref, w_ref, x_ref, g_ref, b_ref, wr_ref, xo_ref, aff_ref):
    y = jnp.dot(a_ref[...], w_ref[...], preferred_element_type=F32)
    xn = _layer_norm(ALPHA * x_ref[...] + y, g_ref[...], b_ref[...])
    _store_token_layout(xo_ref, xn)
    aff_ref[0] = _router_affinity_t(xn, wr_ref[...])


def _proj_ln(a, w, x2d, g, b, wr_t, batch, seq):
    T = x2d.shape[0]
    tm = min(512, seq)
    nts = seq // tm
    return pl.pallas_call(
        _proj_ln_kernel,
        grid=(T // tm,),
        in_specs=[
            pl.BlockSpec((tm, D_MODEL), lambda i: (i, 0)),
            pl.BlockSpec((D_MODEL, D_MODEL), lambda i: (0, 0)),
            pl.BlockSpec((tm, D_MODEL), lambda i: (i, 0)),
            pl.BlockSpec((1, D_MODEL), lambda i: (0, 0)),
            pl.BlockSpec((1, D_MODEL), lambda i: (0, 0)),
            pl.BlockSpec((N_EXPERTS, D_MODEL), lambda i: (0, 0)),
        ],
        out_specs=[
            pl.BlockSpec((tm * TOK_ROWS, LANES), lambda i: (i, 0)),
            pl.BlockSpec((1, N_EXPERTS, tm), lambda i: (i // nts, 0, i % nts)),
        ],
        out_shape=[
            jax.ShapeDtypeStruct((T * TOK_ROWS, LANES), F32),
            jax.ShapeDtypeStruct((batch, N_EXPERTS, seq), F32),
        ],
        compiler_params=pltpu.CompilerParams(dimension_semantics=("arbitrary",)),
        name="proj_ln_router",
    )(a, w, x2d, g, b, wr_t)


SLOT_BLOCK = 64


def _lane_prefix(mask, tri, dst_ref):
    seq = mask.shape[1]
    off = jnp.zeros((mask.shape[0], 1), F32)
    mb = mask.astype(BF16)
    ends = []
    for c in range(seq // LANES):
        res = jnp.dot(mb[:, c * LANES:(c + 1) * LANES], tri, preferred_element_type=F32) + off
        dst_ref[:, c * LANES:(c + 1) * LANES] = res
        off = res[:, LANES - 1:LANES]
        ends.append(off)
    return ends


def _select_kernel(aff_ref, pos_ref, bnd_ref, *, cap):
    aff = aff_ref[0]
    as_float = lambda bits: pltpu.bitcast(bits, F32)
    rows = aff.shape[0]
    cur = jnp.zeros((rows, 1), jnp.int32)
    for bit in range(30, -1, -1):
        cand = cur | (1 << bit)
        cnt = jnp.sum((aff >= as_float(cand)).astype(jnp.int32), axis=1, keepdims=True)
        cur = jnp.where(cnt >= cap, cand, cur)
    gt = aff >= as_float(cur + 1)
    eq = (aff >= as_float(cur)) & jnp.logical_not(gt)
    n_gt = jnp.sum(gt.astype(jnp.int32), axis=1, keepdims=True)
    ties_taken = (cap - n_gt).astype(F32)
    ri = lax.broadcasted_iota(jnp.int32, (LANES, LANES), 0)
    ci = lax.broadcasted_iota(jnp.int32, (LANES, LANES), 1)
    tri = (ri <= ci).astype(BF16)
    _lane_prefix(eq, tri, pos_ref.at[0])
    sel = gt | (eq & (pos_ref[0] <= ties_taken))
    ends = _lane_prefix(sel, tri, pos_ref.at[0])
    pos_ref[0] = jnp.where(sel, pos_ref[0], 0.0)
    cols = []
    for sb in range(cap // SLOT_BLOCK):
        for target in (sb * SLOT_BLOCK + 1, (sb + 1) * SLOT_BLOCK):
            blk = jnp.zeros((rows, 1), jnp.int32)
            for end in ends[:-1]:
                blk = blk + (end < target).astype(jnp.int32)
            cols.append(blk)
    bnd_ref[0] = jnp.concatenate(cols, axis=1)


def _compact_kernel(bnd_ref, pos_ref, aff_ref, idx_ref, gate_ref, ptok, pgate, *, cap):
    b = pl.program_id(0)
    n_sb = cap // SLOT_BLOCK
    groups = SLOT_BLOCK // SUBLANES
    lane = lax.broadcasted_iota(jnp.int32, (SUBLANES, LANES), 1).astype(F32)
    sub1 = (lax.broadcasted_iota(jnp.int32, (SUBLANES, 1), 0) + 1).astype(F32)
    lane_e = lax.broadcasted_iota(jnp.int32, (cap, N_EXPERTS), 1)
    idx_ref[0] = jnp.zeros((cap, N_EXPERTS), jnp.int32)
    gate_ref[0] = jnp.zeros((cap, N_EXPERTS), F32)

    def expert(e, carry):
        def slot_block(sb, carry2):
            base = ((b * N_EXPERTS + e) * n_sb + sb) * 2
            slot0 = jnp.asarray(sb * SLOT_BLOCK, F32)

            def lane_block(c, accs):
                off = pl.multiple_of(c * LANES, LANES)
                p = jnp.broadcast_to(pos_ref[0, e:e + 1, pl.ds(off, LANES)], (SUBLANES, LANES))
                a = jnp.broadcast_to(aff_ref[0, e:e + 1, pl.ds(off, LANES)], (SUBLANES, LANES))
                tokv = lane + jnp.asarray(c * LANES, F32)
                out = []
                for g in range(groups):
                    hit = p == (slot0 + (g * SUBLANES) + sub1)
                    out.append(accs[2 * g] + jnp.where(hit, tokv, 0.0))
                    out.append(accs[2 * g + 1] + jnp.where(hit, a, 0.0))
                return tuple(out)

            zeros = tuple(jnp.zeros((SUBLANES, LANES), F32) for _ in range(2 * groups))
            accs = lax.fori_loop(bnd_ref[base], bnd_ref[base + 1] + 1, lane_block, zeros)
            for g in range(groups):
                rows = pl.ds(pl.multiple_of(sb * SLOT_BLOCK + g * SUBLANES, SUBLANES), SUBLANES)
                ptok[rows, :] = accs[2 * g]
                pgate[rows, :] = accs[2 * g + 1]
            return carry2

        lax.fori_loop(0, n_sb, slot_block, 0)
        tok_col = jnp.sum(ptok[...], axis=1, keepdims=True).astype(jnp.int32)
        gate_col = jnp.sum(pgate[...], axis=1, keepdims=True)
        idx_ref[0] = jnp.where(lane_e == e, tok_col, idx_ref[0])
        gate_ref[0] = jnp.where(lane_e == e, gate_col, gate_ref[0])
        return carry

    for e in range(N_EXPERTS):
        expert(e, 0)


def _route(aff_t, cap):
    batch, _, seq = aff_t.shape
    n_bnd = 2 * (cap // SLOT_BLOCK)
    seq_spec = pl.BlockSpec((1, N_EXPERTS, seq), lambda b, *_: (b, 0, 0))
    rows = batch * N_EXPERTS
    pos, bnd = pl.pallas_call(
        functools.partial(_select_kernel, cap=cap),
        grid=(1,),
        in_specs=[pl.BlockSpec((1, rows, seq), lambda i: (0, 0, 0))],
        out_specs=[pl.BlockSpec((1, rows, seq), lambda i: (0, 0, 0)),
                   pl.BlockSpec((1, rows, n_bnd), lambda i: (0, 0, 0))],
        out_shape=[
            jax.ShapeDtypeStruct((1, rows, seq), F32),
            jax.ShapeDtypeStruct((1, rows, n_bnd), jnp.int32),
        ],
        compiler_params=pltpu.CompilerParams(dimension_semantics=("arbitrary",)),
        name="expert_choice_select",
    )(aff_t.reshape(1, rows, seq))
    pos = pos.reshape(batch, N_EXPERTS, seq)
    slot_spec = pl.BlockSpec((1, cap, N_EXPERTS), lambda b, *_: (b, 0, 0))
    idx_t, gate_t = pl.pallas_call(
        functools.partial(_compact_kernel, cap=cap),
        grid_spec=pltpu.PrefetchScalarGridSpec(
            num_scalar_prefetch=1,
            grid=(batch,),
            in_specs=[seq_spec, seq_spec],
            out_specs=[slot_spec, slot_spec],
            scratch_shapes=[pltpu.VMEM((cap, LANES), F32), pltpu.VMEM((cap, LANES), F32)],
        ),
        out_shape=[
            jax.ShapeDtypeStruct((batch, cap, N_EXPERTS), jnp.int32),
            jax.ShapeDtypeStruct((batch, cap, N_EXPERTS), F32),
        ],
        compiler_params=pltpu.CompilerParams(dimension_semantics=("arbitrary",)),
        name="expert_choice_compact",
    )(bnd.reshape(-1), pos, aff_t)
    return jnp.swapaxes(idx_t, 1, 2), gate_t


FF_CHUNKS = 2
LN_CHUNK = 256
SCATTER_GROUP = 2


def _moe_kernel(idx_ref, x1t_hbm, gate_ref, wg_ref, wu_ref, wd_ref, lng_ref, lnb_ref, out_hbm,
                x1v, acc, xs3, y3, xs2d, yacc, stage, sem_in, sem_out, *, seq, cap):
    b = pl.program_id(0)
    e = pl.program_id(1)
    j = pl.program_id(2)
    tok_rows = seq * TOK_ROWS
    last_e = N_EXPERTS - 1

    def tok_slice(t):
        return pl.ds(pl.multiple_of(t * TOK_ROWS, TOK_ROWS), TOK_ROWS)

    def gather_rows(expert):
        base = (b * N_EXPERTS + expert) * cap
        for i in range(cap):
            xs3[i * TOK_ROWS:(i + 1) * TOK_ROWS, :] = x1v[tok_slice(idx_ref[base + i]), :]

    def scatter_rows(expert):
        base = (b * N_EXPERTS + expert) * cap
        for g0 in range(0, cap, SCATTER_GROUP):
            toks = [idx_ref[base + g0 + k] for k in range(SCATTER_GROUP)]
            vals = [acc[tok_slice(toks[k]), :] + y3[(g0 + k) * TOK_ROWS:(g0 + k + 1) * TOK_ROWS, :]
                    for k in range(SCATTER_GROUP)]
            for k in range(SCATTER_GROUP):
                acc[tok_slice(toks[k]), :] = vals[k]

    def expert_chunk(slot):
        xs = xs2d[slot]
        g = jnp.dot(xs, wg_ref[0, 0].astype(BF16), preferred_element_type=F32)
        u = jnp.dot(xs, wu_ref[0, 0].astype(BF16), preferred_element_type=F32)
        act = (g * jax.nn.sigmoid(g) * u).astype(BF16)
        return jnp.dot(act, wd_ref[0, 0].astype(BF16), preferred_element_type=F32)

    @pl.when((e == 0) & (j == 0))
    def _start_sequence():
        cp = pltpu.make_async_copy(x1t_hbm.at[pl.ds(b * tok_rows, tok_rows)], x1v, sem_in)
        cp.start()
        acc[...] = jnp.zeros_like(acc)
        yacc[...] = jnp.zeros_like(yacc)
        cp.wait()

        def body(i, carry):
            xs3[tok_slice(i), :] = x1v[tok_slice(idx_ref[b * N_EXPERTS * cap + i]), :]
            return carry
        lax.fori_loop(0, cap, body, 0, unroll=8)
        xs2d[0] = _load_token_layout(xs3, 0, cap).astype(BF16)

    def gate_column(expert):
        gates = gate_ref[0]
        lane_e = lax.broadcasted_iota(jnp.int32, gates.shape, 1)
        return jnp.sum(jnp.where(lane_e == expert, gates, 0.0), axis=1, keepdims=True)

    @pl.when(j == 0)
    def _chunk0():
        prev = jnp.maximum(e - 1, 0)
        _store_token_layout(y3, yacc[...] * gate_column(prev))
        scatter_rows(prev)
        yacc[...] = expert_chunk(e % 2)

    @pl.when(j == 1)
    def _chunk1():
        gather_rows(jnp.minimum(e + 1, last_e))
        yacc[...] += expert_chunk(e % 2)
        xs2d[(e + 1) % 2] = _load_token_layout(xs3, 0, cap).astype(BF16)

    @pl.when((e == last_e) & (j == 1))
    def _finalize():
        _store_token_layout(y3, yacc[...] * gate_column(last_e))

        def body(gi, carry):
            base = (b * N_EXPERTS + last_e) * cap + gi * SUBLANES
            toks = [idx_ref[base + k] for k in range(SUBLANES)]
            vals = [acc[tok_slice(toks[k]), :] + y3[tok_slice(gi * SUBLANES + k), :]
                    for k in range(SUBLANES)]
            for k in range(SUBLANES):
                acc[tok_slice(toks[k]), :] = vals[k]
            return carry
        lax.fori_loop(0, cap // SUBLANES, body, 0)

        n_chunks = seq // LN_CHUNK

        def out_copy(c, slot):
            return pltpu.make_async_copy(
                stage.at[slot], out_hbm.at[pl.ds(b * seq + c * LN_CHUNK, LN_CHUNK)], sem_out.at[slot])

        def ln_body(c, carry):
            slot = c % 2
            first = c * LN_CHUNK
            res = _layer_norm(ALPHA * _load_token_layout(x1v, first, LN_CHUNK)
                              + _load_token_layout(acc, first, LN_CHUNK), lng_ref[...], lnb_ref[...])

            @pl.when(c >= 2)
            def _():
                out_copy(c - 2, slot).wait()
            stage[slot] = res
            out_copy(c, slot).start()
            return carry
        lax.fori_loop(0, n_chunks, ln_body, 0)
        out_copy(n_chunks - 2, (n_chunks - 2) % 2).wait()
        out_copy(n_chunks - 1, (n_chunks - 1) % 2).wait()


def _moe(x1t, idx, gate_t, w_gate_up, w_down, ln_g, ln_b, layer, batch, seq):
    cap = idx.shape[2]
    T = batch * seq
    fc = EXPERT_FF // FF_CHUNKS
    assert FF_CHUNKS == 2
    idx_flat = idx.reshape(-1)
    grid_spec = pltpu.PrefetchScalarGridSpec(
        num_scalar_prefetch=1,
        grid=(batch, N_EXPERTS, FF_CHUNKS),
        in_specs=[
            pl.BlockSpec(memory_space=pl.ANY),
            pl.BlockSpec((1, cap, N_EXPERTS), lambda b, e, j, idx: (b, 0, 0)),
            pl.BlockSpec((1, 1, D_MODEL, fc), lambda b, e, j, idx: (layer, e, 0, j)),
            pl.BlockSpec((1, 1, D_MODEL, fc), lambda b, e, j, idx: (layer, e, 0, FF_CHUNKS + j)),
            pl.BlockSpec((1, 1, fc, D_MODEL), lambda b, e, j, idx: (layer, e, j, 0)),
            pl.BlockSpec((1, D_MODEL), lambda b, e, j, idx: (0, 0)),
            pl.BlockSpec((1, D_MODEL), lambda b, e, j, idx: (0, 0)),
        ],
        out_specs=pl.BlockSpec(memory_space=pl.ANY),
        scratch_shapes=[
            pltpu.VMEM((seq * TOK_ROWS, LANES), F32),
            pltpu.VMEM((seq * TOK_ROWS, LANES), F32),
            pltpu.VMEM((cap * TOK_ROWS, LANES), F32),
            pltpu.VMEM((cap * TOK_ROWS, LANES), F32),
            pltpu.VMEM((2, cap, D_MODEL), BF16),
            pltpu.VMEM((cap, D_MODEL), F32),
            pltpu.VMEM((2, LN_CHUNK, D_MODEL), F32),
            pltpu.SemaphoreType.DMA(()),
            pltpu.SemaphoreType.DMA((2,)),
        ],
    )
    return pl.pallas_call(
        functools.partial(_moe_kernel, seq=seq, cap=cap),
        grid_spec=grid_spec,
        out_shape=jax.ShapeDtypeStruct((T, D_MODEL), F32),
        compiler_params=pltpu.CompilerParams(
            dimension_semantics=("arbitrary", "arbitrary", "arbitrary"),
            vmem_limit_bytes=VMEM_LIMIT_MOE),
        name="expert_ffn_combine_ln",
    )(idx_flat, x1t, gate_t, w_gate_up, w_gate_up, w_down, ln_g, ln_b)


HALO = SUBLANES


def _gelu_tanh(x):
    return 0.5 * x * (1.0 + jnp.tanh(0.7978845608028654 * (x + 0.044715 * (x * x * x))))


def _lru_in_kernel(x_ref, xp_ref, xn_ref, w_ref, cw_ref, cb_ref, gg_ref, xc_ref, *, nts):
    i = pl.program_id(0)
    tm = x_ref.shape[0]
    x = x_ref[...].astype(BF16)
    xp = xp_ref[...].astype(BF16)
    xn = xn_ref[...].astype(BF16)
    first = (i % nts) == 0
    last = (i % nts) == nts - 1
    for k, gate in enumerate(_dot_by_columns(x, w_ref, 0, D_MODEL)):
        gg_ref[:, k * MXU_COLS:(k + 1) * MXU_COLS] = _gelu_tanh(gate)
    n_ext = tm + 2 * HALO
    for k in range(D_MODEL // MXU_COLS):
        cols = slice(k * MXU_COLS, (k + 1) * MXU_COLS)
        xb, = _dot_by_columns(x, w_ref, D_MODEL + k * MXU_COLS, MXU_COLS)
        xbp, = _dot_by_columns(xp, w_ref, D_MODEL + k * MXU_COLS, MXU_COLS)
        xbn, = _dot_by_columns(xn, w_ref, D_MODEL + k * MXU_COLS, MXU_COLS)
        ext = jnp.concatenate([jnp.where(first, 0.0, xbp), xb, jnp.where(last, 0.0, xbn)], axis=0)
        xc = cb_ref[:, cols]
        for tap in range(CONV_WIDTH):
            shift = (2 - tap) % n_ext
            src = ext if shift == 0 else pltpu.roll(ext, shift, axis=0)
            xc = xc + cw_ref[tap:tap + 1, cols] * src[HALO:HALO + tm, :]
        xc_ref[:, cols] = xc


def _lru_in(x2d, w_in, conv_w, conv_b, seq):
    T = x2d.shape[0]
    tm = min(512, seq)
    nts = seq // tm
    hb = tm // HALO
    nblk = T // HALO
    return pl.pallas_call(
        functools.partial(_lru_in_kernel, nts=nts),
        grid=(T // tm,),
        in_specs=[
            pl.BlockSpec((tm, D_MODEL), lambda i: (i, 0)),
            pl.BlockSpec((HALO, D_MODEL), lambda i: (jnp.maximum(i * hb - 1, 0), 0)),
            pl.BlockSpec((HALO, D_MODEL), lambda i: (jnp.minimum((i + 1) * hb, nblk - 1), 0)),
            pl.BlockSpec((D_MODEL, 2 * D_MODEL), lambda i: (0, 0)),
            pl.BlockSpec((CONV_WIDTH, D_MODEL), lambda i: (0, 0)),
            pl.BlockSpec((1, D_MODEL), lambda i: (0, 0)),
        ],
        out_specs=[
            pl.BlockSpec((tm, D_MODEL), lambda i: (i, 0)),
            pl.BlockSpec((tm, D_MODEL), lambda i: (i, 0)),
        ],
        out_shape=[
            jax.ShapeDtypeStruct((T, D_MODEL), F32),
            jax.ShapeDtypeStruct((T, D_MODEL), F32),
        ],
        compiler_params=pltpu.CompilerParams(dimension_semantics=("arbitrary",)),
        name="lru_in_conv",
    )(x2d, x2d, x2d, w_in, conv_w, conv_b)


SCAN_TM = 128
CHAIN_PITCH = SCAN_TM + SUBLANES
PAIRS = D_MODEL // (2 * LANES)


def _lru_chain_scan(xc_ref, wg_ref, br_ref, bi_ref, lam_ref, carry, abuf, ubuf, hbuf, *, reverse, seq):
    i = pl.program_id(0)
    nseq, tm = xc_ref.shape[0], xc_ref.shape[1]
    assert 2 * nseq == SUBLANES and tm == SCAN_TM
    nt = seq // tm
    tile = (nt - 1 - i) if reverse else i

    @pl.when(i == 0)
    def _():
        carry[...] = jnp.zeros_like(carry)

    half_nsp = (-0.5 * LRU_C) * jax.nn.softplus(-lam_ref[...])
    row = lax.broadcasted_iota(jnp.int32, (tm, 1), 0)
    is_start = (tile * tm + row) == (seq - 1 if reverse else 0)
    for b in range(nseq):
        xc = xc_ref[b]
        xcb = xc.astype(BF16)
        r_parts, i_parts = [], []
        for n in range(LRU_BLOCKS):
            res = jnp.dot(xcb[:, n * LRU_BLOCK_W:(n + 1) * LRU_BLOCK_W], wg_ref[n],
                          preferred_element_type=F32)
            r_parts.append(res[:, :LRU_BLOCK_W])
            i_parts.append(res[:, LRU_BLOCK_W:])
        tr = jnp.tanh(jnp.concatenate(r_parts, axis=1) + br_ref[...])
        ig = 0.5 * jnp.tanh(jnp.concatenate(i_parts, axis=1) + bi_ref[...]) + 0.5
        log_a = tr * half_nsp + half_nsp
        a = jnp.exp(log_a)
        z = -jnp.tanh(log_a) * (1.0 + a * a)
        root = jnp.where(z > 0.0, z * lax.rsqrt(z), 0.0)
        u = jnp.where(is_start, 1.0, root) * ig * xc
        for lb in range(2 * PAIRS):
            rows = slice((2 * b + lb % 2) * CHAIN_PITCH, (2 * b + lb % 2) * CHAIN_PITCH + tm)
            abuf[lb // 2, rows, :] = a[:, lb * LANES:(lb + 1) * LANES]
            ubuf[lb // 2, rows, :] = u[:, lb * LANES:(lb + 1) * LANES]

    def step(k, hs):
        t = (tm - 1 - k) if reverse else k
        chains = pl.ds(t, SUBLANES, stride=CHAIN_PITCH)
        out = []
        for m in range(PAIRS):
            h = abuf[m, chains, :] * hs[m] + ubuf[m, chains, :]
            hbuf[m, chains, :] = h
            out.append(h)
        return tuple(out)

    hs = lax.fori_loop(0, tm, step, tuple(carry[m] for m in range(PAIRS)), unroll=8)
    for m in range(PAIRS):
        carry[m] = hs[m]


def _chain_tile(hbuf, b, tm):
    return jnp.concatenate(
        [hbuf[lb // 2, (2 * b + lb % 2) * CHAIN_PITCH:(2 * b + lb % 2) * CHAIN_PITCH + tm, :]
         for lb in range(2 * PAIRS)], axis=1)


def _lru_fwd_kernel(xc_ref, wg_ref, br_ref, bi_ref, lam_ref, h_ref, carry, abuf, ubuf, hbuf, *, seq):
    _lru_chain_scan(xc_ref, wg_ref, br_ref, bi_ref, lam_ref, carry, abuf, ubuf, hbuf,
                    reverse=False, seq=seq)
    for b in range(h_ref.shape[0]):
        h_ref[b] = _chain_tile(hbuf, b, h_ref.shape[1])


def _lru_bwd_kernel(xc_ref, wg_ref, br_ref, bi_ref, lam_ref, hf_ref, gg_ref, x_ref, wo_ref,
                    g_ref, b_ref, wr_ref, xo_ref, aff_ref, carry, abuf, ubuf, hbuf, *, seq):
    _lru_chain_scan(xc_ref, wg_ref, br_ref, bi_ref, lam_ref, carry, abuf, ubuf, hbuf,
                    reverse=True, seq=seq)
    tm = x_ref.shape[1]
    for b in range(x_ref.shape[0]):
        h = hf_ref[b] + _chain_tile(hbuf, b, tm)
        y = jnp.dot((h * gg_ref[b]).astype(BF16), wo_ref[...], preferred_element_type=F32)
        xn = _layer_norm(ALPHA * x_ref[b] + y, g_ref[...], b_ref[...])
        _store_token_layout(xo_ref.at[b], xn)
        aff_ref[b] = _router_affinity_t(xn, wr_ref[...])


def _lru_specs(batch, tile_map):
    const2 = lambda i: (0, 0)
    return [
        pl.BlockSpec((batch, SCAN_TM, D_MODEL), tile_map),
        pl.BlockSpec((LRU_BLOCKS, LRU_BLOCK_W, 2 * LRU_BLOCK_W), lambda i: (0, 0, 0)),
        pl.BlockSpec((1, D_MODEL), const2),
        pl.BlockSpec((1, D_MODEL), const2),
        pl.BlockSpec((1, D_MODEL), const2),
    ]


def _lru_scratch():
    buf = pltpu.VMEM((PAIRS, SUBLANES * CHAIN_PITCH, LANES), F32)
    return [pltpu.VMEM((PAIRS, SUBLANES, LANES), F32), buf, buf, buf]


def _lru_forward(xc3, wg, br, bi, lam):
    batch, seq, _ = xc3.shape
    tile_map = lambda i: (0, i, 0)
    return pl.pallas_call(
        functools.partial(_lru_fwd_kernel, seq=seq),
        grid=(seq // SCAN_TM,),
        in_specs=_lru_specs(batch, tile_map),
        out_specs=pl.BlockSpec((batch, SCAN_TM, D_MODEL), tile_map),
        out_shape=jax.ShapeDtypeStruct((batch, seq, D_MODEL), F32),
        scratch_shapes=_lru_scratch(),
        compiler_params=pltpu.CompilerParams(dimension_semantics=("arbitrary",)),
        name="lru_forward_scan",
    )(xc3, wg, br, bi, lam)


def _lru_backward_out(xc3, wg, br, bi, lam, hf3, gg3, x3d, w_out, g, b, wr_t):
    batch, seq, _ = xc3.shape
    nt = seq // SCAN_TM
    tile_map = lambda i: (0, nt - 1 - i, 0)
    const2 = lambda i: (0, 0)
    tile = pl.BlockSpec((batch, SCAN_TM, D_MODEL), tile_map)
    return pl.pallas_call(
        functools.partial(_lru_bwd_kernel, seq=seq),
        grid=(nt,),
        in_specs=_lru_specs(batch, tile_map) + [
            tile, tile, tile,
            pl.BlockSpec((D_MODEL, D_MODEL), const2),
            pl.BlockSpec((1, D_MODEL), const2),
            pl.BlockSpec((1, D_MODEL), const2),
            pl.BlockSpec((N_EXPERTS, D_MODEL), const2),
        ],
        out_specs=[
            pl.BlockSpec((batch, SCAN_TM * TOK_ROWS, LANES), tile_map),
            pl.BlockSpec((batch, N_EXPERTS, SCAN_TM), lambda i: (0, 0, nt - 1 - i)),
        ],
        out_shape=[
            jax.ShapeDtypeStruct((batch, seq * TOK_ROWS, LANES), F32),
            jax.ShapeDtypeStruct((batch, N_EXPERTS, seq), F32),
        ],
        scratch_shapes=_lru_scratch(),
        compiler_params=pltpu.CompilerParams(dimension_semantics=("arbitrary",)),
        name="lru_backward_scan_out_ln_router",
    )(xc3, wg, br, bi, lam, hf3, gg3, x3d, w_out, g, b, wr_t)


def _rotary_tables(seq):
    pos = jnp.arange(seq, dtype=F32)
    inv_freq = ROPE_THETA ** (-jnp.arange(0, ROT_DIM, 2, dtype=F32) / ROT_DIM)
    ang = pos[:, None] * inv_freq[None, :]
    cos, sin = jnp.cos(ang), jnp.sin(ang)
    pad = HEAD_DIM - ROT_DIM
    cos_h = jnp.concatenate([cos, cos, jnp.ones((seq, pad), F32)], axis=1)
    sin_h = jnp.concatenate([-sin, sin, jnp.zeros((seq, pad), F32)], axis=1)
    reps = LANES // HEAD_DIM
    return jnp.tile(cos_h, (1, reps)), jnp.tile(sin_h, (1, reps))


def _moe_block(x1t, aff_t, w_gate_up, w_down, ln_g, ln_b, layer, batch, seq):
    cap = CAPACITY_FACTOR * seq // N_EXPERTS
    idx, gate_t = _route(aff_t, cap)
    return _moe(x1t, idx, gate_t, w_gate_up, w_down, ln_g[layer].reshape(1, -1), ln_b[layer].reshape(1, -1),
                layer, batch, seq)


def kernel(x, attn_w_qkv, attn_w_o, attn_sink, lru_w_in, lru_conv_w, lru_conv_b, lru_w_rgate,
           lru_b_rgate, lru_w_igate, lru_b_igate, lru_lambda, lru_w_out, moe_w_router, moe_w_gate_up,
           moe_w_down, ln_mix_g, ln_mix_b, ln_ffn_g, ln_ffn_b):
    batch, seq, _ = x.shape
    x2d = x.reshape(batch * seq, D_MODEL)
    row = lambda v: v.reshape(1, -1)

    cos_t, sin_t = _rotary_tables(seq)
    q, kv = _qkv_proj(x2d, attn_w_qkv[0].astype(BF16), cos_t, sin_t, seq)
    o = _attention(q, kv, attn_sink[0], batch, seq)
    x1, aff_t = _proj_ln(o, attn_w_o[0].astype(BF16), x2d, row(ln_mix_g[0]), row(ln_mix_b[0]),
                         moe_w_router[0].T.astype(BF16), batch, seq)
    x2 = _moe_block(x1, aff_t, moe_w_gate_up, moe_w_down, ln_ffn_g, ln_ffn_b, 0, batch, seq)

    gg, xc = _lru_in(x2, lru_w_in[0].astype(BF16), lru_conv_w[0], row(lru_conv_b[0]), seq)
    wg = (0.5 * jnp.concatenate([lru_w_rgate[0], lru_w_igate[0]], axis=-1)).astype(BF16)
    half_row = lambda v: 0.5 * v.reshape(1, -1)
    per_seq = lambda t: t.reshape(batch, seq, D_MODEL)
    xc3 = per_seq(xc)
    hf3 = _lru_forward(xc3, wg[0], half_row(lru_b_rgate[0, 0]), half_row(lru_b_igate[0, 0]),
                       row(lru_lambda[0, 0]))
    x3, aff_t = _lru_backward_out(xc3, wg[1], half_row(lru_b_rgate[0, 1]), half_row(lru_b_igate[0, 1]),
                                  row(lru_lambda[0, 1]), hf3, per_seq(gg), per_seq(x2),
                                  lru_w_out[0].astype(BF16), row(ln_mix_g[1]), row(ln_mix_b[1]),
                                  moe_w_router[1].T.astype(BF16))
    x3 = x3.reshape(batch * seq * TOK_ROWS, LANES)
    x4 = _moe_block(x3, aff_t, moe_w_gate_up, moe_w_down, ln_ffn_g, ln_ffn_b, 1, batch, seq)
    return x4.reshape(batch, seq, D_MODEL)
```

```python
import functools

import jax
import jax.numpy as jnp
from jax import lax
from jax.experimental import pallas as pl
from jax.experimental.pallas import tpu as pltpu

F32 = jnp.float32
BF16 = jnp.bfloat16

D_MODEL = 1024
HEAD_DIM = 64
N_Q_HEADS = 16
N_KV_HEADS = 4
GQA_GROUP = 4
WINDOW = 128
BLOCK = 128
ROT_DIM = 16
ROPE_THETA = 500000.0
LRU_BLOCKS = 4
LRU_BLOCK_W = 256
CONV_WIDTH = 4
LRU_C = 8.0
N_EXPERTS = 16
EXPERT_FF = 1024
CAPACITY_FACTOR = 2
LN_EPS = 1e-5
DEPTH = 2
ALPHA = (2 * DEPTH) ** 0.25

LANES = 128
SUBLANES = 8
MXU_COLS = 256
TOK_ROWS = D_MODEL // LANES
VMEM_LIMIT_MOE = 61 * 1024 * 1024


def _layer_norm(v, g, b):
    mu = jnp.mean(v, axis=-1, keepdims=True)
    vc = v - mu
    var = jnp.mean(vc * vc, axis=-1, keepdims=True)
    return vc * lax.rsqrt(var + LN_EPS) * g + b


def _store_token_layout(dst_ref, val):
    n = val.shape[0]
    for s in range(TOK_ROWS):
        dst_ref[pl.ds(s, n, stride=TOK_ROWS), :] = val[:, s * LANES:(s + 1) * LANES]


def _load_token_layout(src_ref, first_token, n):
    base = first_token * TOK_ROWS
    return jnp.concatenate(
        [src_ref[pl.ds(base + s, n, stride=TOK_ROWS), :] for s in range(TOK_ROWS)], axis=1)


def _router_affinity_t(xn, wr_t):
    logits_t = lax.dot_general(wr_t, xn.astype(BF16), (((1,), (1,)), ((), ())),
                               preferred_element_type=F32)
    m = jnp.max(logits_t, axis=0, keepdims=True)
    p = jnp.exp(logits_t - m)
    return p / jnp.sum(p, axis=0, keepdims=True)


def _dot_by_columns(lhs, w_ref, first_col, n_cols):
    return [jnp.dot(lhs, w_ref[:, c:c + MXU_COLS], preferred_element_type=F32)
            for c in range(first_col, first_col + n_cols, MXU_COLS)]


LOG2E = 1.4426950408889634


def _qkv_kernel(x_ref, w_ref, cos_ref, sin_ref, q_ref, kv_ref):
    x = x_ref[...].astype(BF16)
    acc = jnp.dot(x, w_ref[...], preferred_element_type=F32)
    cosb = cos_ref[...]
    sinb = sin_ref[...]
    lane = lax.broadcasted_iota(jnp.int32, cosb.shape, 1)
    low = (lane % HEAD_DIM) < (ROT_DIM // 2)
    nq = N_Q_HEADS * HEAD_DIM
    nk = N_KV_HEADS * HEAD_DIM

    def rot(t):
        up = pltpu.roll(t, LANES - ROT_DIM // 2, axis=1)
        dn = pltpu.roll(t, ROT_DIM // 2, axis=1)
        return t * cosb + jnp.where(low, up, dn) * sinb

    qs = [rot(acc[:, c * LANES:(c + 1) * LANES]) * (LOG2E * HEAD_DIM ** -0.5) for c in range(nq // LANES)]
    q_ref[...] = jnp.concatenate(qs, axis=1).astype(BF16)
    ks = [rot(acc[:, nq + c * LANES:nq + (c + 1) * LANES]) for c in range(nk // LANES)]
    vs = [acc[:, nq + nk + c * LANES:nq + nk + (c + 1) * LANES] for c in range(nk // LANES)]
    swap = lambda t: pltpu.roll(t, HEAD_DIM, axis=1)
    kv_ref[...] = jnp.concatenate(
        ks + [swap(t) for t in ks] + vs + [swap(t) for t in vs], axis=1).astype(BF16)


def _qkv_proj(x2d, w_qkv, cos_t, sin_t, seq):
    T = x2d.shape[0]
    tm = min(512, seq)
    nq = N_Q_HEADS * HEAD_DIM
    nk = N_KV_HEADS * HEAD_DIM
    nts = seq // tm
    return pl.pallas_call(
        _qkv_kernel,
        grid=(T // tm,),
        in_specs=[
            pl.BlockSpec((tm, D_MODEL), lambda i: (i, 0)),
            pl.BlockSpec((D_MODEL, nq + 2 * nk), lambda i: (0, 0)),
            pl.BlockSpec((tm, LANES), lambda i: (i % nts, 0)),
            pl.BlockSpec((tm, LANES), lambda i: (i % nts, 0)),
        ],
        out_specs=[
            pl.BlockSpec((tm, nq), lambda i: (i, 0)),
            pl.BlockSpec((tm, 4 * nk), lambda i: (i, 0)),
        ],
        out_shape=[
            jax.ShapeDtypeStruct((T, nq), BF16),
            jax.ShapeDtypeStruct((T, 4 * nk), BF16),
        ],
        compiler_params=pltpu.CompilerParams(dimension_semantics=("arbitrary",)),
        name="qkv_proj",
    )(x2d, w_qkv, cos_t, sin_t)


QBLKS = 4


def _attn_kernel(sink_ref, q_ref, kvp_ref, kvc_ref, kvn_ref, o_ref, *, seq):
    i = pl.program_id(1)
    nk = N_KV_HEADS * HEAD_DIM
    kv = jnp.concatenate([kvp_ref[...], kvc_ref[...], kvn_ref[...]], axis=0)
    band = 3 * BLOCK
    lo = lax.broadcasted_iota(jnp.int32, (1, LANES), 1) < HEAD_DIM
    zero = jnp.zeros((), BF16)
    k_lo, k_hi, v_dup_t = [], [], []
    for h in range(N_KV_HEADS):
        grp, half = h // 2, h % 2

        def block(base, swapped):
            col = base + (nk if swapped else 0) + grp * LANES
            return kv[:, col:col + LANES]
        k_lo.append(jnp.where(lo, block(0, half == 1), zero))
        k_hi.append(jnp.where(lo, zero, block(0, half == 0)))
        v_dup = jnp.where(lo, block(2 * nk, half == 1), block(2 * nk, half == 0))
        v_dup_t.append(v_dup.astype(F32).T.astype(BF16))

    key = lax.broadcasted_iota(jnp.int32, (BLOCK, BLOCK), 0)
    qry = lax.broadcasted_iota(jnp.int32, (BLOCK, BLOCK), 1)
    feat_lo = lax.broadcasted_iota(jnp.int32, (LANES, 1), 0) < HEAD_DIM
    units = [(j, h) for j in range(QBLKS) for h in range(N_KV_HEADS)]

    def scores_t(j, h):
        rows = slice(j * BLOCK, (j + 1) * BLOCK)
        qp = jnp.concatenate([q_ref[rows, (2 * h) * LANES:(2 * h + 1) * LANES],
                              q_ref[rows, (2 * h + 1) * LANES:(2 * h + 2) * LANES]], axis=0)
        keys = jnp.concatenate([k_lo[h][j * BLOCK:j * BLOCK + band], k_hi[h][j * BLOCK:j * BLOCK + band]],
                               axis=0)
        return lax.dot_general(keys, qp, (((1,), (1,)), ((), ())), preferred_element_type=F32)

    s_next = scores_t(*units[0])
    for u, (j, h) in enumerate(units):
        s = s_next
        if u + 1 < len(units):
            s_next = scores_t(*units[u + 1])
        n = i * QBLKS + j
        ok_prev = jnp.concatenate([(key >= qry) & (n > 0)] * 2, axis=1)
        ok_next = jnp.concatenate([(key <= qry) & (n < seq // BLOCK - 1)] * 2, axis=1)
        vt = v_dup_t[h][:, j * BLOCK:j * BLOCK + band]
        halves = []
        for par in range(2):
            sp = s[par * band:(par + 1) * band, :]
            sp = jnp.concatenate([jnp.where(ok_prev, sp[:BLOCK], -jnp.inf), sp[BLOCK:2 * BLOCK],
                                  jnp.where(ok_next, sp[2 * BLOCK:], -jnp.inf)], axis=0)
            sink_row = jnp.concatenate(
                [jnp.full((1, BLOCK), sink_ref[h * GQA_GROUP + g] * LOG2E, F32) for g in (par, par + 2)],
                axis=1)
            m = jnp.maximum(jnp.max(sp, axis=0, keepdims=True), sink_row)
            p = jnp.exp2(sp - m)
            denom = jnp.sum(p, axis=0, keepdims=True) + jnp.exp2(sink_row - m)
            halves.append(jnp.dot(vt, p.astype(BF16), preferred_element_type=F32) / denom)
        out_t = jnp.where(feat_lo, halves[0], halves[1])
        o_ref[j * BLOCK:(j + 1) * BLOCK, (2 * h) * LANES:(2 * h + 2) * LANES] = jnp.concatenate(
            [out_t[:, :BLOCK].T, out_t[:, BLOCK:].T], axis=1).astype(BF16)


def _attention(q, kv, sink, batch, seq):
    T = q.shape[0]
    qt = QBLKS * BLOCK
    nq = seq // qt
    nb = seq // BLOCK
    wkv = kv.shape[1]
    cur = lambda b, i: (b * nq + i, 0)
    prev = lambda b, i: (b * nb + jnp.maximum(i * QBLKS - 1, 0), 0)
    nxt = lambda b, i: (b * nb + jnp.minimum(i * QBLKS + QBLKS, nb - 1), 0)
    return pl.pallas_call(
        functools.partial(_attn_kernel, seq=seq),
        grid=(batch, nq),
        in_specs=[
            pl.BlockSpec(memory_space=pltpu.SMEM),
            pl.BlockSpec((qt, D_MODEL), cur),
            pl.BlockSpec((BLOCK, wkv), prev),
            pl.BlockSpec((qt, wkv), cur),
            pl.BlockSpec((BLOCK, wkv), nxt),
        ],
        out_specs=pl.BlockSpec((qt, D_MODEL), cur),
        out_shape=jax.ShapeDtypeStruct((T, D_MODEL), BF16),
        compiler_params=pltpu.CompilerParams(dimension_semantics=("arbitrary", "arbitrary")),
        name="swa_attention",
    )(sink, q, kv, kv, kv)


def _proj_ln_kernel(a_ref, w_ref, x_ref, g_ref, b_ref, wr_ref, xo_ref, aff_ref):
    y = jnp.dot(a_ref[...], w_ref[...], preferred_element_type=F32)
    xn = _layer_norm(ALPHA * x_ref[...] + y, g_ref[...], b_ref[...])
    _store_token_layout(xo_ref, xn)
    aff_ref[0] = _router_affinity_t(xn, wr_ref[...])


def _proj_ln(a, w, x2d, g, b, wr_t, batch, seq):
    T = x2d.shape[0]
    tm = min(512, seq)
    nts = seq // tm
    return pl.pallas_call(
        _proj_ln_kernel,
        grid=(T // tm,),
        in_specs=[
            pl.BlockSpec((tm, D_MODEL), lambda i: (i, 0)),
            pl.BlockSpec((D_MODEL, D_MODEL), lambda i: (0, 0)),
            pl.BlockSpec((tm, D_MODEL), lambda i: (i, 0)),
            pl.BlockSpec((1, D_MODEL), lambda i: (0, 0)),
            pl.BlockSpec((1, D_MODEL), lambda i: (0, 0)),
            pl.BlockSpec((N_EXPERTS, D_MODEL), lambda i: (0, 0)),
        ],
        out_specs=[
            pl.BlockSpec((tm * TOK_ROWS, LANES), lambda i: (i, 0)),
            pl.BlockSpec((1, N_EXPERTS, tm), lambda i: (i // nts, 0, i % nts)),
        ],
        out_shape=[
            jax.ShapeDtypeStruct((T * TOK_ROWS, LANES), F32),
            jax.ShapeDtypeStruct((batch, N_EXPERTS, seq), F32),
        ],
        compiler_params=pltpu.CompilerParams(dimension_semantics=("arbitrary",)),
        name="proj_ln_router",
    )(a, w, x2d, g, b, wr_t)


SLOT_BLOCK = 64


def _lane_prefix(mask, tri, dst_ref):
    seq = mask.shape[1]
    off = jnp.zeros((mask.shape[0], 1), F32)
    mb = mask.astype(BF16)
    ends = []
    for c in range(seq // LANES):
        res = jnp.dot(mb[:, c * LANES:(c + 1) * LANES], tri, preferred_element_type=F32) + off
        dst_ref[:, c * LANES:(c + 1) * LANES] = res
        off = res[:, LANES - 1:LANES]
        ends.append(off)
    return ends


def _select_kernel(aff_ref, pos_ref, bnd_ref, *, cap):
    aff = aff_ref[0]
    as_float = lambda bits: pltpu.bitcast(bits, F32)
    rows = aff.shape[0]
    cur = jnp.zeros((rows, 1), jnp.int32)
    for bit in range(30, -1, -1):
        cand = cur | (1 << bit)
        cnt = jnp.sum((aff >= as_float(cand)).astype(jnp.int32), axis=1, keepdims=True)
        cur = jnp.where(cnt >= cap, cand, cur)
    gt = aff >= as_float(cur + 1)
    eq = (aff >= as_float(cur)) & jnp.logical_not(gt)
    n_gt = jnp.sum(gt.astype(jnp.int32), axis=1, keepdims=True)
    ties_taken = (cap - n_gt).astype(F32)
    ri = lax.broadcasted_iota(jnp.int32, (LANES, LANES), 0)
    ci = lax.broadcasted_iota(jnp.int32, (LANES, LANES), 1)
    tri = (ri <= ci).astype(BF16)
    _lane_prefix(eq, tri, pos_ref.at[0])
    sel = gt | (eq & (pos_ref[0] <= ties_taken))
    ends = _lane_prefix(sel, tri, pos_ref.at[0])
    pos_ref[0] = jnp.where(sel, pos_ref[0], 0.0)
    cols = []
    for sb in range(cap // SLOT_BLOCK):
        for target in (sb * SLOT_BLOCK + 1, (sb + 1) * SLOT_BLOCK):
            blk = jnp.zeros((rows, 1), jnp.int32)
            for end in ends[:-1]:
                blk = blk + (end < target).astype(jnp.int32)
            cols.append(blk)
    bnd_ref[0] = jnp.concatenate(cols, axis=1)


def _compact_kernel(bnd_ref, pos_ref, aff_ref, idx_ref, gate_ref, ptok, pgate, *, cap):
    b = pl.program_id(0)
    n_sb = cap // SLOT_BLOCK
    groups = SLOT_BLOCK // SUBLANES
    lane = lax.broadcasted_iota(jnp.int32, (SUBLANES, LANES), 1).astype(F32)
    sub1 = (lax.broadcasted_iota(jnp.int32, (SUBLANES, 1), 0) + 1).astype(F32)
    lane_e = lax.broadcasted_iota(jnp.int32, (cap, N_EXPERTS), 1)
    idx_ref[0] = jnp.zeros((cap, N_EXPERTS), jnp.int32)
    gate_ref[0] = jnp.zeros((cap, N_EXPERTS), F32)

    def expert(e, carry):
        def slot_block(sb, carry2):
            base = ((b * N_EXPERTS + e) * n_sb + sb) * 2
            slot0 = jnp.asarray(sb * SLOT_BLOCK, F32)

            def lane_block(c, accs):
                off = pl.multiple_of(c * LANES, LANES)
                p = jnp.broadcast_to(pos_ref[0, e:e + 1, pl.ds(off, LANES)], (SUBLANES, LANES))
                a = jnp.broadcast_to(aff_ref[0, e:e + 1, pl.ds(off, LANES)], (SUBLANES, LANES))
                tokv = lane + jnp.asarray(c * LANES, F32)
                out = []
                for g in range(groups):
                    hit = p == (slot0 + (g * SUBLANES) + sub1)
                    out.append(accs[2 * g] + jnp.where(hit, tokv, 0.0))
                    out.append(accs[2 * g + 1] + jnp.where(hit, a, 0.0))
                return tuple(out)

            zeros = tuple(jnp.zeros((SUBLANES, LANES), F32) for _ in range(2 * groups))
            accs = lax.fori_loop(bnd_ref[base], bnd_ref[base + 1] + 1, lane_block, zeros)
            for g in range(groups):
                rows = pl.ds(pl.multiple_of(sb * SLOT_BLOCK + g * SUBLANES, SUBLANES), SUBLANES)
                ptok[rows, :] = accs[2 * g]
                pgate[rows, :] = accs[2 * g + 1]
            return carry2

        lax.fori_loop(0, n_sb, slot_block, 0)
        tok_col = jnp.sum(ptok[...], axis=1, keepdims=True).astype(jnp.int32)
        gate_col = jnp.sum(pgate[...], axis=1, keepdims=True)
        idx_ref[0] = jnp.where(lane_e == e, tok_col, idx_ref[0])
        gate_ref[0] = jnp.where(lane_e == e, gate_col, gate_ref[0])
        return carry

    for e in range(N_EXPERTS):
        expert(e, 0)


def _route(aff_t, cap):
    batch, _, seq = aff_t.shape
    n_bnd = 2 * (cap // SLOT_BLOCK)
    seq_spec = pl.BlockSpec((1, N_EXPERTS, seq), lambda b, *_: (b, 0, 0))
    rows = batch * N_EXPERTS
    pos, bnd = pl.pallas_call(
        functools.partial(_select_kernel, cap=cap),
        grid=(1,),
        in_specs=[pl.BlockSpec((1, rows, seq), lambda i: (0, 0, 0))],
        out_specs=[pl.BlockSpec((1, rows, seq), lambda i: (0, 0, 0)),
                   pl.BlockSpec((1, rows, n_bnd), lambda i: (0, 0, 0))],
        out_shape=[
            jax.ShapeDtypeStruct((1, rows, seq), F32),
            jax.ShapeDtypeStruct((1, rows, n_bnd), jnp.int32),
        ],
        compiler_params=pltpu.CompilerParams(dimension_semantics=("arbitrary",)),
        name="expert_choice_select",
    )(aff_t.reshape(1, rows, seq))
    pos = pos.reshape(batch, N_EXPERTS, seq)
    slot_spec = pl.BlockSpec((1, cap, N_EXPERTS), lambda b, *_: (b, 0, 0))
    idx_t, gate_t = pl.pallas_call(
        functools.partial(_compact_kernel, cap=cap),
        grid_spec=pltpu.PrefetchScalarGridSpec(
            num_scalar_prefetch=1,
            grid=(batch,),
            in_specs=[seq_spec, seq_spec],
            out_specs=[slot_spec, slot_spec],
            scratch_shapes=[pltpu.VMEM((cap, LANES), F32), pltpu.VMEM((cap, LANES), F32)],
        ),
        out_shape=[
            jax.ShapeDtypeStruct((batch, cap, N_EXPERTS), jnp.int32),
            jax.ShapeDtypeStruct((batch, cap, N_EXPERTS), F32),
        ],
        compiler_params=pltpu.CompilerParams(dimension_semantics=("arbitrary",)),
        name="expert_choice_compact",
    )(bnd.reshape(-1), pos, aff_t)
    return jnp.swapaxes(idx_t, 1, 2), gate_t


FF_CHUNKS = 2
LN_CHUNK = 256
SCATTER_GROUP = 2


def _moe_kernel(idx_ref, x1t_hbm, gate_ref, wg_ref, wu_ref, wd_ref, lng_ref, lnb_ref, out_hbm,
                x1v, acc, xs3, y3, xs2d, yacc, stage, sem_in, sem_out, *, seq, cap):
    b = pl.program_id(0)
    e = pl.program_id(1)
    j = pl.program_id(2)
    tok_rows = seq * TOK_ROWS
    last_e = N_EXPERTS - 1

    def tok_slice(t):
        return pl.ds(pl.multiple_of(t * TOK_ROWS, TOK_ROWS), TOK_ROWS)

    def gather_rows(expert):
        base = (b * N_EXPERTS + expert) * cap
        for i in range(cap):
            xs3[i * TOK_ROWS:(i + 1) * TOK_ROWS, :] = x1v[tok_slice(idx_ref[base + i]), :]

    def scatter_rows(expert):
        base = (b * N_EXPERTS + expert) * cap
        for g0 in range(0, cap, SCATTER_GROUP):
            toks = [idx_ref[base + g0 + k] for k in range(SCATTER_GROUP)]
            vals = [acc[tok_slice(toks[k]), :] + y3[(g0 + k) * TOK_ROWS:(g0 + k + 1) * TOK_ROWS, :]
                    for k in range(SCATTER_GROUP)]
            for k in range(SCATTER_GROUP):
                acc[tok_slice(toks[k]), :] = vals[k]

    def expert_chunk(slot):
        xs = xs2d[slot]
        g = jnp.dot(xs, wg_ref[0, 0].astype(BF16), preferred_element_type=F32)
        u = jnp.dot(xs, wu_ref[0, 0].astype(BF16), preferred_element_type=F32)
        act = (g * jax.nn.sigmoid(g) * u).astype(BF16)
        return jnp.dot(act, wd_ref[0, 0].astype(BF16), preferred_element_type=F32)

    def load_sequence(seq_idx):
        return pltpu.make_async_copy(x1t_hbm.at[pl.ds(seq_idx * tok_rows, tok_rows)], x1v, sem_in)

    def init_accumulator():
        acc[...] = ALPHA * x1v[...]

    @pl.when((b == 0) & (e == 0) & (j == 0))
    def _first_sequence():
        load_sequence(0).start()
        load_sequence(0).wait()
        init_accumulator()

    @pl.when((e == 0) & (j == 0))
    def _start_sequence():
        yacc[...] = jnp.zeros_like(yacc)

        def body(i, carry):
            xs3[tok_slice(i), :] = x1v[tok_slice(idx_ref[b * N_EXPERTS * cap + i]), :]
            return carry
        lax.fori_loop(0, cap, body, 0, unroll=8)
        xs2d[0] = _load_token_layout(xs3, 0, cap).astype(BF16)

    def gate_column(expert):
        gates = gate_ref[0]
        lane_e = lax.broadcasted_iota(jnp.int32, gates.shape, 1)
        return jnp.sum(jnp.where(lane_e == expert, gates, 0.0), axis=1, keepdims=True)

    @pl.when(j == 0)
    def _chunk0():
        prev = jnp.maximum(e - 1, 0)
        _store_token_layout(y3, yacc[...] * gate_column(prev))
        scatter_rows(prev)
        yacc[...] = expert_chunk(e % 2)

    @pl.when(j == 1)
    def _chunk1():
        gather_rows(jnp.minimum(e + 1, last_e))
        yacc[...] += expert_chunk(e % 2)
        xs2d[(e + 1) % 2] = _load_token_layout(xs3, 0, cap).astype(BF16)

    @pl.when((e == last_e) & (j == 1))
    def _finalize():
        _store_token_layout(y3, yacc[...] * gate_column(last_e))

        def body(gi, carry):
            base = (b * N_EXPERTS + last_e) * cap + gi * SUBLANES
            toks = [idx_ref[base + k] for k in range(SUBLANES)]
            vals = [acc[tok_slice(toks[k]), :] + y3[tok_slice(gi * SUBLANES + k), :]
                    for k in range(SUBLANES)]
            for k in range(SUBLANES):
                acc[tok_slice(toks[k]), :] = vals[k]
            return carry
        lax.fori_loop(0, cap // SUBLANES, body, 0)

        n_seq = pl.num_programs(0)

        @pl.when(b + 1 < n_seq)
        def _():
            load_sequence(b + 1).start()

        n_chunks = seq // LN_CHUNK

        def out_copy(c, slot):
            return pltpu.make_async_copy(
                stage.at[slot], out_hbm.at[pl.ds(b * seq + c * LN_CHUNK, LN_CHUNK)], sem_out.at[slot])

        def ln_body(c, carry):
            slot = c % 2
            first = c * LN_CHUNK
            res = _layer_norm(_load_token_layout(acc, first, LN_CHUNK), lng_ref[...], lnb_ref[...])

            @pl.when(c >= 2)
            def _():
                out_copy(c - 2, slot).wait()
            stage[slot] = res
            out_copy(c, slot).start()
            return carry
        lax.fori_loop(0, n_chunks, ln_body, 0)
        out_copy(n_chunks - 2, (n_chunks - 2) % 2).wait()
        out_copy(n_chunks - 1, (n_chunks - 1) % 2).wait()

        @pl.when(b + 1 < n_seq)
        def _():
            load_sequence(b + 1).wait()
            init_accumulator()


def _moe(x1t, idx, gate_t, w_gate_up, w_down, ln_g, ln_b, layer, batch, seq):
    cap = idx.shape[2]
    T = batch * seq
    fc = EXPERT_FF // FF_CHUNKS
    assert FF_CHUNKS == 2
    idx_flat = idx.reshape(-1)
    grid_spec = pltpu.PrefetchScalarGridSpec(
        num_scalar_prefetch=1,
        grid=(batch, N_EXPERTS, FF_CHUNKS),
        in_specs=[
            pl.BlockSpec(memory_space=pl.ANY),
            pl.BlockSpec((1, cap, N_EXPERTS), lambda b, e, j, idx: (b, 0, 0)),
            pl.BlockSpec((1, 1, D_MODEL, fc), lambda b, e, j, idx: (layer, e, 0, j)),
            pl.BlockSpec((1, 1, D_MODEL, fc), lambda b, e, j, idx: (layer, e, 0, FF_CHUNKS + j)),
            pl.BlockSpec((1, 1, fc, D_MODEL), lambda b, e, j, idx: (layer, e, j, 0)),
            pl.BlockSpec((1, D_MODEL), lambda b, e, j, idx: (0, 0)),
            pl.BlockSpec((1, D_MODEL), lambda b, e, j, idx: (0, 0)),
        ],
        out_specs=pl.BlockSpec(memory_space=pl.ANY),
        scratch_shapes=[
            pltpu.VMEM((seq * TOK_ROWS, LANES), F32),
            pltpu.VMEM((seq * TOK_ROWS, LANES), F32),
            pltpu.VMEM((cap * TOK_ROWS, LANES), F32),
            pltpu.VMEM((cap * TOK_ROWS, LANES), F32),
            pltpu.VMEM((2, cap, D_MODEL), BF16),
            pltpu.VMEM((cap, D_MODEL), F32),
            pltpu.VMEM((2, LN_CHUNK, D_MODEL), F32),
            pltpu.SemaphoreType.DMA(()),
            pltpu.SemaphoreType.DMA((2,)),
        ],
    )
    return pl.pallas_call(
        functools.partial(_moe_kernel, seq=seq, cap=cap),
        grid_spec=grid_spec,
        out_shape=jax.ShapeDtypeStruct((T, D_MODEL), F32),
        compiler_params=pltpu.CompilerParams(
            dimension_semantics=("arbitrary", "arbitrary", "arbitrary"),
            vmem_limit_bytes=VMEM_LIMIT_MOE),
        name="expert_ffn_combine_ln",
    )(idx_flat, x1t, gate_t, w_gate_up, w_gate_up, w_down, ln_g, ln_b)


HALO = SUBLANES


def _gelu_tanh(x):
    return 0.5 * x * (1.0 + jnp.tanh(0.7978845608028654 * (x + 0.044715 * (x * x * x))))


def _lru_in_kernel(x_ref, xp_ref, xn_ref, w_ref, cw_ref, cb_ref, gg_ref, xc_ref, *, nts):
    i = pl.program_id(0)
    tm = x_ref.shape[0]
    x = x_ref[...].astype(BF16)
    xp = xp_ref[...].astype(BF16)
    xn = xn_ref[...].astype(BF16)
    first = (i % nts) == 0
    last = (i % nts) == nts - 1
    for k, gate in enumerate(_dot_by_columns(x, w_ref, 0, D_MODEL)):
        gg_ref[:, k * MXU_COLS:(k + 1) * MXU_COLS] = _gelu_tanh(gate)
    n_ext = tm + 2 * HALO
    for k in range(D_MODEL // MXU_COLS):
        cols = slice(k * MXU_COLS, (k + 1) * MXU_COLS)
        xb, = _dot_by_columns(x, w_ref, D_MODEL + k * MXU_COLS, MXU_COLS)
        xbp, = _dot_by_columns(xp, w_ref, D_MODEL + k * MXU_COLS, MXU_COLS)
        xbn, = _dot_by_columns(xn, w_ref, D_MODEL + k * MXU_COLS, MXU_COLS)
        ext = jnp.concatenate([jnp.where(first, 0.0, xbp), xb, jnp.where(last, 0.0, xbn)], axis=0)
        xc = cb_ref[:, cols]
        for tap in range(CONV_WIDTH):
            shift = (2 - tap) % n_ext
            src = ext if shift == 0 else pltpu.roll(ext, shift, axis=0)
            xc = xc + cw_ref[tap:tap + 1, cols] * src[HALO:HALO + tm, :]
        xc_ref[:, cols] = xc


def _lru_in(x2d, w_in, conv_w, conv_b, seq):
    T = x2d.shape[0]
    tm = min(512, seq)
    nts = seq // tm
    hb = tm // HALO
    nblk = T // HALO
    return pl.pallas_call(
        functools.partial(_lru_in_kernel, nts=nts),
        grid=(T // tm,),
        in_specs=[
            pl.BlockSpec((tm, D_MODEL), lambda i: (i, 0)),
            pl.BlockSpec((HALO, D_MODEL), lambda i: (jnp.maximum(i * hb - 1, 0), 0)),
            pl.BlockSpec((HALO, D_MODEL), lambda i: (jnp.minimum((i + 1) * hb, nblk - 1), 0)),
            pl.BlockSpec((D_MODEL, 2 * D_MODEL), lambda i: (0, 0)),
            pl.BlockSpec((CONV_WIDTH, D_MODEL), lambda i: (0, 0)),
            pl.BlockSpec((1, D_MODEL), lambda i: (0, 0)),
        ],
        out_specs=[
            pl.BlockSpec((tm, D_MODEL), lambda i: (i, 0)),
            pl.BlockSpec((tm, D_MODEL), lambda i: (i, 0)),
        ],
        out_shape=[
            jax.ShapeDtypeStruct((T, D_MODEL), F32),
            jax.ShapeDtypeStruct((T, D_MODEL), F32),
        ],
        compiler_params=pltpu.CompilerParams(dimension_semantics=("arbitrary",)),
        name="lru_in_conv",
    )(x2d, x2d, x2d, w_in, conv_w, conv_b)


SCAN_TM = 128
CHAIN_PITCH = SCAN_TM + SUBLANES
PAIRS = D_MODEL // (2 * LANES)


def _lru_chain_scan(xc_ref, wg_ref, br_ref, bi_ref, lam_ref, carry, abuf, ubuf, hbuf, *, reverse, seq,
                    before_sequence=None):
    i = pl.program_id(0)
    nseq, tm = xc_ref.shape[0], xc_ref.shape[1]
    assert 2 * nseq == SUBLANES and tm == SCAN_TM
    nt = seq // tm
    tile = (nt - 1 - i) if reverse else i
    half_nsp = (-0.5 * LRU_C) * jax.nn.softplus(-lam_ref[...])
    row = lax.broadcasted_iota(jnp.int32, (tm, 1), 0)
    is_start = (tile * tm + row) == (seq - 1 if reverse else 0)
    for b in range(nseq):
        if before_sequence is not None:
            before_sequence(b)
        xc = xc_ref[b]
        xcb = xc.astype(BF16)
        r_parts, i_parts = [], []
        for n in range(LRU_BLOCKS):
            res = jnp.dot(xcb[:, n * LRU_BLOCK_W:(n + 1) * LRU_BLOCK_W], wg_ref[n],
                          preferred_element_type=F32)
            r_parts.append(res[:, :LRU_BLOCK_W])
            i_parts.append(res[:, LRU_BLOCK_W:])
        tr = jnp.tanh(jnp.concatenate(r_parts, axis=1) + br_ref[...])
        ig = 0.5 * jnp.tanh(jnp.concatenate(i_parts, axis=1) + bi_ref[...]) + 0.5
        log_a = tr * half_nsp + half_nsp
        a = jnp.exp(log_a)
        z = -jnp.tanh(log_a) * (1.0 + a * a)
        root = jnp.where(z > 0.0, z * lax.rsqrt(z), 0.0)
        u = jnp.where(is_start, 1.0, root) * ig * xc
        for lb in range(2 * PAIRS):
            rows = slice((2 * b + lb % 2) * CHAIN_PITCH, (2 * b + lb % 2) * CHAIN_PITCH + tm)
            abuf[lb // 2, rows, :] = a[:, lb * LANES:(lb + 1) * LANES]
            ubuf[lb // 2, rows, :] = u[:, lb * LANES:(lb + 1) * LANES]

    def step(k, hs):
        t = (tm - 1 - k) if reverse else k
        chains = pl.ds(t, SUBLANES, stride=CHAIN_PITCH)
        out = []
        for m in range(PAIRS):
            h = abuf[m, chains, :] * hs[m] + ubuf[m, chains, :]
            hbuf[m, chains, :] = h
            out.append(h)
        return tuple(out)

    hs = lax.fori_loop(0, tm, step, tuple(carry[m] for m in range(PAIRS)), unroll=8)
    for m in range(PAIRS):
        carry[m] = hs[m]


def _chain_tile(hbuf, b, tm):
    return jnp.concatenate(
        [hbuf[lb // 2, (2 * b + lb % 2) * CHAIN_PITCH:(2 * b + lb % 2) * CHAIN_PITCH + tm, :]
         for lb in range(2 * PAIRS)], axis=1)


def _lru_fwd_kernel(xc_ref, wg_ref, br_ref, bi_ref, lam_ref, h_ref, carry, abuf, ubuf, hbuf, *, seq):
    @pl.when(pl.program_id(0) == 0)
    def _():
        carry[...] = jnp.zeros_like(carry)

    _lru_chain_scan(xc_ref, wg_ref, br_ref, bi_ref, lam_ref, carry, abuf, ubuf, hbuf,
                    reverse=False, seq=seq)
    for b in range(h_ref.shape[0]):
        h_ref[b] = _chain_tile(hbuf, b, h_ref.shape[1])


def _lru_bwd_kernel(xc_ref, wg_ref, br_ref, bi_ref, lam_ref, hf_ref, gg_ref, x_ref, wo_ref,
                    g_ref, b_ref, wr_ref, xo_ref, aff_ref, carry, abuf, ubuf, hbuf, *, seq):
    @pl.when(pl.program_id(0) == 0)
    def _():
        carry[...] = jnp.zeros_like(carry)
        hbuf[...] = jnp.zeros_like(hbuf)

    tm = x_ref.shape[1]

    def finish_sequence(b):
        h = hf_ref[b] + _chain_tile(hbuf, b, tm)
        y = jnp.dot((h * gg_ref[b]).astype(BF16), wo_ref[...], preferred_element_type=F32)
        xn = _layer_norm(ALPHA * x_ref[b] + y, g_ref[...], b_ref[...])
        _store_token_layout(xo_ref.at[b], xn)
        aff_ref[b] = _router_affinity_t(xn, wr_ref[...])

    _lru_chain_scan(xc_ref, wg_ref, br_ref, bi_ref, lam_ref, carry, abuf, ubuf, hbuf,
                    reverse=True, seq=seq, before_sequence=finish_sequence)


def _lru_specs(batch, tile_map):
    const2 = lambda i: (0, 0)
    return [
        pl.BlockSpec((batch, SCAN_TM, D_MODEL), tile_map),
        pl.BlockSpec((LRU_BLOCKS, LRU_BLOCK_W, 2 * LRU_BLOCK_W), lambda i: (0, 0, 0)),
        pl.BlockSpec((1, D_MODEL), const2),
        pl.BlockSpec((1, D_MODEL), const2),
        pl.BlockSpec((1, D_MODEL), const2),
    ]


def _lru_scratch():
    buf = pltpu.VMEM((PAIRS, SUBLANES * CHAIN_PITCH, LANES), F32)
    return [pltpu.VMEM((PAIRS, SUBLANES, LANES), F32), buf, buf, buf]


def _lru_forward(xc3, wg, br, bi, lam):
    batch, seq, _ = xc3.shape
    tile_map = lambda i: (0, i, 0)
    return pl.pallas_call(
        functools.partial(_lru_fwd_kernel, seq=seq),
        grid=(seq // SCAN_TM,),
        in_specs=_lru_specs(batch, tile_map),
        out_specs=pl.BlockSpec((batch, SCAN_TM, D_MODEL), tile_map),
        out_shape=jax.ShapeDtypeStruct((batch, seq, D_MODEL), F32),
        scratch_shapes=_lru_scratch(),
        compiler_params=pltpu.CompilerParams(dimension_semantics=("arbitrary",)),
        name="lru_forward_scan",
    )(xc3, wg, br, bi, lam)


def _lru_backward_out(xc3, wg, br, bi, lam, hf3, gg3, x3d, w_out, g, b, wr_t):
    batch, seq, _ = xc3.shape
    nt = seq // SCAN_TM
    scan_map = lambda i: (0, jnp.maximum(nt - 1 - i, 0), 0)
    tile_map = lambda i: (0, jnp.minimum(nt - i, nt - 1), 0)
    const2 = lambda i: (0, 0)
    tile = pl.BlockSpec((batch, SCAN_TM, D_MODEL), tile_map)
    return pl.pallas_call(
        functools.partial(_lru_bwd_kernel, seq=seq),
        grid=(nt + 1,),
        in_specs=_lru_specs(batch, scan_map) + [
            tile, tile, tile,
            pl.BlockSpec((D_MODEL, D_MODEL), const2),
            pl.BlockSpec((1, D_MODEL), const2),
            pl.BlockSpec((1, D_MODEL), const2),
            pl.BlockSpec((N_EXPERTS, D_MODEL), const2),
        ],
        out_specs=[
            pl.BlockSpec((batch, SCAN_TM * TOK_ROWS, LANES), tile_map),
            pl.BlockSpec((batch, N_EXPERTS, SCAN_TM), lambda i: (0, 0, jnp.minimum(nt - i, nt - 1))),
        ],
        out_shape=[
            jax.ShapeDtypeStruct((batch, seq * TOK_ROWS, LANES), F32),
            jax.ShapeDtypeStruct((batch, N_EXPERTS, seq), F32),
        ],
        scratch_shapes=_lru_scratch(),
        compiler_params=pltpu.CompilerParams(dimension_semantics=("arbitrary",)),
        name="lru_backward_scan_out_ln_router",
    )(xc3, wg, br, bi, lam, hf3, gg3, x3d, w_out, g, b, wr_t)


def _rotary_tables(seq):
    pos = jnp.arange(seq, dtype=F32)
    inv_freq = ROPE_THETA ** (-jnp.arange(0, ROT_DIM, 2, dtype=F32) / ROT_DIM)
    ang = pos[:, None] * inv_freq[None, :]
    cos, sin = jnp.cos(ang), jnp.sin(ang)
    pad = HEAD_DIM - ROT_DIM
    cos_h = jnp.concatenate([cos, cos, jnp.ones((seq, pad), F32)], axis=1)
    sin_h = jnp.concatenate([-sin, sin, jnp.zeros((seq, pad), F32)], axis=1)
    reps = LANES // HEAD_DIM
    return jnp.tile(cos_h, (1, reps)), jnp.tile(sin_h, (1, reps))


def _moe_block(x1t, aff_t, w_gate_up, w_down, ln_g, ln_b, layer, batch, seq):
    cap = CAPACITY_FACTOR * seq // N_EXPERTS
    idx, gate_t = _route(aff_t, cap)
    return _moe(x1t, idx, gate_t, w_gate_up, w_down, ln_g[layer].reshape(1, -1), ln_b[layer].reshape(1, -1),
                layer, batch, seq)


def kernel(x, attn_w_qkv, attn_w_o, attn_sink, lru_w_in, lru_conv_w, lru_conv_b, lru_w_rgate,
           lru_b_rgate, lru_w_igate, lru_b_igate, lru_lambda, lru_w_out, moe_w_router, moe_w_gate_up,
           moe_w_down, ln_mix_g, ln_mix_b, ln_ffn_g, ln_ffn_b):
    batch, seq, _ = x.shape
    x2d = x.reshape(batch * seq, D_MODEL)
    row = lambda v: v.reshape(1, -1)

    cos_t, sin_t = _rotary_tables(seq)
    q, kv = _qkv_proj(x2d, attn_w_qkv[0].astype(BF16), cos_t, sin_t, seq)
    o = _attention(q, kv, attn_sink[0], batch, seq)
    x1, aff_t = _proj_ln(o, attn_w_o[0].astype(BF16), x2d, row(ln_mix_g[0]), row(ln_mix_b[0]),
                         moe_w_router[0].T.astype(BF16), batch, seq)
    x2 = _moe_block(x1, aff_t, moe_w_gate_up, moe_w_down, ln_ffn_g, ln_ffn_b, 0, batch, seq)

    gg, xc = _lru_in(x2, lru_w_in[0].astype(BF16), lru_conv_w[0], row(lru_conv_b[0]), seq)
    wg = (0.5 * jnp.concatenate([lru_w_rgate[0], lru_w_igate[0]], axis=-1)).astype(BF16)
    half_row = lambda v: 0.5 * v.reshape(1, -1)
    per_seq = lambda t: t.reshape(batch, seq, D_MODEL)
    xc3 = per_seq(xc)
    hf3 = _lru_forward(xc3, wg[0], half_row(lru_b_rgate[0, 0]), half_row(lru_b_igate[0, 0]),
                       row(lru_lambda[0, 0]))
    x3, aff_t = _lru_backward_out(xc3, wg[1], half_row(lru_b_rgate[0, 1]), half_row(lru_b_igate[0, 1]),
                                  row(lru_lambda[0, 1]), hf3, per_seq(gg), per_seq(x2),
                                  lru_w_out[0].astype(BF16), row(ln_mix_g[1]), row(ln_mix_b[1]),
                                  moe_w_router[1].T.astype(BF16))
    x3 = x3.reshape(batch * seq * TOK_ROWS, LANES)
    x4 = _moe_block(x3, aff_t, moe_w_gate_up, moe_w_down, ln_ffn_g, ln_ffn_b, 1, batch, seq)
    return x4.reshape(batch, seq, D_MODEL)
```

```python
import functools

import jax
import jax.numpy as jnp
from jax import lax
from jax.experimental import pallas as pl
from jax.experimental.pallas import tpu as pltpu

F32 = jnp.float32
BF16 = jnp.bfloat16

D_MODEL = 1024
HEAD_DIM = 64
N_Q_HEADS = 16
N_KV_HEADS = 4
GQA_GROUP = 4
WINDOW = 128
BLOCK = 128
ROT_DIM = 16
ROPE_THETA = 500000.0
LRU_BLOCKS = 4
LRU_BLOCK_W = 256
CONV_WIDTH = 4
LRU_C = 8.0
N_EXPERTS = 16
EXPERT_FF = 1024
CAPACITY_FACTOR = 2
LN_EPS = 1e-5
DEPTH = 2
ALPHA = (2 * DEPTH) ** 0.25

LANES = 128
SUBLANES = 8
MXU_COLS = 256
TOK_ROWS = D_MODEL // LANES
VMEM_LIMIT_MOE = 61 * 1024 * 1024


def _layer_norm(v, g, b):
    mu = jnp.mean(v, axis=-1, keepdims=True)
    vc = v - mu
    var = jnp.mean(vc * vc, axis=-1, keepdims=True)
    return vc * lax.rsqrt(var + LN_EPS) * g + b


def _store_token_layout(dst_ref, val):
    n = val.shape[0]
    for s in range(TOK_ROWS):
        dst_ref[pl.ds(s, n, stride=TOK_ROWS), :] = val[:, s * LANES:(s + 1) * LANES]


def _load_token_layout(src_ref, first_token, n):
    base = first_token * TOK_ROWS
    return jnp.concatenate(
        [src_ref[pl.ds(base + s, n, stride=TOK_ROWS), :] for s in range(TOK_ROWS)], axis=1)


def _router_affinity_t(xn, wr_t):
    logits_t = lax.dot_general(wr_t, xn.astype(BF16), (((1,), (1,)), ((), ())),
                               preferred_element_type=F32)
    m = jnp.max(logits_t, axis=0, keepdims=True)
    p = jnp.exp(logits_t - m)
    return p / jnp.sum(p, axis=0, keepdims=True)


def _dot_by_columns(lhs, w_ref, first_col, n_cols):
    return [jnp.dot(lhs, w_ref[:, c:c + MXU_COLS], preferred_element_type=F32)
            for c in range(first_col, first_col + n_cols, MXU_COLS)]


LOG2E = 1.4426950408889634


def _qkv_kernel(x_ref, w_ref, cos_ref, sin_ref, q_ref, kv_ref):
    x = x_ref[...].astype(BF16)
    acc = jnp.dot(x, w_ref[...], preferred_element_type=F32)
    cosb = cos_ref[...]
    sinb = sin_ref[...]
    lane = lax.broadcasted_iota(jnp.int32, cosb.shape, 1)
    low = (lane % HEAD_DIM) < (ROT_DIM // 2)
    nq = N_Q_HEADS * HEAD_DIM
    nk = N_KV_HEADS * HEAD_DIM

    def rot(t):
        up = pltpu.roll(t, LANES - ROT_DIM // 2, axis=1)
        dn = pltpu.roll(t, ROT_DIM // 2, axis=1)
        return t * cosb + jnp.where(low, up, dn) * sinb

    qs = [rot(acc[:, c * LANES:(c + 1) * LANES]) * (LOG2E * HEAD_DIM ** -0.5) for c in range(nq // LANES)]
    q_ref[...] = jnp.concatenate(qs, axis=1).astype(BF16)
    ks = [rot(acc[:, nq + c * LANES:nq + (c + 1) * LANES]) for c in range(nk // LANES)]
    vs = [acc[:, nq + nk + c * LANES:nq + nk + (c + 1) * LANES] for c in range(nk // LANES)]
    swap = lambda t: pltpu.roll(t, HEAD_DIM, axis=1)
    kv_ref[...] = jnp.concatenate(
        ks + [swap(t) for t in ks] + vs + [swap(t) for t in vs], axis=1).astype(BF16)


def _qkv_proj(x2d, w_qkv, cos_t, sin_t, seq):
    T = x2d.shape[0]
    tm = min(512, seq)
    nq = N_Q_HEADS * HEAD_DIM
    nk = N_KV_HEADS * HEAD_DIM
    nts = seq // tm
    return pl.pallas_call(
        _qkv_kernel,
        grid=(T // tm,),
        in_specs=[
            pl.BlockSpec((tm, D_MODEL), lambda i: (i, 0)),
            pl.BlockSpec((D_MODEL, nq + 2 * nk), lambda i: (0, 0)),
            pl.BlockSpec((tm, LANES), lambda i: (i % nts, 0)),
            pl.BlockSpec((tm, LANES), lambda i: (i % nts, 0)),
        ],
        out_specs=[
            pl.BlockSpec((tm, nq), lambda i: (i, 0)),
            pl.BlockSpec((tm, 4 * nk), lambda i: (i, 0)),
        ],
        out_shape=[
            jax.ShapeDtypeStruct((T, nq), BF16),
            jax.ShapeDtypeStruct((T, 4 * nk), BF16),
        ],
        compiler_params=pltpu.CompilerParams(dimension_semantics=("arbitrary",)),
        name="qkv_proj",
    )(x2d, w_qkv, cos_t, sin_t)


QBLKS = 8


def _attn_kernel(sink_ref, q_ref, kvp_ref, kvc_ref, kvn_ref, o_ref, *, seq):
    i = pl.program_id(1)
    nk = N_KV_HEADS * HEAD_DIM
    kv = jnp.concatenate([kvp_ref[...], kvc_ref[...], kvn_ref[...]], axis=0)
    band = 3 * BLOCK
    lo = lax.broadcasted_iota(jnp.int32, (1, LANES), 1) < HEAD_DIM
    zero = jnp.zeros((), BF16)
    k_lo, k_hi, v_dup_t = [], [], []
    for h in range(N_KV_HEADS):
        grp, half = h // 2, h % 2

        def block(base, swapped):
            col = base + (nk if swapped else 0) + grp * LANES
            return kv[:, col:col + LANES]
        k_lo.append(jnp.where(lo, block(0, half == 1), zero))
        k_hi.append(jnp.where(lo, zero, block(0, half == 0)))
        v_dup = jnp.where(lo, block(2 * nk, half == 1), block(2 * nk, half == 0))
        v_dup_t.append(v_dup.astype(F32).T.astype(BF16))

    key = lax.broadcasted_iota(jnp.int32, (BLOCK, BLOCK), 0)
    qry = lax.broadcasted_iota(jnp.int32, (BLOCK, BLOCK), 1)
    feat_lo = lax.broadcasted_iota(jnp.int32, (LANES, 1), 0) < HEAD_DIM
    units = [(j, h) for j in range(QBLKS) for h in range(N_KV_HEADS)]

    def scores_t(j, h):
        rows = slice(j * BLOCK, (j + 1) * BLOCK)
        qp = jnp.concatenate([q_ref[rows, (2 * h) * LANES:(2 * h + 1) * LANES],
                              q_ref[rows, (2 * h + 1) * LANES:(2 * h + 2) * LANES]], axis=0)
        keys = jnp.concatenate([k_lo[h][j * BLOCK:j * BLOCK + band], k_hi[h][j * BLOCK:j * BLOCK + band]],
                               axis=0)
        return lax.dot_general(keys, qp, (((1,), (1,)), ((), ())), preferred_element_type=F32)

    s_next = scores_t(*units[0])
    for u, (j, h) in enumerate(units):
        s = s_next
        if u + 1 < len(units):
            s_next = scores_t(*units[u + 1])
        n = i * QBLKS + j
        ok_prev = jnp.concatenate([(key >= qry) & (n > 0)] * 2, axis=1)
        ok_next = jnp.concatenate([(key <= qry) & (n < seq // BLOCK - 1)] * 2, axis=1)
        vt = v_dup_t[h][:, j * BLOCK:j * BLOCK + band]
        halves = []
        for par in range(2):
            sp = s[par * band:(par + 1) * band, :]
            sp = jnp.concatenate([jnp.where(ok_prev, sp[:BLOCK], -jnp.inf), sp[BLOCK:2 * BLOCK],
                                  jnp.where(ok_next, sp[2 * BLOCK:], -jnp.inf)], axis=0)
            sink_row = jnp.concatenate(
                [jnp.full((1, BLOCK), sink_ref[h * GQA_GROUP + g] * LOG2E, F32) for g in (par, par + 2)],
                axis=1)
            m = jnp.maximum(jnp.max(sp, axis=0, keepdims=True), sink_row)
            p = jnp.exp2(sp - m)
            denom = jnp.sum(p, axis=0, keepdims=True) + jnp.exp2(sink_row - m)
            halves.append(jnp.dot(vt, p.astype(BF16), preferred_element_type=F32) / denom)
        out_t = jnp.where(feat_lo, halves[0], halves[1])
        o_ref[j * BLOCK:(j + 1) * BLOCK, (2 * h) * LANES:(2 * h + 2) * LANES] = jnp.concatenate(
            [out_t[:, :BLOCK].T, out_t[:, BLOCK:].T], axis=1).astype(BF16)


def _attention(q, kv, sink, batch, seq):
    T = q.shape[0]
    qt = QBLKS * BLOCK
    nq = seq // qt
    nb = seq // BLOCK
    wkv = kv.shape[1]
    cur = lambda b, i: (b * nq + i, 0)
    prev = lambda b, i: (b * nb + jnp.maximum(i * QBLKS - 1, 0), 0)
    nxt = lambda b, i: (b * nb + jnp.minimum(i * QBLKS + QBLKS, nb - 1), 0)
    return pl.pallas_call(
        functools.partial(_attn_kernel, seq=seq),
        grid=(batch, nq),
        in_specs=[
            pl.BlockSpec(memory_space=pltpu.SMEM),
            pl.BlockSpec((qt, D_MODEL), cur),
            pl.BlockSpec((BLOCK, wkv), prev),
            pl.BlockSpec((qt, wkv), cur),
            pl.BlockSpec((BLOCK, wkv), nxt),
        ],
        out_specs=pl.BlockSpec((qt, D_MODEL), cur),
        out_shape=jax.ShapeDtypeStruct((T, D_MODEL), BF16),
        compiler_params=pltpu.CompilerParams(dimension_semantics=("arbitrary", "arbitrary")),
        name="swa_attention",
    )(sink, q, kv, kv, kv)


def _proj_ln_kernel(a_ref, w_ref, x_ref, g_ref, b_ref, wr_ref, xo_ref, aff_ref):
    y = jnp.dot(a_ref[...], w_ref[...], preferred_element_type=F32)
    xn = _layer_norm(ALPHA * x_ref[...] + y, g_ref[...], b_ref[...])
    _store_token_layout(xo_ref, xn)
    aff_ref[0] = _router_affinity_t(xn, wr_ref[...])


def _proj_ln(a, w, x2d, g, b, wr_t, batch, seq):
    T = x2d.shape[0]
    tm = min(512, seq)
    nts = seq // tm
    return pl.pallas_call(
        _proj_ln_kernel,
        grid=(T // tm,),
        in_specs=[
            pl.BlockSpec((tm, D_MODEL), lambda i: (i, 0)),
            pl.BlockSpec((D_MODEL, D_MODEL), lambda i: (0, 0)),
            pl.BlockSpec((tm, D_MODEL), lambda i: (i, 0)),
            pl.BlockSpec((1, D_MODEL), lambda i: (0, 0)),
            pl.BlockSpec((1, D_MODEL), lambda i: (0, 0)),
            pl.BlockSpec((N_EXPERTS, D_MODEL), lambda i: (0, 0)),
        ],
        out_specs=[
            pl.BlockSpec((tm * TOK_ROWS, LANES), lambda i: (i, 0)),
            pl.BlockSpec((1, N_EXPERTS, tm), lambda i: (i // nts, 0, i % nts)),
        ],
        out_shape=[
            jax.ShapeDtypeStruct((T * TOK_ROWS, LANES), F32),
            jax.ShapeDtypeStruct((batch, N_EXPERTS, seq), F32),
        ],
        compiler_params=pltpu.CompilerParams(dimension_semantics=("arbitrary",)),
        name="proj_ln_router",
    )(a, w, x2d, g, b, wr_t)


SLOT_BLOCK = 64


def _lane_prefix(mask, tri, dst_ref):
    seq = mask.shape[1]
    off = jnp.zeros((mask.shape[0], 1), F32)
    mb = mask.astype(BF16)
    ends = []
    for c in range(seq // LANES):
        res = jnp.dot(mb[:, c * LANES:(c + 1) * LANES], tri, preferred_element_type=F32) + off
        dst_ref[:, c * LANES:(c + 1) * LANES] = res
        off = res[:, LANES - 1:LANES]
        ends.append(off)
    return ends


def _select_kernel(aff_ref, pos_ref, bnd_ref, *, cap):
    aff = aff_ref[0]
    as_float = lambda bits: pltpu.bitcast(bits, F32)
    rows = aff.shape[0]
    cur = jnp.zeros((rows, 1), jnp.int32)
    for bit in range(30, -1, -1):
        cand = cur | (1 << bit)
        cnt = jnp.sum((aff >= as_float(cand)).astype(jnp.int32), axis=1, keepdims=True)
        cur = jnp.where(cnt >= cap, cand, cur)
    gt = aff >= as_float(cur + 1)
    eq = (aff >= as_float(cur)) & jnp.logical_not(gt)
    n_gt = jnp.sum(gt.astype(jnp.int32), axis=1, keepdims=True)
    ties_taken = (cap - n_gt).astype(F32)
    ri = lax.broadcasted_iota(jnp.int32, (LANES, LANES), 0)
    ci = lax.broadcasted_iota(jnp.int32, (LANES, LANES), 1)
    tri = (ri <= ci).astype(BF16)
    _lane_prefix(eq, tri, pos_ref.at[0])
    sel = gt | (eq & (pos_ref[0] <= ties_taken))
    ends = _lane_prefix(sel, tri, pos_ref.at[0])
    pos_ref[0] = jnp.where(sel, pos_ref[0], 0.0)
    cols = []
    for sb in range(cap // SLOT_BLOCK):
        for target in (sb * SLOT_BLOCK + 1, (sb + 1) * SLOT_BLOCK):
            blk = jnp.zeros((rows, 1), jnp.int32)
            for end in ends[:-1]:
                blk = blk + (end < target).astype(jnp.int32)
            cols.append(blk)
    bnd_ref[0] = jnp.concatenate(cols, axis=1)


def _compact_kernel(bnd_ref, pos_ref, aff_ref, idx_ref, gate_ref, ptok, pgate, *, cap):
    b = pl.program_id(0)
    n_sb = cap // SLOT_BLOCK
    groups = SLOT_BLOCK // SUBLANES
    lane = lax.broadcasted_iota(jnp.int32, (SUBLANES, LANES), 1).astype(F32)
    sub1 = (lax.broadcasted_iota(jnp.int32, (SUBLANES, 1), 0) + 1).astype(F32)
    lane_e = lax.broadcasted_iota(jnp.int32, (cap, N_EXPERTS), 1)
    idx_ref[0] = jnp.zeros((cap, N_EXPERTS), jnp.int32)
    gate_ref[0] = jnp.zeros((cap, N_EXPERTS), F32)

    def expert(e, carry):
        def slot_block(sb, carry2):
            base = ((b * N_EXPERTS + e) * n_sb + sb) * 2
            slot0 = jnp.asarray(sb * SLOT_BLOCK, F32)

            def lane_block(c, accs):
                off = pl.multiple_of(c * LANES, LANES)
                p = jnp.broadcast_to(pos_ref[0, e:e + 1, pl.ds(off, LANES)], (SUBLANES, LANES))
                a = jnp.broadcast_to(aff_ref[0, e:e + 1, pl.ds(off, LANES)], (SUBLANES, LANES))
                tokv = lane + jnp.asarray(c * LANES, F32)
                out = []
                for g in range(groups):
                    hit = p == (slot0 + (g * SUBLANES) + sub1)
                    out.append(jnp.where(hit, tokv, accs[2 * g]))
                    out.append(jnp.where(hit, a, accs[2 * g + 1]))
                return tuple(out)

            zeros = tuple(jnp.zeros((SUBLANES, LANES), F32) for _ in range(2 * groups))
            accs = lax.fori_loop(bnd_ref[base], bnd_ref[base + 1] + 1, lane_block, zeros)
            for g in range(groups):
                rows = pl.ds(pl.multiple_of(sb * SLOT_BLOCK + g * SUBLANES, SUBLANES), SUBLANES)
                ptok[rows, :] = accs[2 * g]
                pgate[rows, :] = accs[2 * g + 1]
            return carry2

        lax.fori_loop(0, n_sb, slot_block, 0)
        tok_col = jnp.sum(ptok[...], axis=1, keepdims=True).astype(jnp.int32)
        gate_col = jnp.sum(pgate[...], axis=1, keepdims=True)
        idx_ref[0] = jnp.where(lane_e == e, tok_col, idx_ref[0])
        gate_ref[0] = jnp.where(lane_e == e, gate_col, gate_ref[0])
        return carry

    for e in range(N_EXPERTS):
        expert(e, 0)


def _route(aff_t, cap):
    batch, _, seq = aff_t.shape
    n_bnd = 2 * (cap // SLOT_BLOCK)
    seq_spec = pl.BlockSpec((1, N_EXPERTS, seq), lambda b, *_: (b, 0, 0))
    rows = batch * N_EXPERTS
    pos, bnd = pl.pallas_call(
        functools.partial(_select_kernel, cap=cap),
        grid=(1,),
        in_specs=[pl.BlockSpec((1, rows, seq), lambda i: (0, 0, 0))],
        out_specs=[pl.BlockSpec((1, rows, seq), lambda i: (0, 0, 0)),
                   pl.BlockSpec((1, rows, n_bnd), lambda i: (0, 0, 0))],
        out_shape=[
            jax.ShapeDtypeStruct((1, rows, seq), F32),
            jax.ShapeDtypeStruct((1, rows, n_bnd), jnp.int32),
        ],
        compiler_params=pltpu.CompilerParams(dimension_semantics=("arbitrary",)),
        name="expert_choice_select",
    )(aff_t.reshape(1, rows, seq))
    pos = pos.reshape(batch, N_EXPERTS, seq)
    slot_spec = pl.BlockSpec((1, cap, N_EXPERTS), lambda b, *_: (b, 0, 0))
    idx_t, gate_t = pl.pallas_call(
        functools.partial(_compact_kernel, cap=cap),
        grid_spec=pltpu.PrefetchScalarGridSpec(
            num_scalar_prefetch=1,
            grid=(batch,),
            in_specs=[seq_spec, seq_spec],
            out_specs=[slot_spec, slot_spec],
            scratch_shapes=[pltpu.VMEM((cap, LANES), F32), pltpu.VMEM((cap, LANES), F32)],
        ),
        out_shape=[
            jax.ShapeDtypeStruct((batch, cap, N_EXPERTS), jnp.int32),
            jax.ShapeDtypeStruct((batch, cap, N_EXPERTS), F32),
        ],
        compiler_params=pltpu.CompilerParams(dimension_semantics=("arbitrary",)),
        name="expert_choice_compact",
    )(bnd.reshape(-1), pos, aff_t)
    return jnp.swapaxes(idx_t, 1, 2), gate_t


FF_CHUNKS = 2
LN_CHUNK = 256
SCATTER_GROUP = 2


def _moe_kernel(idx_ref, x1t_hbm, gate_ref, wg_ref, wu_ref, wd_ref, lng_ref, lnb_ref, out_hbm,
                x1v, acc, xs3, y3, xs2d, yacc, stage, sem_in, sem_out, *, seq, cap):
    b = pl.program_id(0)
    e = pl.program_id(1)
    j = pl.program_id(2)
    tok_rows = seq * TOK_ROWS
    last_e = N_EXPERTS - 1

    def tok_slice(t):
        return pl.ds(pl.multiple_of(t * TOK_ROWS, TOK_ROWS), TOK_ROWS)

    def gather_rows(expert):
        base = (b * N_EXPERTS + expert) * cap
        for i in range(cap):
            xs3[i * TOK_ROWS:(i + 1) * TOK_ROWS, :] = x1v[tok_slice(idx_ref[base + i]), :]

    def scatter_rows(expert):
        base = (b * N_EXPERTS + expert) * cap
        for g0 in range(0, cap, SCATTER_GROUP):
            toks = [idx_ref[base + g0 + k] for k in range(SCATTER_GROUP)]
            vals = [acc[tok_slice(toks[k]), :] + y3[(g0 + k) * TOK_ROWS:(g0 + k + 1) * TOK_ROWS, :]
                    for k in range(SCATTER_GROUP)]
            for k in range(SCATTER_GROUP):
                acc[tok_slice(toks[k]), :] = vals[k]

    def expert_chunk(slot):
        xs = xs2d[slot]
        g = jnp.dot(xs, wg_ref[0, 0].astype(BF16), preferred_element_type=F32)
        u = jnp.dot(xs, wu_ref[0, 0].astype(BF16), preferred_element_type=F32)
        act = (g * jax.nn.sigmoid(g) * u).astype(BF16)
        return jnp.dot(act, wd_ref[0, 0].astype(BF16), preferred_element_type=F32)

    def load_sequence(seq_idx):
        return pltpu.make_async_copy(x1t_hbm.at[pl.ds(seq_idx * tok_rows, tok_rows)], x1v, sem_in)

    def init_accumulator():
        acc[...] = ALPHA * x1v[...]

    @pl.when((b == 0) & (e == 0) & (j == 0))
    def _first_sequence():
        load_sequence(0).start()
        load_sequence(0).wait()
        init_accumulator()

    @pl.when((e == 0) & (j == 0))
    def _start_sequence():
        yacc[...] = jnp.zeros_like(yacc)

        def body(i, carry):
            xs3[tok_slice(i), :] = x1v[tok_slice(idx_ref[b * N_EXPERTS * cap + i]), :]
            return carry
        lax.fori_loop(0, cap, body, 0, unroll=8)
        xs2d[0] = _load_token_layout(xs3, 0, cap).astype(BF16)

    def gate_column(expert):
        gates = gate_ref[0]
        lane_e = lax.broadcasted_iota(jnp.int32, gates.shape, 1)
        return jnp.sum(jnp.where(lane_e == expert, gates, 0.0), axis=1, keepdims=True)

    @pl.when(j == 0)
    def _chunk0():
        prev = jnp.maximum(e - 1, 0)
        _store_token_layout(y3, yacc[...] * gate_column(prev))
        scatter_rows(prev)
        yacc[...] = expert_chunk(e % 2)

    @pl.when(j == 1)
    def _chunk1():
        gather_rows(jnp.minimum(e + 1, last_e))
        yacc[...] += expert_chunk(e % 2)
        xs2d[(e + 1) % 2] = _load_token_layout(xs3, 0, cap).astype(BF16)

    @pl.when((e == last_e) & (j == 1))
    def _finalize():
        _store_token_layout(y3, yacc[...] * gate_column(last_e))

        def body(gi, carry):
            base = (b * N_EXPERTS + last_e) * cap + gi * SUBLANES
            toks = [idx_ref[base + k] for k in range(SUBLANES)]
            vals = [acc[tok_slice(toks[k]), :] + y3[tok_slice(gi * SUBLANES + k), :]
                    for k in range(SUBLANES)]
            for k in range(SUBLANES):
                acc[tok_slice(toks[k]), :] = vals[k]
            return carry
        lax.fori_loop(0, cap // SUBLANES, body, 0)

        n_seq = pl.num_programs(0)

        @pl.when(b + 1 < n_seq)
        def _():
            load_sequence(b + 1).start()

        n_chunks = seq // LN_CHUNK

        def out_copy(c, slot):
            return pltpu.make_async_copy(
                stage.at[slot], out_hbm.at[pl.ds(b * seq + c * LN_CHUNK, LN_CHUNK)], sem_out.at[slot])

        def ln_body(c, carry):
            slot = c % 2
            first = c * LN_CHUNK
            res = _layer_norm(_load_token_layout(acc, first, LN_CHUNK), lng_ref[...], lnb_ref[...])

            @pl.when(c >= 2)
            def _():
                out_copy(c - 2, slot).wait()
            stage[slot] = res
            out_copy(c, slot).start()
            return carry
        lax.fori_loop(0, n_chunks, ln_body, 0)
        out_copy(n_chunks - 2, (n_chunks - 2) % 2).wait()
        out_copy(n_chunks - 1, (n_chunks - 1) % 2).wait()

        @pl.when(b + 1 < n_seq)
        def _():
            load_sequence(b + 1).wait()
            init_accumulator()


def _moe(x1t, idx, gate_t, w_gate_up, w_down, ln_g, ln_b, layer, batch, seq):
    cap = idx.shape[2]
    T = batch * seq
    fc = EXPERT_FF // FF_CHUNKS
    assert FF_CHUNKS == 2
    idx_flat = idx.reshape(-1)
    grid_spec = pltpu.PrefetchScalarGridSpec(
        num_scalar_prefetch=1,
        grid=(batch, N_EXPERTS, FF_CHUNKS),
        in_specs=[
            pl.BlockSpec(memory_space=pl.ANY),
            pl.BlockSpec((1, cap, N_EXPERTS), lambda b, e, j, idx: (b, 0, 0)),
            pl.BlockSpec((1, 1, D_MODEL, fc), lambda b, e, j, idx: (layer, e, 0, j)),
            pl.BlockSpec((1, 1, D_MODEL, fc), lambda b, e, j, idx: (layer, e, 0, FF_CHUNKS + j)),
            pl.BlockSpec((1, 1, fc, D_MODEL), lambda b, e, j, idx: (layer, e, j, 0)),
            pl.BlockSpec((1, D_MODEL), lambda b, e, j, idx: (0, 0)),
            pl.BlockSpec((1, D_MODEL), lambda b, e, j, idx: (0, 0)),
        ],
        out_specs=pl.BlockSpec(memory_space=pl.ANY),
        scratch_shapes=[
            pltpu.VMEM((seq * TOK_ROWS, LANES), F32),
            pltpu.VMEM((seq * TOK_ROWS, LANES), F32),
            pltpu.VMEM((cap * TOK_ROWS, LANES), F32),
            pltpu.VMEM((cap * TOK_ROWS, LANES), F32),
            pltpu.VMEM((2, cap, D_MODEL), BF16),
            pltpu.VMEM((cap, D_MODEL), F32),
            pltpu.VMEM((2, LN_CHUNK, D_MODEL), F32),
            pltpu.SemaphoreType.DMA(()),
            pltpu.SemaphoreType.DMA((2,)),
        ],
    )
    return pl.pallas_call(
        functools.partial(_moe_kernel, seq=seq, cap=cap),
        grid_spec=grid_spec,
        out_shape=jax.ShapeDtypeStruct((T, D_MODEL), F32),
        compiler_params=pltpu.CompilerParams(
            dimension_semantics=("arbitrary", "arbitrary", "arbitrary"),
            vmem_limit_bytes=VMEM_LIMIT_MOE),
        name="expert_ffn_combine_ln",
    )(idx_flat, x1t, gate_t, w_gate_up, w_gate_up, w_down, ln_g, ln_b)


HALO = SUBLANES


def _gelu_tanh(x):
    return 0.5 * x * (1.0 + jnp.tanh(0.7978845608028654 * (x + 0.044715 * (x * x * x))))


def _lru_in_kernel(x_ref, xp_ref, xn_ref, w_ref, cw_ref, cb_ref, gg_ref, xc_ref, *, nts):
    i = pl.program_id(0)
    tm = x_ref.shape[0]
    x = x_ref[...].astype(BF16)
    xp = xp_ref[...].astype(BF16)
    xn = xn_ref[...].astype(BF16)
    first = (i % nts) == 0
    last = (i % nts) == nts - 1
    for k, gate in enumerate(_dot_by_columns(x, w_ref, 0, D_MODEL)):
        gg_ref[:, k * MXU_COLS:(k + 1) * MXU_COLS] = _gelu_tanh(gate)
    n_ext = tm + 2 * HALO
    for k in range(D_MODEL // MXU_COLS):
        cols = slice(k * MXU_COLS, (k + 1) * MXU_COLS)
        xb, = _dot_by_columns(x, w_ref, D_MODEL + k * MXU_COLS, MXU_COLS)
        xbp, = _dot_by_columns(xp, w_ref, D_MODEL + k * MXU_COLS, MXU_COLS)
        xbn, = _dot_by_columns(xn, w_ref, D_MODEL + k * MXU_COLS, MXU_COLS)
        ext = jnp.concatenate([jnp.where(first, 0.0, xbp), xb, jnp.where(last, 0.0, xbn)], axis=0)
        xc = cb_ref[:, cols]
        for tap in range(CONV_WIDTH):
            shift = (2 - tap) % n_ext
            src = ext if shift == 0 else pltpu.roll(ext, shift, axis=0)
            xc = xc + cw_ref[tap:tap + 1, cols] * src[HALO:HALO + tm, :]
        xc_ref[:, cols] = xc


def _lru_in(x2d, w_in, conv_w, conv_b, seq):
    T = x2d.shape[0]
    tm = min(512, seq)
    nts = seq // tm
    hb = tm // HALO
    nblk = T // HALO
    return pl.pallas_call(
        functools.partial(_lru_in_kernel, nts=nts),
        grid=(T // tm,),
        in_specs=[
            pl.BlockSpec((tm, D_MODEL), lambda i: (i, 0)),
            pl.BlockSpec((HALO, D_MODEL), lambda i: (jnp.maximum(i * hb - 1, 0), 0)),
            pl.BlockSpec((HALO, D_MODEL), lambda i: (jnp.minimum((i + 1) * hb, nblk - 1), 0)),
            pl.BlockSpec((D_MODEL, 2 * D_MODEL), lambda i: (0, 0)),
            pl.BlockSpec((CONV_WIDTH, D_MODEL), lambda i: (0, 0)),
            pl.BlockSpec((1, D_MODEL), lambda i: (0, 0)),
        ],
        out_specs=[
            pl.BlockSpec((tm, D_MODEL), lambda i: (i, 0)),
            pl.BlockSpec((tm, D_MODEL), lambda i: (i, 0)),
        ],
        out_shape=[
            jax.ShapeDtypeStruct((T, D_MODEL), F32),
            jax.ShapeDtypeStruct((T, D_MODEL), F32),
        ],
        compiler_params=pltpu.CompilerParams(dimension_semantics=("arbitrary",)),
        name="lru_in_conv",
    )(x2d, x2d, x2d, w_in, conv_w, conv_b)


SCAN_TM = 128
CHAIN_PITCH = SCAN_TM + SUBLANES
PAIRS = D_MODEL // (2 * LANES)


def _lru_chain_scan(xc_ref, wg_ref, br_ref, bi_ref, lam_ref, carry, abuf, ubuf, hbuf, *, reverse, seq):
    i = pl.program_id(0)
    nseq, tm = xc_ref.shape[0], xc_ref.shape[1]
    assert 2 * nseq == SUBLANES and tm == SCAN_TM
    nt = seq // tm
    tile = (nt - 1 - i) if reverse else i

    @pl.when(i == 0)
    def _():
        carry[...] = jnp.zeros_like(carry)

    half_nsp = (-0.5 * LRU_C) * jax.nn.softplus(-lam_ref[...])
    row = lax.broadcasted_iota(jnp.int32, (tm, 1), 0)
    is_start = (tile * tm + row) == (seq - 1 if reverse else 0)
    for b in range(nseq):
        xc = xc_ref[b]
        xcb = xc.astype(BF16)
        r_parts, i_parts = [], []
        for n in range(LRU_BLOCKS):
            res = jnp.dot(xcb[:, n * LRU_BLOCK_W:(n + 1) * LRU_BLOCK_W], wg_ref[n],
                          preferred_element_type=F32)
            r_parts.append(res[:, :LRU_BLOCK_W])
            i_parts.append(res[:, LRU_BLOCK_W:])
        tr = jnp.tanh(jnp.concatenate(r_parts, axis=1) + br_ref[...])
        ig = 0.5 * jnp.tanh(jnp.concatenate(i_parts, axis=1) + bi_ref[...]) + 0.5
        log_a = tr * half_nsp + half_nsp
        a = jnp.exp(log_a)
        z = -jnp.tanh(log_a) * (1.0 + a * a)
        root = jnp.where(z > 0.0, z * lax.rsqrt(z), 0.0)
        u = jnp.where(is_start, 1.0, root) * ig * xc
        for lb in range(2 * PAIRS):
            rows = slice((2 * b + lb % 2) * CHAIN_PITCH, (2 * b + lb % 2) * CHAIN_PITCH + tm)
            abuf[lb // 2, rows, :] = a[:, lb * LANES:(lb + 1) * LANES]
            ubuf[lb // 2, rows, :] = u[:, lb * LANES:(lb + 1) * LANES]

    def step(k, hs):
        t = (tm - 1 - k) if reverse else k
        chains = pl.ds(t, SUBLANES, stride=CHAIN_PITCH)
        out = []
        for m in range(PAIRS):
            h = abuf[m, chains, :] * hs[m] + ubuf[m, chains, :]
            hbuf[m, chains, :] = h
            out.append(h)
        return tuple(out)

    hs = lax.fori_loop(0, tm, step, tuple(carry[m] for m in range(PAIRS)), unroll=8)
    for m in range(PAIRS):
        carry[m] = hs[m]


def _chain_tile(hbuf, b, tm):
    return jnp.concatenate(
        [hbuf[lb // 2, (2 * b + lb % 2) * CHAIN_PITCH:(2 * b + lb % 2) * CHAIN_PITCH + tm, :]
         for lb in range(2 * PAIRS)], axis=1)


def _lru_fwd_kernel(xc_ref, wg_ref, br_ref, bi_ref, lam_ref, h_ref, carry, abuf, ubuf, hbuf, *, seq):
    _lru_chain_scan(xc_ref, wg_ref, br_ref, bi_ref, lam_ref, carry, abuf, ubuf, hbuf,
                    reverse=False, seq=seq)
    for b in range(h_ref.shape[0]):
        h_ref[b] = _chain_tile(hbuf, b, h_ref.shape[1])


def _lru_bwd_kernel(xc_ref, wg_ref, br_ref, bi_ref, lam_ref, hf_ref, gg_ref, x_ref, wo_ref,
                    g_ref, b_ref, wr_ref, xo_ref, aff_ref, carry, abuf, ubuf, hbuf, *, seq):
    _lru_chain_scan(xc_ref, wg_ref, br_ref, bi_ref, lam_ref, carry, abuf, ubuf, hbuf,
                    reverse=True, seq=seq)
    tm = x_ref.shape[1]
    for b in range(x_ref.shape[0]):
        h = hf_ref[b] + _chain_tile(hbuf, b, tm)
        y = jnp.dot((h * gg_ref[b]).astype(BF16), wo_ref[...], preferred_element_type=F32)
        xn = _layer_norm(ALPHA * x_ref[b] + y, g_ref[...], b_ref[...])
        _store_token_layout(xo_ref.at[b], xn)
        aff_ref[b] = _router_affinity_t(xn, wr_ref[...])


def _lru_specs(batch, tile_map):
    const2 = lambda i: (0, 0)
    return [
        pl.BlockSpec((batch, SCAN_TM, D_MODEL), tile_map),
        pl.BlockSpec((LRU_BLOCKS, LRU_BLOCK_W, 2 * LRU_BLOCK_W), lambda i: (0, 0, 0)),
        pl.BlockSpec((1, D_MODEL), const2),
        pl.BlockSpec((1, D_MODEL), const2),
        pl.BlockSpec((1, D_MODEL), const2),
    ]


def _lru_scratch():
    buf = pltpu.VMEM((PAIRS, SUBLANES * CHAIN_PITCH, LANES), F32)
    return [pltpu.VMEM((PAIRS, SUBLANES, LANES), F32), buf, buf, buf]


def _lru_forward(xc3, wg, br, bi, lam):
    batch, seq, _ = xc3.shape
    tile_map = lambda i: (0, i, 0)
    return pl.pallas_call(
        functools.partial(_lru_fwd_kernel, seq=seq),
        grid=(seq // SCAN_TM,),
        in_specs=_lru_specs(batch, tile_map),
        out_specs=pl.BlockSpec((batch, SCAN_TM, D_MODEL), tile_map),
        out_shape=jax.ShapeDtypeStruct((batch, seq, D_MODEL), F32),
        scratch_shapes=_lru_scratch(),
        compiler_params=pltpu.CompilerParams(dimension_semantics=("arbitrary",)),
        name="lru_forward_scan",
    )(xc3, wg, br, bi, lam)


def _lru_backward_out(xc3, wg, br, bi, lam, hf3, gg3, x3d, w_out, g, b, wr_t):
    batch, seq, _ = xc3.shape
    nt = seq // SCAN_TM
    tile_map = lambda i: (0, nt - 1 - i, 0)
    const2 = lambda i: (0, 0)
    tile = pl.BlockSpec((batch, SCAN_TM, D_MODEL), tile_map)
    return pl.pallas_call(
        functools.partial(_lru_bwd_kernel, seq=seq),
        grid=(nt,),
        in_specs=_lru_specs(batch, tile_map) + [
            tile, tile, tile,
            pl.BlockSpec((D_MODEL, D_MODEL), const2),
            pl.BlockSpec((1, D_MODEL), const2),
            pl.BlockSpec((1, D_MODEL), const2),
            pl.BlockSpec((N_EXPERTS, D_MODEL), const2),
        ],
        out_specs=[
            pl.BlockSpec((batch, SCAN_TM * TOK_ROWS, LANES), tile_map),
            pl.BlockSpec((batch, N_EXPERTS, SCAN_TM), lambda i: (0, 0, nt - 1 - i)),
        ],
        out_shape=[
            jax.ShapeDtypeStruct((batch, seq * TOK_ROWS, LANES), F32),
            jax.ShapeDtypeStruct((batch, N_EXPERTS, seq), F32),
        ],
        scratch_shapes=_lru_scratch(),
        compiler_params=pltpu.CompilerParams(dimension_semantics=("arbitrary",)),
        name="lru_backward_scan_out_ln_router",
    )(xc3, wg, br, bi, lam, hf3, gg3, x3d, w_out, g, b, wr_t)


def _rotary_tables(seq):
    pos = jnp.arange(seq, dtype=F32)
    inv_freq = ROPE_THETA ** (-jnp.arange(0, ROT_DIM, 2, dtype=F32) / ROT_DIM)
    ang = pos[:, None] * inv_freq[None, :]
    cos, sin = jnp.cos(ang), jnp.sin(ang)
    pad = HEAD_DIM - ROT_DIM
    cos_h = jnp.concatenate([cos, cos, jnp.ones((seq, pad), F32)], axis=1)
    sin_h = jnp.concatenate([-sin, sin, jnp.zeros((seq, pad), F32)], axis=1)
    reps = LANES // HEAD_DIM
    return jnp.tile(cos_h, (1, reps)), jnp.tile(sin_h, (1, reps))


def _moe_block(x1t, aff_t, w_gate_up, w_down, ln_g, ln_b, layer, batch, seq):
    cap = CAPACITY_FACTOR * seq // N_EXPERTS
    idx, gate_t = _route(aff_t, cap)
    return _moe(x1t, idx, gate_t, w_gate_up, w_down, ln_g[layer].reshape(1, -1), ln_b[layer].reshape(1, -1),
                layer, batch, seq)


def kernel(x, attn_w_qkv, attn_w_o, attn_sink, lru_w_in, lru_conv_w, lru_conv_b, lru_w_rgate,
           lru_b_rgate, lru_w_igate, lru_b_igate, lru_lambda, lru_w_out, moe_w_router, moe_w_gate_up,
           moe_w_down, ln_mix_g, ln_mix_b, ln_ffn_g, ln_ffn_b):
    batch, seq, _ = x.shape
    x2d = x.reshape(batch * seq, D_MODEL)
    row = lambda v: v.reshape(1, -1)

    cos_t, sin_t = _rotary_tables(seq)
    q, kv = _qkv_proj(x2d, attn_w_qkv[0].astype(BF16), cos_t, sin_t, seq)
    o = _attention(q, kv, attn_sink[0], batch, seq)
    x1, aff_t = _proj_ln(o, attn_w_o[0].astype(BF16), x2d, row(ln_mix_g[0]), row(ln_mix_b[0]),
                         moe_w_router[0].T.astype(BF16), batch, seq)
    x2 = _moe_block(x1, aff_t, moe_w_gate_up, moe_w_down, ln_ffn_g, ln_ffn_b, 0, batch, seq)

    gg, xc = _lru_in(x2, lru_w_in[0].astype(BF16), lru_conv_w[0], row(lru_conv_b[0]), seq)
    wg = (0.5 * jnp.concatenate([lru_w_rgate[0], lru_w_igate[0]], axis=-1)).astype(BF16)
    half_row = lambda v: 0.5 * v.reshape(1, -1)
    per_seq = lambda t: t.reshape(batch, seq, D_MODEL)
    xc3 = per_seq(xc)
    hf3 = _lru_forward(xc3, wg[0], half_row(lru_b_rgate[0, 0]), half_row(lru_b_igate[0, 0]),
                       row(lru_lambda[0, 0]))
    x3, aff_t = _lru_backward_out(xc3, wg[1], half_row(lru_b_rgate[0, 1]), half_row(lru_b_igate[0, 1]),
                                  row(lru_lambda[0, 1]), hf3, per_seq(gg), per_seq(x2),
                                  lru_w_out[0].astype(BF16), row(ln_mix_g[1]), row(ln_mix_b[1]),
                                  moe_w_router[1].T.astype(BF16))
    x3 = x3.reshape(batch * seq * TOK_ROWS, LANES)
    x4 = _moe_block(x3, aff_t, moe_w_gate_up, moe_w_down, ln_ffn_g, ln_ffn_b, 1, batch, seq)
    return x4.reshape(batch, seq, D_MODEL)
```

```python
import functools

import jax
import jax.numpy as jnp
from jax import lax
from jax.experimental import pallas as pl
from jax.experimental.pallas import tpu as pltpu

F32 = jnp.float32
BF16 = jnp.bfloat16

D_MODEL = 1024
HEAD_DIM = 64
N_Q_HEADS = 16
N_KV_HEADS = 4
GQA_GROUP = 4
WINDOW = 128
BLOCK = 128
ROT_DIM = 16
ROPE_THETA = 500000.0
LRU_BLOCKS = 4
LRU_BLOCK_W = 256
CONV_WIDTH = 4
LRU_C = 8.0
N_EXPERTS = 16
EXPERT_FF = 1024
CAPACITY_FACTOR = 2
LN_EPS = 1e-5
DEPTH = 2
ALPHA = (2 * DEPTH) ** 0.25

LANES = 128
SUBLANES = 8
MXU_COLS = 256
TOK_ROWS = D_MODEL // LANES
VMEM_LIMIT_MOE = 61 * 1024 * 1024


def _layer_norm(v, g, b):
    mu = jnp.mean(v, axis=-1, keepdims=True)
    vc = v - mu
    var = jnp.mean(vc * vc, axis=-1, keepdims=True)
    return vc * lax.rsqrt(var + LN_EPS) * g + b


def _store_token_layout(dst_ref, val):
    n = val.shape[0]
    for s in range(TOK_ROWS):
        dst_ref[pl.ds(s, n, stride=TOK_ROWS), :] = val[:, s * LANES:(s + 1) * LANES]


def _load_token_layout(src_ref, first_token, n):
    base = first_token * TOK_ROWS
    return jnp.concatenate(
        [src_ref[pl.ds(base + s, n, stride=TOK_ROWS), :] for s in range(TOK_ROWS)], axis=1)


def _router_affinity_t(xn, wr_t):
    logits_t = lax.dot_general(wr_t, xn.astype(BF16), (((1,), (1,)), ((), ())),
                               preferred_element_type=F32)
    m = jnp.max(logits_t, axis=0, keepdims=True)
    p = jnp.exp(logits_t - m)
    return p / jnp.sum(p, axis=0, keepdims=True)


def _dot_by_columns(lhs, w_ref, first_col, n_cols):
    return [jnp.dot(lhs, w_ref[:, c:c + MXU_COLS], preferred_element_type=F32)
            for c in range(first_col, first_col + n_cols, MXU_COLS)]


LOG2E = 1.4426950408889634


def _qkv_kernel(x_ref, w_ref, cos_ref, sin_ref, q_ref, kv_ref):
    x = x_ref[...].astype(BF16)
    acc = jnp.dot(x, w_ref[...], preferred_element_type=F32)
    cosb = cos_ref[...]
    sinb = sin_ref[...]
    lane = lax.broadcasted_iota(jnp.int32, cosb.shape, 1)
    low = (lane % HEAD_DIM) < (ROT_DIM // 2)
    nq = N_Q_HEADS * HEAD_DIM
    nk = N_KV_HEADS * HEAD_DIM

    def rot(t):
        up = pltpu.roll(t, LANES - ROT_DIM // 2, axis=1)
        dn = pltpu.roll(t, ROT_DIM // 2, axis=1)
        return t * cosb + jnp.where(low, up, dn) * sinb

    qs = [rot(acc[:, c * LANES:(c + 1) * LANES]) * (LOG2E * HEAD_DIM ** -0.5) for c in range(nq // LANES)]
    q_ref[...] = jnp.concatenate(qs, axis=1).astype(BF16)
    ks = [rot(acc[:, nq + c * LANES:nq + (c + 1) * LANES]) for c in range(nk // LANES)]
    vs = [acc[:, nq + nk + c * LANES:nq + nk + (c + 1) * LANES] for c in range(nk // LANES)]
    swap = lambda t: pltpu.roll(t, HEAD_DIM, axis=1)
    kv_ref[...] = jnp.concatenate(
        ks + [swap(t) for t in ks] + vs + [swap(t) for t in vs], axis=1).astype(BF16)


def _qkv_proj(x2d, w_qkv, cos_t, sin_t, seq):
    T = x2d.shape[0]
    tm = min(512, seq)
    nq = N_Q_HEADS * HEAD_DIM
    nk = N_KV_HEADS * HEAD_DIM
    nts = seq // tm
    return pl.pallas_call(
        _qkv_kernel,
        grid=(T // tm,),
        in_specs=[
            pl.BlockSpec((tm, D_MODEL), lambda i: (i, 0)),
            pl.BlockSpec((D_MODEL, nq + 2 * nk), lambda i: (0, 0)),
            pl.BlockSpec((tm, LANES), lambda i: (i % nts, 0)),
            pl.BlockSpec((tm, LANES), lambda i: (i % nts, 0)),
        ],
        out_specs=[
            pl.BlockSpec((tm, nq), lambda i: (i, 0)),
            pl.BlockSpec((tm, 4 * nk), lambda i: (i, 0)),
        ],
        out_shape=[
            jax.ShapeDtypeStruct((T, nq), BF16),
            jax.ShapeDtypeStruct((T, 4 * nk), BF16),
        ],
        compiler_params=pltpu.CompilerParams(dimension_semantics=("arbitrary",)),
        name="qkv_proj",
    )(x2d, w_qkv, cos_t, sin_t)


QBLKS = 8


def _attn_kernel(sink_ref, q_ref, kvp_ref, kvc_ref, kvn_ref, o_ref, *, seq):
    i = pl.program_id(1)
    nk = N_KV_HEADS * HEAD_DIM
    kv = jnp.concatenate([kvp_ref[...], kvc_ref[...], kvn_ref[...]], axis=0)
    band = 3 * BLOCK
    lo = lax.broadcasted_iota(jnp.int32, (1, LANES), 1) < HEAD_DIM
    zero = jnp.zeros((), BF16)
    k_lo, k_hi, v_dup_t = [], [], []
    for h in range(N_KV_HEADS):
        grp, half = h // 2, h % 2

        def block(base, swapped):
            col = base + (nk if swapped else 0) + grp * LANES
            return kv[:, col:col + LANES]
        k_lo.append(jnp.where(lo, block(0, half == 1), zero))
        k_hi.append(jnp.where(lo, zero, block(0, half == 0)))
        v_dup = jnp.where(lo, block(2 * nk, half == 1), block(2 * nk, half == 0))
        v_dup_t.append(v_dup.astype(F32).T.astype(BF16))

    key = lax.broadcasted_iota(jnp.int32, (BLOCK, BLOCK), 0)
    qry = lax.broadcasted_iota(jnp.int32, (BLOCK, BLOCK), 1)
    feat_lo = lax.broadcasted_iota(jnp.int32, (LANES, 1), 0) < HEAD_DIM
    units = [(j, h) for j in range(QBLKS) for h in range(N_KV_HEADS)]

    def scores_t(j, h):
        rows = slice(j * BLOCK, (j + 1) * BLOCK)
        qp = jnp.concatenate([q_ref[rows, (2 * h) * LANES:(2 * h + 1) * LANES],
                              q_ref[rows, (2 * h + 1) * LANES:(2 * h + 2) * LANES]], axis=0)
        keys = jnp.concatenate([k_lo[h][j * BLOCK:j * BLOCK + band], k_hi[h][j * BLOCK:j * BLOCK + band]],
                               axis=0)
        return lax.dot_general(keys, qp, (((1,), (1,)), ((), ())), preferred_element_type=F32)

    s_next = scores_t(*units[0])
    for u, (j, h) in enumerate(units):
        s = s_next
        if u + 1 < len(units):
            s_next = scores_t(*units[u + 1])
        n = i * QBLKS + j
        ok_prev = jnp.concatenate([(key >= qry) & (n > 0)] * 2, axis=1)
        ok_next = jnp.concatenate([(key <= qry) & (n < seq // BLOCK - 1)] * 2, axis=1)
        vt = v_dup_t[h][:, j * BLOCK:j * BLOCK + band]
        halves = []
        for par in range(2):
            sp = s[par * band:(par + 1) * band, :]
            sp = jnp.concatenate([jnp.where(ok_prev, sp[:BLOCK], -jnp.inf), sp[BLOCK:2 * BLOCK],
                                  jnp.where(ok_next, sp[2 * BLOCK:], -jnp.inf)], axis=0)
            sink_row = jnp.concatenate(
                [jnp.full((1, BLOCK), sink_ref[h * GQA_GROUP + g] * LOG2E, F32) for g in (par, par + 2)],
                axis=1)
            m = jnp.maximum(jnp.max(sp, axis=0, keepdims=True), sink_row)
            p = jnp.exp2(sp - m)
            denom = jnp.sum(p, axis=0, keepdims=True) + jnp.exp2(sink_row - m)
            halves.append(jnp.dot(vt, p.astype(BF16), preferred_element_type=F32) / denom)
        out_t = jnp.where(feat_lo, halves[0], halves[1])
        o_ref[j * BLOCK:(j + 1) * BLOCK, (2 * h) * LANES:(2 * h + 2) * LANES] = jnp.concatenate(
            [out_t[:, :BLOCK].T, out_t[:, BLOCK:].T], axis=1).astype(BF16)


def _attention(q, kv, sink, batch, seq):
    T = q.shape[0]
    qt = QBLKS * BLOCK
    nq = seq // qt
    nb = seq // BLOCK
    wkv = kv.shape[1]
    cur = lambda b, i: (b * nq + i, 0)
    prev = lambda b, i: (b * nb + jnp.maximum(i * QBLKS - 1, 0), 0)
    nxt = lambda b, i: (b * nb + jnp.minimum(i * QBLKS + QBLKS, nb - 1), 0)
    return pl.pallas_call(
        functools.partial(_attn_kernel, seq=seq),
        grid=(batch, nq),
        in_specs=[
            pl.BlockSpec(memory_space=pltpu.SMEM),
            pl.BlockSpec((qt, D_MODEL), cur),
            pl.BlockSpec((BLOCK, wkv), prev),
            pl.BlockSpec((qt, wkv), cur),
            pl.BlockSpec((BLOCK, wkv), nxt),
        ],
        out_specs=pl.BlockSpec((qt, D_MODEL), cur),
        out_shape=jax.ShapeDtypeStruct((T, D_MODEL), BF16),
        compiler_params=pltpu.CompilerParams(dimension_semantics=("arbitrary", "arbitrary")),
        name="swa_attention",
    )(sink, q, kv, kv, kv)


def _proj_ln_kernel(a_ref, w_ref, x_ref, g_ref, b_ref, wr_ref, xo_ref, aff_ref):
    y = jnp.dot(a_ref[...], w_ref[...], preferred_element_type=F32)
    xn = _layer_norm(ALPHA * x_ref[...] + y, g_ref[...], b_ref[...])
    _store_token_layout(xo_ref, xn)
    aff_ref[0] = _router_affinity_t(xn, wr_ref[...])


def _proj_ln(a, w, x2d, g, b, wr_t, batch, seq):
    T = x2d.shape[0]
    tm = min(512, seq)
    nts = seq // tm
    return pl.pallas_call(
        _proj_ln_kernel,
        grid=(T // tm,),
        in_specs=[
            pl.BlockSpec((tm, D_MODEL), lambda i: (i, 0)),
            pl.BlockSpec((D_MODEL, D_MODEL), lambda i: (0, 0)),
            pl.BlockSpec((tm, D_MODEL), lambda i: (i, 0)),
            pl.BlockSpec((1, D_MODEL), lambda i: (0, 0)),
            pl.BlockSpec((1, D_MODEL), lambda i: (0, 0)),
            pl.BlockSpec((N_EXPERTS, D_MODEL), lambda i: (0, 0)),
        ],
        out_specs=[
            pl.BlockSpec((tm * TOK_ROWS, LANES), lambda i: (i, 0)),
            pl.BlockSpec((1, N_EXPERTS, tm), lambda i: (i // nts, 0, i % nts)),
        ],
        out_shape=[
            jax.ShapeDtypeStruct((T * TOK_ROWS, LANES), F32),
            jax.ShapeDtypeStruct((batch, N_EXPERTS, seq), F32),
        ],
        compiler_params=pltpu.CompilerParams(dimension_semantics=("arbitrary",)),
        name="proj_ln_router",
    )(a, w, x2d, g, b, wr_t)


SLOT_BLOCK = 64


def _lane_prefix(mask, tri, dst_ref):
    seq = mask.shape[1]
    off = jnp.zeros((mask.shape[0], 1), F32)
    mb = mask.astype(BF16)
    ends = []
    for c in range(seq // LANES):
        res = jnp.dot(mb[:, c * LANES:(c + 1) * LANES], tri, preferred_element_type=F32) + off
        dst_ref[:, c * LANES:(c + 1) * LANES] = res
        off = res[:, LANES - 1:LANES]
        ends.append(off)
    return ends


def _select_kernel(aff_ref, pos_ref, bnd_ref, *, cap):
    aff = aff_ref[0]
    as_float = lambda bits: pltpu.bitcast(bits, F32)
    rows = aff.shape[0]
    cur = jnp.zeros((rows, 1), jnp.int32)
    for bit in range(30, -1, -1):
        cand = cur | (1 << bit)
        cnt = jnp.sum((aff >= as_float(cand)).astype(jnp.int32), axis=1, keepdims=True)
        cur = jnp.where(cnt >= cap, cand, cur)
    gt = aff >= as_float(cur + 1)
    eq = (aff >= as_float(cur)) & jnp.logical_not(gt)
    n_gt = jnp.sum(gt.astype(jnp.int32), axis=1, keepdims=True)
    ties_taken = (cap - n_gt).astype(F32)
    ri = lax.broadcasted_iota(jnp.int32, (LANES, LANES), 0)
    ci = lax.broadcasted_iota(jnp.int32, (LANES, LANES), 1)
    tri = (ri <= ci).astype(BF16)
    _lane_prefix(eq, tri, pos_ref.at[0])
    sel = gt | (eq & (pos_ref[0] <= ties_taken))
    ends = _lane_prefix(sel, tri, pos_ref.at[0])
    pos_ref[0] = jnp.where(sel, pos_ref[0], 0.0)
    cols = []
    for sb in range(cap // SLOT_BLOCK):
        for target in (sb * SLOT_BLOCK + 1, (sb + 1) * SLOT_BLOCK):
            blk = jnp.zeros((rows, 1), jnp.int32)
            for end in ends[:-1]:
                blk = blk + (end < target).astype(jnp.int32)
            cols.append(blk)
    bnd_ref[0] = jnp.concatenate(cols, axis=1)


def _compact_kernel(bnd_ref, pos_ref, aff_ref, idx_ref, gate_ref, ptok, pgate, *, cap):
    b = pl.program_id(0)
    n_sb = cap // SLOT_BLOCK
    groups = SLOT_BLOCK // SUBLANES
    lane = lax.broadcasted_iota(jnp.int32, (SUBLANES, LANES), 1).astype(F32)
    sub1 = (lax.broadcasted_iota(jnp.int32, (SUBLANES, 1), 0) + 1).astype(F32)
    lane_e = lax.broadcasted_iota(jnp.int32, (cap, N_EXPERTS), 1)
    idx_ref[0] = jnp.zeros((cap, N_EXPERTS), jnp.int32)
    gate_ref[0] = jnp.zeros((cap, N_EXPERTS), F32)

    def expert(e, carry):
        def slot_block(sb, carry2):
            base = ((b * N_EXPERTS + e) * n_sb + sb) * 2
            slot0 = jnp.asarray(sb * SLOT_BLOCK, F32)

            def lane_block(c, accs):
                off = pl.multiple_of(c * LANES, LANES)
                p = jnp.broadcast_to(pos_ref[0, e:e + 1, pl.ds(off, LANES)], (SUBLANES, LANES))
                a = jnp.broadcast_to(aff_ref[0, e:e + 1, pl.ds(off, LANES)], (SUBLANES, LANES))
                tokv = lane + jnp.asarray(c * LANES, F32)
                out = []
                for g in range(groups):
                    hit = p == (slot0 + (g * SUBLANES) + sub1)
                    out.append(jnp.where(hit, tokv, accs[2 * g]))
                    out.append(jnp.where(hit, a, accs[2 * g + 1]))
                return tuple(out)

            zeros = tuple(jnp.zeros((SUBLANES, LANES), F32) for _ in range(2 * groups))
            accs = lax.fori_loop(bnd_ref[base], bnd_ref[base + 1] + 1, lane_block, zeros)
            for g in range(groups):
                rows = pl.ds(pl.multiple_of(sb * SLOT_BLOCK + g * SUBLANES, SUBLANES), SUBLANES)
                ptok[rows, :] = accs[2 * g]
                pgate[rows, :] = accs[2 * g + 1]
            return carry2

        lax.fori_loop(0, n_sb, slot_block, 0)
        tok_col = jnp.sum(ptok[...], axis=1, keepdims=True).astype(jnp.int32)
        gate_col = jnp.sum(pgate[...], axis=1, keepdims=True)
        idx_ref[0] = jnp.where(lane_e == e, tok_col, idx_ref[0])
        gate_ref[0] = jnp.where(lane_e == e, gate_col, gate_ref[0])
        return carry

    for e in range(N_EXPERTS):
        expert(e, 0)


def _route(aff_t, cap):
    batch, _, seq = aff_t.shape
    n_bnd = 2 * (cap // SLOT_BLOCK)
    seq_spec = pl.BlockSpec((1, N_EXPERTS, seq), lambda b, *_: (b, 0, 0))
    rows = batch * N_EXPERTS
    pos, bnd = pl.pallas_call(
        functools.partial(_select_kernel, cap=cap),
        grid=(1,),
        in_specs=[pl.BlockSpec((1, rows, seq), lambda i: (0, 0, 0))],
        out_specs=[pl.BlockSpec((1, rows, seq), lambda i: (0, 0, 0)),
                   pl.BlockSpec((1, rows, n_bnd), lambda i: (0, 0, 0))],
        out_shape=[
            jax.ShapeDtypeStruct((1, rows, seq), F32),
            jax.ShapeDtypeStruct((1, rows, n_bnd), jnp.int32),
        ],
        compiler_params=pltpu.CompilerParams(dimension_semantics=("arbitrary",)),
        name="expert_choice_select",
    )(aff_t.reshape(1, rows, seq))
    pos = pos.reshape(batch, N_EXPERTS, seq)
    slot_spec = pl.BlockSpec((1, cap, N_EXPERTS), lambda b, *_: (b, 0, 0))
    idx_t, gate_t = pl.pallas_call(
        functools.partial(_compact_kernel, cap=cap),
        grid_spec=pltpu.PrefetchScalarGridSpec(
            num_scalar_prefetch=1,
            grid=(batch,),
            in_specs=[seq_spec, seq_spec],
            out_specs=[slot_spec, slot_spec],
            scratch_shapes=[pltpu.VMEM((cap, LANES), F32), pltpu.VMEM((cap, LANES), F32)],
        ),
        out_shape=[
            jax.ShapeDtypeStruct((batch, cap, N_EXPERTS), jnp.int32),
            jax.ShapeDtypeStruct((batch, cap, N_EXPERTS), F32),
        ],
        compiler_params=pltpu.CompilerParams(dimension_semantics=("arbitrary",)),
        name="expert_choice_compact",
    )(bnd.reshape(-1), pos, aff_t)
    return jnp.swapaxes(idx_t, 1, 2), gate_t


FF_CHUNKS = 2
LN_CHUNK = 256
SCATTER_GROUP = 2


def _moe_kernel(idx_ref, x1t_hbm, gate_ref, wg_ref, wu_ref, wd_ref, lng_ref, lnb_ref, out_hbm,
                x1v, acc, xs3, y3, xs2d, yacc, stage, sem_in, sem_out, *, seq, cap):
    b = pl.program_id(0)
    e = pl.program_id(1)
    j = pl.program_id(2)
    tok_rows = seq * TOK_ROWS
    last_e = N_EXPERTS - 1

    def tok_slice(t):
        return pl.ds(pl.multiple_of(t * TOK_ROWS, TOK_ROWS), TOK_ROWS)

    def gather_rows(expert):
        base = (b * N_EXPERTS + expert) * cap
        for i in range(cap):
            xs3[i * TOK_ROWS:(i + 1) * TOK_ROWS, :] = x1v[tok_slice(idx_ref[base + i]), :]

    def scatter_rows(expert):
        base = (b * N_EXPERTS + expert) * cap
        for g0 in range(0, cap, SCATTER_GROUP):
            toks = [idx_ref[base + g0 + k] for k in range(SCATTER_GROUP)]
            vals = [acc[tok_slice(toks[k]), :] + y3[(g0 + k) * TOK_ROWS:(g0 + k + 1) * TOK_ROWS, :]
                    for k in range(SCATTER_GROUP)]
            for k in range(SCATTER_GROUP):
                acc[tok_slice(toks[k]), :] = vals[k]

    def expert_chunk(slot):
        xs = xs2d[slot]
        g = jnp.dot(xs, wg_ref[0, 0].astype(BF16), preferred_element_type=F32)
        u = jnp.dot(xs, wu_ref[0, 0].astype(BF16), preferred_element_type=F32)
        act = (g * jax.nn.sigmoid(g) * u).astype(BF16)
        return jnp.dot(act, wd_ref[0, 0].astype(BF16), preferred_element_type=F32)

    def load_sequence(seq_idx):
        return pltpu.make_async_copy(x1t_hbm.at[pl.ds(seq_idx * tok_rows, tok_rows)], x1v, sem_in)

    def init_accumulator():
        acc[...] = ALPHA * x1v[...]

    @pl.when((b == 0) & (e == 0) & (j == 0))
    def _first_sequence():
        load_sequence(0).start()
        load_sequence(0).wait()
        init_accumulator()

    @pl.when((e == 0) & (j == 0))
    def _start_sequence():
        yacc[...] = jnp.zeros_like(yacc)

        def body(i, carry):
            xs3[tok_slice(i), :] = x1v[tok_slice(idx_ref[b * N_EXPERTS * cap + i]), :]
            return carry
        lax.fori_loop(0, cap, body, 0, unroll=8)
        xs2d[0] = _load_token_layout(xs3, 0, cap).astype(BF16)

    def gate_column(expert):
        gates = gate_ref[0]
        lane_e = lax.broadcasted_iota(jnp.int32, gates.shape, 1)
        return jnp.sum(jnp.where(lane_e == expert, gates, 0.0), axis=1, keepdims=True)

    @pl.when(j == 0)
    def _chunk0():
        prev = jnp.maximum(e - 1, 0)
        _store_token_layout(y3, yacc[...] * gate_column(prev))
        scatter_rows(prev)
        yacc[...] = expert_chunk(e % 2)

    @pl.when(j == 1)
    def _chunk1():
        gather_rows(jnp.minimum(e + 1, last_e))
        yacc[...] += expert_chunk(e % 2)
        xs2d[(e + 1) % 2] = _load_token_layout(xs3, 0, cap).astype(BF16)

    @pl.when((e == last_e) & (j == 1))
    def _finalize():
        _store_token_layout(y3, yacc[...] * gate_column(last_e))

        def body(gi, carry):
            base = (b * N_EXPERTS + last_e) * cap + gi * SUBLANES
            toks = [idx_ref[base + k] for k in range(SUBLANES)]
            vals = [acc[tok_slice(toks[k]), :] + y3[tok_slice(gi * SUBLANES + k), :]
                    for k in range(SUBLANES)]
            for k in range(SUBLANES):
                acc[tok_slice(toks[k]), :] = vals[k]
            return carry
        lax.fori_loop(0, cap // SUBLANES, body, 0)

        n_seq = pl.num_programs(0)

        @pl.when(b + 1 < n_seq)
        def _():
            load_sequence(b + 1).start()

        n_chunks = seq // LN_CHUNK

        def out_copy(c, slot):
            return pltpu.make_async_copy(
                stage.at[slot], out_hbm.at[pl.ds(b * seq + c * LN_CHUNK, LN_CHUNK)], sem_out.at[slot])

        def ln_body(c, carry):
            slot = c % 2
            first = c * LN_CHUNK
            res = _layer_norm(_load_token_layout(acc, first, LN_CHUNK), lng_ref[...], lnb_ref[...])

            @pl.when(c >= 2)
            def _():
                out_copy(c - 2, slot).wait()
            stage[slot] = res
            out_copy(c, slot).start()
            return carry
        lax.fori_loop(0, n_chunks, ln_body, 0)
        out_copy(n_chunks - 2, (n_chunks - 2) % 2).wait()
        out_copy(n_chunks - 1, (n_chunks - 1) % 2).wait()

        @pl.when(b + 1 < n_seq)
        def _():
            load_sequence(b + 1).wait()
            init_accumulator()


def _moe(x1t, idx, gate_t, w_gate_up, w_down, ln_g, ln_b, layer, batch, seq):
    cap = idx.shape[2]
    T = batch * seq
    fc = EXPERT_FF // FF_CHUNKS
    assert FF_CHUNKS == 2
    idx_flat = idx.reshape(-1)
    grid_spec = pltpu.PrefetchScalarGridSpec(
        num_scalar_prefetch=1,
        grid=(batch, N_EXPERTS, FF_CHUNKS),
        in_specs=[
            pl.BlockSpec(memory_space=pl.ANY),
            pl.BlockSpec((1, cap, N_EXPERTS), lambda b, e, j, idx: (b, 0, 0)),
            pl.BlockSpec((1, 1, D_MODEL, fc), lambda b, e, j, idx: (layer, e, 0, j)),
            pl.BlockSpec((1, 1, D_MODEL, fc), lambda b, e, j, idx: (layer, e, 0, FF_CHUNKS + j)),
            pl.BlockSpec((1, 1, fc, D_MODEL), lambda b, e, j, idx: (layer, e, j, 0)),
            pl.BlockSpec((1, D_MODEL), lambda b, e, j, idx: (0, 0)),
            pl.BlockSpec((1, D_MODEL), lambda b, e, j, idx: (0, 0)),
        ],
        out_specs=pl.BlockSpec(memory_space=pl.ANY),
        scratch_shapes=[
            pltpu.VMEM((seq * TOK_ROWS, LANES), F32),
            pltpu.VMEM((seq * TOK_ROWS, LANES), F32),
            pltpu.VMEM((cap * TOK_ROWS, LANES), F32),
            pltpu.VMEM((cap * TOK_ROWS, LANES), F32),
            pltpu.VMEM((2, cap, D_MODEL), BF16),
            pltpu.VMEM((cap, D_MODEL), F32),
            pltpu.VMEM((2, LN_CHUNK, D_MODEL), F32),
            pltpu.SemaphoreType.DMA(()),
            pltpu.SemaphoreType.DMA((2,)),
        ],
    )
    return pl.pallas_call(
        functools.partial(_moe_kernel, seq=seq, cap=cap),
        grid_spec=grid_spec,
        out_shape=jax.ShapeDtypeStruct((T, D_MODEL), F32),
        compiler_params=pltpu.CompilerParams(
            dimension_semantics=("arbitrary", "arbitrary", "arbitrary"),
            vmem_limit_bytes=VMEM_LIMIT_MOE),
        name="expert_ffn_combine_ln",
    )(idx_flat, x1t, gate_t, w_gate_up, w_gate_up, w_down, ln_g, ln_b)


HALO = SUBLANES


def _gelu_tanh(x):
    return 0.5 * x * (1.0 + jnp.tanh(0.7978845608028654 * (x + 0.044715 * (x * x * x))))


def _lru_in_kernel(x_ref, xp_ref, xn_ref, w_ref, cw_ref, cb_ref, gg_ref, xc_ref, *, nts):
    i = pl.program_id(0)
    tm = x_ref.shape[0]
    x = x_ref[...].astype(BF16)
    xp = xp_ref[...].astype(BF16)
    xn = xn_ref[...].astype(BF16)
    first = (i % nts) == 0
    last = (i % nts) == nts - 1
    for k, gate in enumerate(_dot_by_columns(x, w_ref, 0, D_MODEL)):
        gg_ref[:, k * MXU_COLS:(k + 1) * MXU_COLS] = _gelu_tanh(gate)
    n_ext = tm + 2 * HALO
    for k in range(D_MODEL // MXU_COLS):
        cols = slice(k * MXU_COLS, (k + 1) * MXU_COLS)
        xb, = _dot_by_columns(x, w_ref, D_MODEL + k * MXU_COLS, MXU_COLS)
        xbp, = _dot_by_columns(xp, w_ref, D_MODEL + k * MXU_COLS, MXU_COLS)
        xbn, = _dot_by_columns(xn, w_ref, D_MODEL + k * MXU_COLS, MXU_COLS)
        ext = jnp.concatenate([jnp.where(first, 0.0, xbp), xb, jnp.where(last, 0.0, xbn)], axis=0)
        xc = cb_ref[:, cols]
        for tap in range(CONV_WIDTH):
            shift = (2 - tap) % n_ext
            src = ext if shift == 0 else pltpu.roll(ext, shift, axis=0)
            xc = xc + cw_ref[tap:tap + 1, cols] * src[HALO:HALO + tm, :]
        xc_ref[:, cols] = xc


def _lru_in(x2d, w_in, conv_w, conv_b, seq):
    T = x2d.shape[0]
    tm = min(512, seq)
    nts = seq // tm
    hb = tm // HALO
    nblk = T // HALO
    return pl.pallas_call(
        functools.partial(_lru_in_kernel, nts=nts),
        grid=(T // tm,),
        in_specs=[
            pl.BlockSpec((tm, D_MODEL), lambda i: (i, 0)),
            pl.BlockSpec((HALO, D_MODEL), lambda i: (jnp.maximum(i * hb - 1, 0), 0)),
            pl.BlockSpec((HALO, D_MODEL), lambda i: (jnp.minimum((i + 1) * hb, nblk - 1), 0)),
            pl.BlockSpec((D_MODEL, 2 * D_MODEL), lambda i: (0, 0)),
            pl.BlockSpec((CONV_WIDTH, D_MODEL), lambda i: (0, 0)),
            pl.BlockSpec((1, D_MODEL), lambda i: (0, 0)),
        ],
        out_specs=[
            pl.BlockSpec((tm, D_MODEL), lambda i: (i, 0)),
            pl.BlockSpec((tm, D_MODEL), lambda i: (i, 0)),
        ],
        out_shape=[
            jax.ShapeDtypeStruct((T, D_MODEL), F32),
            jax.ShapeDtypeStruct((T, D_MODEL), F32),
        ],
        compiler_params=pltpu.CompilerParams(dimension_semantics=("arbitrary",)),
        name="lru_in_conv",
    )(x2d, x2d, x2d, w_in, conv_w, conv_b)


SCAN_TM = 128
CHAIN_PITCH = SCAN_TM + SUBLANES
PAIRS = D_MODEL // (2 * LANES)


def _lru_chain_scan(xc_ref, wg_ref, br_ref, bi_ref, lam_ref, carry, abuf, ubuf, hbuf, *, reverse, seq):
    i = pl.program_id(0)
    nseq, tm = xc_ref.shape[0], xc_ref.shape[1]
    assert 2 * nseq == SUBLANES and tm == SCAN_TM
    nt = seq // tm
    tile = (nt - 1 - i) if reverse else i

    @pl.when(i == 0)
    def _():
        carry[...] = jnp.zeros_like(carry)

    half_nsp = (-0.5 * LRU_C) * jax.nn.softplus(-lam_ref[...])
    row = lax.broadcasted_iota(jnp.int32, (tm, 1), 0)
    is_start = (tile * tm + row) == (seq - 1 if reverse else 0)
    xc_all = jnp.concatenate([xc_ref[b] for b in range(nseq)], axis=0)
    xcb = xc_all.astype(BF16)
    r_parts, i_parts = [], []
    for n in range(LRU_BLOCKS):
        res = jnp.dot(xcb[:, n * LRU_BLOCK_W:(n + 1) * LRU_BLOCK_W], wg_ref[n],
                      preferred_element_type=F32)
        r_parts.append(res[:, :LRU_BLOCK_W])
        i_parts.append(res[:, LRU_BLOCK_W:])
    r_pre = jnp.concatenate(r_parts, axis=1)
    i_pre = jnp.concatenate(i_parts, axis=1)
    for b in range(nseq):
        rows_b = slice(b * tm, (b + 1) * tm)
        xc = xc_all[rows_b]
        tr = jnp.tanh(r_pre[rows_b] + br_ref[...])
        ig = 0.5 * jnp.tanh(i_pre[rows_b] + bi_ref[...]) + 0.5
        log_a = tr * half_nsp + half_nsp
        a = jnp.exp(log_a)
        z = -jnp.tanh(log_a) * (1.0 + a * a)
        root = jnp.where(z > 0.0, z * lax.rsqrt(z), 0.0)
        u = jnp.where(is_start, 1.0, root) * ig * xc
        for lb in range(2 * PAIRS):
            rows = slice((2 * b + lb % 2) * CHAIN_PITCH, (2 * b + lb % 2) * CHAIN_PITCH + tm)
            abuf[lb // 2, rows, :] = a[:, lb * LANES:(lb + 1) * LANES]
            ubuf[lb // 2, rows, :] = u[:, lb * LANES:(lb + 1) * LANES]

    def step(k, hs):
        t = (tm - 1 - k) if reverse else k
        chains = pl.ds(t, SUBLANES, stride=CHAIN_PITCH)
        out = []
        for m in range(PAIRS):
            h = abuf[m, chains, :] * hs[m] + ubuf[m, chains, :]
            hbuf[m, chains, :] = h
            out.append(h)
        return tuple(out)

    hs = lax.fori_loop(0, tm, step, tuple(carry[m] for m in range(PAIRS)), unroll=8)
    for m in range(PAIRS):
        carry[m] = hs[m]


def _chain_tile(hbuf, b, tm):
    return jnp.concatenate(
        [hbuf[lb // 2, (2 * b + lb % 2) * CHAIN_PITCH:(2 * b + lb % 2) * CHAIN_PITCH + tm, :]
         for lb in range(2 * PAIRS)], axis=1)


def _lru_fwd_kernel(xc_ref, wg_ref, br_ref, bi_ref, lam_ref, h_ref, carry, abuf, ubuf, hbuf, *, seq):
    _lru_chain_scan(xc_ref, wg_ref, br_ref, bi_ref, lam_ref, carry, abuf, ubuf, hbuf,
                    reverse=False, seq=seq)
    for b in range(h_ref.shape[0]):
        h_ref[b] = _chain_tile(hbuf, b, h_ref.shape[1])


def _lru_bwd_kernel(xc_ref, wg_ref, br_ref, bi_ref, lam_ref, hf_ref, gg_ref, x_ref, wo_ref,
                    g_ref, b_ref, wr_ref, xo_ref, aff_ref, carry, abuf, ubuf, hbuf, *, seq):
    _lru_chain_scan(xc_ref, wg_ref, br_ref, bi_ref, lam_ref, carry, abuf, ubuf, hbuf,
                    reverse=True, seq=seq)
    nseq, tm = x_ref.shape[0], x_ref.shape[1]
    gated = [((hf_ref[b] + _chain_tile(hbuf, b, tm)) * gg_ref[b]).astype(BF16) for b in range(nseq)]
    y_all = jnp.dot(jnp.concatenate(gated, axis=0), wo_ref[...], preferred_element_type=F32)
    for b in range(nseq):
        xn = _layer_norm(ALPHA * x_ref[b] + y_all[b * tm:(b + 1) * tm], g_ref[...], b_ref[...])
        _store_token_layout(xo_ref.at[b], xn)
        aff_ref[b] = _router_affinity_t(xn, wr_ref[...])


def _lru_specs(batch, tile_map):
    const2 = lambda i: (0, 0)
    return [
        pl.BlockSpec((batch, SCAN_TM, D_MODEL), tile_map),
        pl.BlockSpec((LRU_BLOCKS, LRU_BLOCK_W, 2 * LRU_BLOCK_W), lambda i: (0, 0, 0)),
        pl.BlockSpec((1, D_MODEL), const2),
        pl.BlockSpec((1, D_MODEL), const2),
        pl.BlockSpec((1, D_MODEL), const2),
    ]


def _lru_scratch():
    buf = pltpu.VMEM((PAIRS, SUBLANES * CHAIN_PITCH, LANES), F32)
    return [pltpu.VMEM((PAIRS, SUBLANES, LANES), F32), buf, buf, buf]


def _lru_forward(xc3, wg, br, bi, lam):
    batch, seq, _ = xc3.shape
    tile_map = lambda i: (0, i, 0)
    return pl.pallas_call(
        functools.partial(_lru_fwd_kernel, seq=seq),
        grid=(seq // SCAN_TM,),
        in_specs=_lru_specs(batch, tile_map),
        out_specs=pl.BlockSpec((batch, SCAN_TM, D_MODEL), tile_map),
        out_shape=jax.ShapeDtypeStruct((batch, seq, D_MODEL), F32),
        scratch_shapes=_lru_scratch(),
        compiler_params=pltpu.CompilerParams(dimension_semantics=("arbitrary",)),
        name="lru_forward_scan",
    )(xc3, wg, br, bi, lam)


def _lru_backward_out(xc3, wg, br, bi, lam, hf3, gg3, x3d, w_out, g, b, wr_t):
    batch, seq, _ = xc3.shape
    nt = seq // SCAN_TM
    tile_map = lambda i: (0, nt - 1 - i, 0)
    const2 = lambda i: (0, 0)
    tile = pl.BlockSpec((batch, SCAN_TM, D_MODEL), tile_map)
    return pl.pallas_call(
        functools.partial(_lru_bwd_kernel, seq=seq),
        grid=(nt,),
        in_specs=_lru_specs(batch, tile_map) + [
            tile, tile, tile,
            pl.BlockSpec((D_MODEL, D_MODEL), const2),
            pl.BlockSpec((1, D_MODEL), const2),
            pl.BlockSpec((1, D_MODEL), const2),
            pl.BlockSpec((N_EXPERTS, D_MODEL), const2),
        ],
        out_specs=[
            pl.BlockSpec((batch, SCAN_TM * TOK_ROWS, LANES), tile_map),
            pl.BlockSpec((batch, N_EXPERTS, SCAN_TM), lambda i: (0, 0, nt - 1 - i)),
        ],
        out_shape=[
            jax.ShapeDtypeStruct((batch, seq * TOK_ROWS, LANES), F32),
            jax.ShapeDtypeStruct((batch, N_EXPERTS, seq), F32),
        ],
        scratch_shapes=_lru_scratch(),
        compiler_params=pltpu.CompilerParams(dimension_semantics=("arbitrary",)),
        name="lru_backward_scan_out_ln_router",
    )(xc3, wg, br, bi, lam, hf3, gg3, x3d, w_out, g, b, wr_t)


def _rotary_tables(seq):
    pos = jnp.arange(seq, dtype=F32)
    inv_freq = ROPE_THETA ** (-jnp.arange(0, ROT_DIM, 2, dtype=F32) / ROT_DIM)
    ang = pos[:, None] * inv_freq[None, :]
    cos, sin = jnp.cos(ang), jnp.sin(ang)
    pad = HEAD_DIM - ROT_DIM
    cos_h = jnp.concatenate([cos, cos, jnp.ones((seq, pad), F32)], axis=1)
    sin_h = jnp.concatenate([-sin, sin, jnp.zeros((seq, pad), F32)], axis=1)
    reps = LANES // HEAD_DIM
    return jnp.tile(cos_h, (1, reps)), jnp.tile(sin_h, (1, reps))


def _moe_block(x1t, aff_t, w_gate_up, w_down, ln_g, ln_b, layer, batch, seq):
    cap = CAPACITY_FACTOR * seq // N_EXPERTS
    idx, gate_t = _route(aff_t, cap)
    return _moe(x1t, idx, gate_t, w_gate_up, w_down, ln_g[layer].reshape(1, -1), ln_b[layer].reshape(1, -1),
                layer, batch, seq)


def kernel(x, attn_w_qkv, attn_w_o, attn_sink, lru_w_in, lru_conv_w, lru_conv_b, lru_w_rgate,
           lru_b_rgate, lru_w_igate, lru_b_igate, lru_lambda, lru_w_out, moe_w_router, moe_w_gate_up,
           moe_w_down, ln_mix_g, ln_mix_b, ln_ffn_g, ln_ffn_b):
    batch, seq, _ = x.shape
    x2d = x.reshape(batch * seq, D_MODEL)
    row = lambda v: v.reshape(1, -1)

    cos_t, sin_t = _rotary_tables(seq)
    q, kv = _qkv_proj(x2d, attn_w_qkv[0].astype(BF16), cos_t, sin_t, seq)
    o = _attention(q, kv, attn_sink[0], batch, seq)
    x1, aff_t = _proj_ln(o, attn_w_o[0].astype(BF16), x2d, row(ln_mix_g[0]), row(ln_mix_b[0]),
                         moe_w_router[0].T.astype(BF16), batch, seq)
    x2 = _moe_block(x1, aff_t, moe_w_gate_up, moe_w_down, ln_ffn_g, ln_ffn_b, 0, batch, seq)

    gg, xc = _lru_in(x2, lru_w_in[0].astype(BF16), lru_conv_w[0], row(lru_conv_b[0]), seq)
    wg = (0.5 * jnp.concatenate([lru_w_rgate[0], lru_w_igate[0]], axis=-1)).astype(BF16)
    half_row = lambda v: 0.5 * v.reshape(1, -1)
    per_seq = lambda t: t.reshape(batch, seq, D_MODEL)
    xc3 = per_seq(xc)
    hf3 = _lru_forward(xc3, wg[0], half_row(lru_b_rgate[0, 0]), half_row(lru_b_igate[0, 0]),
                       row(lru_lambda[0, 0]))
    x3, aff_t = _lru_backward_out(xc3, wg[1], half_row(lru_b_rgate[0, 1]), half_row(lru_b_igate[0, 1]),
                                  row(lru_lambda[0, 1]), hf3, per_seq(gg), per_seq(x2),
                                  lru_w_out[0].astype(BF16), row(ln_mix_g[1]), row(ln_mix_b[1]),
                                  moe_w_router[1].T.astype(BF16))
    x3 = x3.reshape(batch * seq * TOK_ROWS, LANES)
    x4 = _moe_block(x3, aff_t, moe_w_gate_up, moe_w_down, ln_ffn_g, ln_ffn_b, 1, batch, seq)
    return x4.reshape(batch, seq, D_MODEL)
```

```python
import functools

import jax
import jax.numpy as jnp
from jax import lax
from jax.experimental import pallas as pl
from jax.experimental.pallas import tpu as pltpu

F32 = jnp.float32
BF16 = jnp.bfloat16

D_MODEL = 1024
HEAD_DIM = 64
N_Q_HEADS = 16
N_KV_HEADS = 4
GQA_GROUP = 4
WINDOW = 128
BLOCK = 128
ROT_DIM = 16
ROPE_THETA = 500000.0
LRU_BLOCKS = 4
LRU_BLOCK_W = 256
CONV_WIDTH = 4
LRU_C = 8.0
N_EXPERTS = 16
EXPERT_FF = 1024
CAPACITY_FACTOR = 2
LN_EPS = 1e-5
DEPTH = 2
ALPHA = (2 * DEPTH) ** 0.25

LANES = 128
SUBLANES = 8
MXU_COLS = 256
ROW_TILE = 1024
TOK_ROWS = D_MODEL // LANES
VMEM_LIMIT_MOE = 61 * 1024 * 1024


def _layer_norm(v, g, b):
    mu = jnp.mean(v, axis=-1, keepdims=True)
    vc = v - mu
    var = jnp.mean(vc * vc, axis=-1, keepdims=True)
    return vc * lax.rsqrt(var + LN_EPS) * g + b


def _store_token_layout(dst_ref, val):
    n = val.shape[0]
    for s in range(TOK_ROWS):
        dst_ref[pl.ds(s, n, stride=TOK_ROWS), :] = val[:, s * LANES:(s + 1) * LANES]


def _load_token_layout(src_ref, first_token, n):
    base = first_token * TOK_ROWS
    return jnp.concatenate(
        [src_ref[pl.ds(base + s, n, stride=TOK_ROWS), :] for s in range(TOK_ROWS)], axis=1)


def _router_affinity_t(xn, wr_t):
    logits_t = lax.dot_general(wr_t, xn.astype(BF16), (((1,), (1,)), ((), ())),
                               preferred_element_type=F32)
    m = jnp.max(logits_t, axis=0, keepdims=True)
    p = jnp.exp(logits_t - m)
    return p / jnp.sum(p, axis=0, keepdims=True)


def _dot_by_columns(lhs, w_ref, first_col, n_cols):
    return [jnp.dot(lhs, w_ref[:, c:c + MXU_COLS], preferred_element_type=F32)
            for c in range(first_col, first_col + n_cols, MXU_COLS)]


LOG2E = 1.4426950408889634


def _qkv_kernel(x_ref, w_ref, cos_ref, sin_ref, q_ref, kv_ref):
    x = x_ref[...].astype(BF16)
    acc = jnp.dot(x, w_ref[...], preferred_element_type=F32)
    cosb = cos_ref[...]
    sinb = sin_ref[...]
    lane = lax.broadcasted_iota(jnp.int32, cosb.shape, 1)
    low = (lane % HEAD_DIM) < (ROT_DIM // 2)
    nq = N_Q_HEADS * HEAD_DIM
    nk = N_KV_HEADS * HEAD_DIM

    def rot(t):
        up = pltpu.roll(t, LANES - ROT_DIM // 2, axis=1)
        dn = pltpu.roll(t, ROT_DIM // 2, axis=1)
        return t * cosb + jnp.where(low, up, dn) * sinb

    qs = [rot(acc[:, c * LANES:(c + 1) * LANES]) * (LOG2E * HEAD_DIM ** -0.5) for c in range(nq // LANES)]
    q_ref[...] = jnp.concatenate(qs, axis=1).astype(BF16)
    ks = [rot(acc[:, nq + c * LANES:nq + (c + 1) * LANES]) for c in range(nk // LANES)]
    vs = [acc[:, nq + nk + c * LANES:nq + nk + (c + 1) * LANES] for c in range(nk // LANES)]
    swap = lambda t: pltpu.roll(t, HEAD_DIM, axis=1)
    kv_ref[...] = jnp.concatenate(
        ks + [swap(t) for t in ks] + vs + [swap(t) for t in vs], axis=1).astype(BF16)


def _qkv_proj(x2d, w_qkv, cos_t, sin_t, seq):
    T = x2d.shape[0]
    tm = min(ROW_TILE, seq)
    nq = N_Q_HEADS * HEAD_DIM
    nk = N_KV_HEADS * HEAD_DIM
    nts = seq // tm
    return pl.pallas_call(
        _qkv_kernel,
        grid=(T // tm,),
        in_specs=[
            pl.BlockSpec((tm, D_MODEL), lambda i: (i, 0)),
            pl.BlockSpec((D_MODEL, nq + 2 * nk), lambda i: (0, 0)),
            pl.BlockSpec((tm, LANES), lambda i: (i % nts, 0)),
            pl.BlockSpec((tm, LANES), lambda i: (i % nts, 0)),
        ],
        out_specs=[
            pl.BlockSpec((tm, nq), lambda i: (i, 0)),
            pl.BlockSpec((tm, 4 * nk), lambda i: (i, 0)),
        ],
        out_shape=[
            jax.ShapeDtypeStruct((T, nq), BF16),
            jax.ShapeDtypeStruct((T, 4 * nk), BF16),
        ],
        compiler_params=pltpu.CompilerParams(dimension_semantics=("arbitrary",)),
        name="qkv_proj",
    )(x2d, w_qkv, cos_t, sin_t)


QBLKS = 8


def _attn_kernel(sink_ref, q_ref, kvp_ref, kvc_ref, kvn_ref, o_ref, *, seq):
    i = pl.program_id(1)
    nk = N_KV_HEADS * HEAD_DIM
    kv = jnp.concatenate([kvp_ref[...], kvc_ref[...], kvn_ref[...]], axis=0)
    band = 3 * BLOCK
    lo = lax.broadcasted_iota(jnp.int32, (1, LANES), 1) < HEAD_DIM
    zero = jnp.zeros((), BF16)
    k_lo, k_hi, v_dup_t = [], [], []
    for h in range(N_KV_HEADS):
        grp, half = h // 2, h % 2

        def block(base, swapped):
            col = base + (nk if swapped else 0) + grp * LANES
            return kv[:, col:col + LANES]
        k_lo.append(jnp.where(lo, block(0, half == 1), zero))
        k_hi.append(jnp.where(lo, zero, block(0, half == 0)))
        v_dup = jnp.where(lo, block(2 * nk, half == 1), block(2 * nk, half == 0))
        v_dup_t.append(v_dup.astype(F32).T.astype(BF16))

    key = lax.broadcasted_iota(jnp.int32, (BLOCK, BLOCK), 0)
    qry = lax.broadcasted_iota(jnp.int32, (BLOCK, BLOCK), 1)
    feat_lo = lax.broadcasted_iota(jnp.int32, (LANES, 1), 0) < HEAD_DIM
    units = [(j, h) for j in range(QBLKS) for h in range(N_KV_HEADS)]

    def scores_t(j, h):
        rows = slice(j * BLOCK, (j + 1) * BLOCK)
        qp = jnp.concatenate([q_ref[rows, (2 * h) * LANES:(2 * h + 1) * LANES],
                              q_ref[rows, (2 * h + 1) * LANES:(2 * h + 2) * LANES]], axis=0)
        keys = jnp.concatenate([k_lo[h][j * BLOCK:j * BLOCK + band], k_hi[h][j * BLOCK:j * BLOCK + band]],
                               axis=0)
        return lax.dot_general(keys, qp, (((1,), (1,)), ((), ())), preferred_element_type=F32)

    s_next = scores_t(*units[0])
    for u, (j, h) in enumerate(units):
        s = s_next
        if u + 1 < len(units):
            s_next = scores_t(*units[u + 1])
        n = i * QBLKS + j
        ok_prev = jnp.concatenate([(key >= qry) & (n > 0)] * 2, axis=1)
        ok_next = jnp.concatenate([(key <= qry) & (n < seq // BLOCK - 1)] * 2, axis=1)
        vt = v_dup_t[h][:, j * BLOCK:j * BLOCK + band]
        halves = []
        for par in range(2):
            sp = s[par * band:(par + 1) * band, :]
            sp = jnp.concatenate([jnp.where(ok_prev, sp[:BLOCK], -jnp.inf), sp[BLOCK:2 * BLOCK],
                                  jnp.where(ok_next, sp[2 * BLOCK:], -jnp.inf)], axis=0)
            sink_row = jnp.concatenate(
                [jnp.full((1, BLOCK), sink_ref[h * GQA_GROUP + g] * LOG2E, F32) for g in (par, par + 2)],
                axis=1)
            m = jnp.maximum(jnp.max(sp, axis=0, keepdims=True), sink_row)
            p = jnp.exp2(sp - m)
            denom = jnp.sum(p, axis=0, keepdims=True) + jnp.exp2(sink_row - m)
            halves.append(jnp.dot(vt, p.astype(BF16), preferred_element_type=F32) / denom)
        out_t = jnp.where(feat_lo, halves[0], halves[1])
        o_ref[j * BLOCK:(j + 1) * BLOCK, (2 * h) * LANES:(2 * h + 2) * LANES] = jnp.concatenate(
            [out_t[:, :BLOCK].T, out_t[:, BLOCK:].T], axis=1).astype(BF16)


def _attention(q, kv, sink, batch, seq):
    T = q.shape[0]
    qt = QBLKS * BLOCK
    nq = seq // qt
    nb = seq // BLOCK
    wkv = kv.shape[1]
    cur = lambda b, i: (b * nq + i, 0)
    prev = lambda b, i: (b * nb + jnp.maximum(i * QBLKS - 1, 0), 0)
    nxt = lambda b, i: (b * nb + jnp.minimum(i * QBLKS + QBLKS, nb - 1), 0)
    return pl.pallas_call(
        functools.partial(_attn_kernel, seq=seq),
        grid=(batch, nq),
        in_specs=[
            pl.BlockSpec(memory_space=pltpu.SMEM),
            pl.BlockSpec((qt, D_MODEL), cur),
            pl.BlockSpec((BLOCK, wkv), prev),
            pl.BlockSpec((qt, wkv), cur),
            pl.BlockSpec((BLOCK, wkv), nxt),
        ],
        out_specs=pl.BlockSpec((qt, D_MODEL), cur),
        out_shape=jax.ShapeDtypeStruct((T, D_MODEL), BF16),
        compiler_params=pltpu.CompilerParams(dimension_semantics=("arbitrary", "arbitrary")),
        name="swa_attention",
    )(sink, q, kv, kv, kv)


def _proj_ln_kernel(a_ref, w_ref, x_ref, g_ref, b_ref, wr_ref, xo_ref, aff_ref):
    y = jnp.dot(a_ref[...], w_ref[...], preferred_element_type=F32)
    xn = _layer_norm(ALPHA * x_ref[...] + y, g_ref[...], b_ref[...])
    _store_token_layout(xo_ref, xn)
    aff_ref[0] = _router_affinity_t(xn, wr_ref[...])


def _proj_ln(a, w, x2d, g, b, wr_t, batch, seq):
    T = x2d.shape[0]
    tm = min(ROW_TILE, seq)
    nts = seq // tm
    return pl.pallas_call(
        _proj_ln_kernel,
        grid=(T // tm,),
        in_specs=[
            pl.BlockSpec((tm, D_MODEL), lambda i: (i, 0)),
            pl.BlockSpec((D_MODEL, D_MODEL), lambda i: (0, 0)),
            pl.BlockSpec((tm, D_MODEL), lambda i: (i, 0)),
            pl.BlockSpec((1, D_MODEL), lambda i: (0, 0)),
            pl.BlockSpec((1, D_MODEL), lambda i: (0, 0)),
            pl.BlockSpec((N_EXPERTS, D_MODEL), lambda i: (0, 0)),
        ],
        out_specs=[
            pl.BlockSpec((tm * TOK_ROWS, LANES), lambda i: (i, 0)),
            pl.BlockSpec((1, N_EXPERTS, tm), lambda i: (i // nts, 0, i % nts)),
        ],
        out_shape=[
            jax.ShapeDtypeStruct((T * TOK_ROWS, LANES), F32),
            jax.ShapeDtypeStruct((batch, N_EXPERTS, seq), F32),
        ],
        compiler_params=pltpu.CompilerParams(dimension_semantics=("arbitrary",)),
        name="proj_ln_router",
    )(a, w, x2d, g, b, wr_t)


SLOT_BLOCK = 64


def _lane_prefix(mask, tri, dst_ref):
    seq = mask.shape[1]
    off = jnp.zeros((mask.shape[0], 1), F32)
    mb = mask.astype(BF16)
    ends = []
    for c in range(seq // LANES):
        res = jnp.dot(mb[:, c * LANES:(c + 1) * LANES], tri, preferred_element_type=F32) + off
        dst_ref[:, c * LANES:(c + 1) * LANES] = res
        off = res[:, LANES - 1:LANES]
        ends.append(off)
    return ends


def _select_kernel(aff_ref, pos_ref, bnd_ref, *, cap):
    aff = aff_ref[0]
    as_float = lambda bits: pltpu.bitcast(bits, F32)
    rows = aff.shape[0]
    cur = jnp.zeros((rows, 1), jnp.int32)
    for bit in range(30, -1, -1):
        cand = cur | (1 << bit)
        cnt = jnp.sum((aff >= as_float(cand)).astype(jnp.int32), axis=1, keepdims=True)
        cur = jnp.where(cnt >= cap, cand, cur)
    gt = aff >= as_float(cur + 1)
    eq = (aff >= as_float(cur)) & jnp.logical_not(gt)
    n_gt = jnp.sum(gt.astype(jnp.int32), axis=1, keepdims=True)
    ties_taken = (cap - n_gt).astype(F32)
    ri = lax.broadcasted_iota(jnp.int32, (LANES, LANES), 0)
    ci = lax.broadcasted_iota(jnp.int32, (LANES, LANES), 1)
    tri = (ri <= ci).astype(BF16)
    _lane_prefix(eq, tri, pos_ref.at[0])
    sel = gt | (eq & (pos_ref[0] <= ties_taken))
    ends = _lane_prefix(sel, tri, pos_ref.at[0])
    pos_ref[0] = jnp.where(sel, pos_ref[0], 0.0)
    cols = []
    for sb in range(cap // SLOT_BLOCK):
        for target in (sb * SLOT_BLOCK + 1, (sb + 1) * SLOT_BLOCK):
            blk = jnp.zeros((rows, 1), jnp.int32)
            for end in ends[:-1]:
                blk = blk + (end < target).astype(jnp.int32)
            cols.append(blk)
    bnd_ref[0] = jnp.concatenate(cols, axis=1)


def _compact_kernel(bnd_ref, pos_ref, aff_ref, idx_ref, gate_ref, ptok, pgate, *, cap):
    b = pl.program_id(0)
    n_sb = cap // SLOT_BLOCK
    groups = SLOT_BLOCK // SUBLANES
    lane = lax.broadcasted_iota(jnp.int32, (SUBLANES, LANES), 1).astype(F32)
    sub1 = (lax.broadcasted_iota(jnp.int32, (SUBLANES, 1), 0) + 1).astype(F32)
    lane_e = lax.broadcasted_iota(jnp.int32, (cap, N_EXPERTS), 1)
    idx_ref[0] = jnp.zeros((cap, N_EXPERTS), jnp.int32)
    gate_ref[0] = jnp.zeros((cap, N_EXPERTS), F32)

    def expert(e, carry):
        def slot_block(sb, carry2):
            base = ((b * N_EXPERTS + e) * n_sb + sb) * 2
            slot0 = jnp.asarray(sb * SLOT_BLOCK, F32)

            def lane_block(c, accs):
                off = pl.multiple_of(c * LANES, LANES)
                p = jnp.broadcast_to(pos_ref[0, e:e + 1, pl.ds(off, LANES)], (SUBLANES, LANES))
                a = jnp.broadcast_to(aff_ref[0, e:e + 1, pl.ds(off, LANES)], (SUBLANES, LANES))
                tokv = lane + jnp.asarray(c * LANES, F32)
                out = []
                for g in range(groups):
                    hit = p == (slot0 + (g * SUBLANES) + sub1)
                    out.append(jnp.where(hit, tokv, accs[2 * g]))
                    out.append(jnp.where(hit, a, accs[2 * g + 1]))
                return tuple(out)

            zeros = tuple(jnp.zeros((SUBLANES, LANES), F32) for _ in range(2 * groups))
            accs = lax.fori_loop(bnd_ref[base], bnd_ref[base + 1] + 1, lane_block, zeros)
            for g in range(groups):
                rows = pl.ds(pl.multiple_of(sb * SLOT_BLOCK + g * SUBLANES, SUBLANES), SUBLANES)
                ptok[rows, :] = accs[2 * g]
                pgate[rows, :] = accs[2 * g + 1]
            return carry2

        lax.fori_loop(0, n_sb, slot_block, 0)
        tok_col = jnp.sum(ptok[...], axis=1, keepdims=True).astype(jnp.int32)
        gate_col = jnp.sum(pgate[...], axis=1, keepdims=True)
        idx_ref[0] = jnp.where(lane_e == e, tok_col, idx_ref[0])
        gate_ref[0] = jnp.where(lane_e == e, gate_col, gate_ref[0])
        return carry

    for e in range(N_EXPERTS):
        expert(e, 0)


def _route(aff_t, cap):
    batch, _, seq = aff_t.shape
    n_bnd = 2 * (cap // SLOT_BLOCK)
    seq_spec = pl.BlockSpec((1, N_EXPERTS, seq), lambda b, *_: (b, 0, 0))
    rows = batch * N_EXPERTS
    pos, bnd = pl.pallas_call(
        functools.partial(_select_kernel, cap=cap),
        grid=(1,),
        in_specs=[pl.BlockSpec((1, rows, seq), lambda i: (0, 0, 0))],
        out_specs=[pl.BlockSpec((1, rows, seq), lambda i: (0, 0, 0)),
                   pl.BlockSpec((1, rows, n_bnd), lambda i: (0, 0, 0))],
        out_shape=[
            jax.ShapeDtypeStruct((1, rows, seq), F32),
            jax.ShapeDtypeStruct((1, rows, n_bnd), jnp.int32),
        ],
        compiler_params=pltpu.CompilerParams(dimension_semantics=("arbitrary",)),
        name="expert_choice_select",
    )(aff_t.reshape(1, rows, seq))
    pos = pos.reshape(batch, N_EXPERTS, seq)
    slot_spec = pl.BlockSpec((1, cap, N_EXPERTS), lambda b, *_: (b, 0, 0))
    idx_t, gate_t = pl.pallas_call(
        functools.partial(_compact_kernel, cap=cap),
        grid_spec=pltpu.PrefetchScalarGridSpec(
            num_scalar_prefetch=1,
            grid=(batch,),
            in_specs=[seq_spec, seq_spec],
            out_specs=[slot_spec, slot_spec],
            scratch_shapes=[pltpu.VMEM((cap, LANES), F32), pltpu.VMEM((cap, LANES), F32)],
        ),
        out_shape=[
            jax.ShapeDtypeStruct((batch, cap, N_EXPERTS), jnp.int32),
            jax.ShapeDtypeStruct((batch, cap, N_EXPERTS), F32),
        ],
        compiler_params=pltpu.CompilerParams(dimension_semantics=("arbitrary",)),
        name="expert_choice_compact",
    )(bnd.reshape(-1), pos, aff_t)
    return jnp.swapaxes(idx_t, 1, 2), gate_t


FF_CHUNKS = 2
LN_CHUNK = 256
SCATTER_GROUP = 2


def _moe_kernel(idx_ref, x1t_hbm, gate_ref, wg_ref, wu_ref, wd_ref, lng_ref, lnb_ref, out_hbm,
                x1v, acc, xs3, y3, xs2d, yacc, stage, sem_in, sem_out, *, seq, cap):
    b = pl.program_id(0)
    e = pl.program_id(1)
    j = pl.program_id(2)
    tok_rows = seq * TOK_ROWS
    last_e = N_EXPERTS - 1

    def tok_slice(t):
        return pl.ds(pl.multiple_of(t * TOK_ROWS, TOK_ROWS), TOK_ROWS)

    def gather_rows(expert):
        base = (b * N_EXPERTS + expert) * cap
        for i in range(cap):
            xs3[i * TOK_ROWS:(i + 1) * TOK_ROWS, :] = x1v[tok_slice(idx_ref[base + i]), :]

    def scatter_rows(expert):
        base = (b * N_EXPERTS + expert) * cap
        for g0 in range(0, cap, SCATTER_GROUP):
            toks = [idx_ref[base + g0 + k] for k in range(SCATTER_GROUP)]
            vals = [acc[tok_slice(toks[k]), :] + y3[(g0 + k) * TOK_ROWS:(g0 + k + 1) * TOK_ROWS, :]
                    for k in range(SCATTER_GROUP)]
            for k in range(SCATTER_GROUP):
                acc[tok_slice(toks[k]), :] = vals[k]

    def expert_chunk(slot):
        xs = xs2d[slot]
        g = jnp.dot(xs, wg_ref[0, 0].astype(BF16), preferred_element_type=F32)
        u = jnp.dot(xs, wu_ref[0, 0].astype(BF16), preferred_element_type=F32)
        act = (g * jax.nn.sigmoid(g) * u).astype(BF16)
        return jnp.dot(act, wd_ref[0, 0].astype(BF16), preferred_element_type=F32)

    def load_sequence(seq_idx):
        return pltpu.make_async_copy(x1t_hbm.at[pl.ds(seq_idx * tok_rows, tok_rows)], x1v, sem_in)

    def init_accumulator():
        acc[...] = ALPHA * x1v[...]

    @pl.when((b == 0) & (e == 0) & (j == 0))
    def _first_sequence():
        load_sequence(0).start()
        load_sequence(0).wait()
        init_accumulator()

    @pl.when((e == 0) & (j == 0))
    def _start_sequence():
        yacc[...] = jnp.zeros_like(yacc)

        def body(i, carry):
            xs3[tok_slice(i), :] = x1v[tok_slice(idx_ref[b * N_EXPERTS * cap + i]), :]
            return carry
        lax.fori_loop(0, cap, body, 0, unroll=8)
        xs2d[0] = _load_token_layout(xs3, 0, cap).astype(BF16)

    def gate_column(expert):
        gates = gate_ref[0]
        lane_e = lax.broadcasted_iota(jnp.int32, gates.shape, 1)
        return jnp.sum(jnp.where(lane_e == expert, gates, 0.0), axis=1, keepdims=True)

    @pl.when(j == 0)
    def _chunk0():
        prev = jnp.maximum(e - 1, 0)
        _store_token_layout(y3, yacc[...] * gate_column(prev))
        scatter_rows(prev)
        yacc[...] = expert_chunk(e % 2)

    @pl.when(j == 1)
    def _chunk1():
        gather_rows(jnp.minimum(e + 1, last_e))
        yacc[...] += expert_chunk(e % 2)
        xs2d[(e + 1) % 2] = _load_token_layout(xs3, 0, cap).astype(BF16)

    @pl.when((e == last_e) & (j == 1))
    def _finalize():
        _store_token_layout(y3, yacc[...] * gate_column(last_e))

        def body(gi, carry):
            base = (b * N_EXPERTS + last_e) * cap + gi * SUBLANES
            toks = [idx_ref[base + k] for k in range(SUBLANES)]
            vals = [acc[tok_slice(toks[k]), :] + y3[tok_slice(gi * SUBLANES + k), :]
                    for k in range(SUBLANES)]
            for k in range(SUBLANES):
                acc[tok_slice(toks[k]), :] = vals[k]
            return carry
        lax.fori_loop(0, cap // SUBLANES, body, 0)

        n_seq = pl.num_programs(0)

        @pl.when(b + 1 < n_seq)
        def _():
            load_sequence(b + 1).start()

        n_chunks = seq // LN_CHUNK

        def out_copy(c, slot):
            return pltpu.make_async_copy(
                stage.at[slot], out_hbm.at[pl.ds(b * seq + c * LN_CHUNK, LN_CHUNK)], sem_out.at[slot])

        def ln_body(c, carry):
            slot = c % 2
            first = c * LN_CHUNK
            res = _layer_norm(_load_token_layout(acc, first, LN_CHUNK), lng_ref[...], lnb_ref[...])

            @pl.when(c >= 2)
            def _():
                out_copy(c - 2, slot).wait()
            stage[slot] = res
            out_copy(c, slot).start()
            return carry
        lax.fori_loop(0, n_chunks, ln_body, 0)
        out_copy(n_chunks - 2, (n_chunks - 2) % 2).wait()
        out_copy(n_chunks - 1, (n_chunks - 1) % 2).wait()

        @pl.when(b + 1 < n_seq)
        def _():
            load_sequence(b + 1).wait()
            init_accumulator()


def _moe(x1t, idx, gate_t, w_gate_up, w_down, ln_g, ln_b, layer, batch, seq):
    cap = idx.shape[2]
    T = batch * seq
    fc = EXPERT_FF // FF_CHUNKS
    assert FF_CHUNKS == 2
    idx_flat = idx.reshape(-1)
    grid_spec = pltpu.PrefetchScalarGridSpec(
        num_scalar_prefetch=1,
        grid=(batch, N_EXPERTS, FF_CHUNKS),
        in_specs=[
            pl.BlockSpec(memory_space=pl.ANY),
            pl.BlockSpec((1, cap, N_EXPERTS), lambda b, e, j, idx: (b, 0, 0)),
            pl.BlockSpec((1, 1, D_MODEL, fc), lambda b, e, j, idx: (layer, e, 0, j)),
            pl.BlockSpec((1, 1, D_MODEL, fc), lambda b, e, j, idx: (layer, e, 0, FF_CHUNKS + j)),
            pl.BlockSpec((1, 1, fc, D_MODEL), lambda b, e, j, idx: (layer, e, j, 0)),
            pl.BlockSpec((1, D_MODEL), lambda b, e, j, idx: (0, 0)),
            pl.BlockSpec((1, D_MODEL), lambda b, e, j, idx: (0, 0)),
        ],
        out_specs=pl.BlockSpec(memory_space=pl.ANY),
        scratch_shapes=[
            pltpu.VMEM((seq * TOK_ROWS, LANES), F32),
            pltpu.VMEM((seq * TOK_ROWS, LANES), F32),
            pltpu.VMEM((cap * TOK_ROWS, LANES), F32),
            pltpu.VMEM((cap * TOK_ROWS, LANES), F32),
            pltpu.VMEM((2, cap, D_MODEL), BF16),
            pltpu.VMEM((cap, D_MODEL), F32),
            pltpu.VMEM((2, LN_CHUNK, D_MODEL), F32),
            pltpu.SemaphoreType.DMA(()),
            pltpu.SemaphoreType.DMA((2,)),
        ],
    )
    return pl.pallas_call(
        functools.partial(_moe_kernel, seq=seq, cap=cap),
        grid_spec=grid_spec,
        out_shape=jax.ShapeDtypeStruct((T, D_MODEL), F32),
        compiler_params=pltpu.CompilerParams(
            dimension_semantics=("arbitrary", "arbitrary", "arbitrary"),
            vmem_limit_bytes=VMEM_LIMIT_MOE),
        name="expert_ffn_combine_ln",
    )(idx_flat, x1t, gate_t, w_gate_up, w_gate_up, w_down, ln_g, ln_b)


HALO = SUBLANES


def _gelu_tanh(x):
    return 0.5 * x * (1.0 + jnp.tanh(0.7978845608028654 * (x + 0.044715 * (x * x * x))))


def _lru_in_kernel(x_ref, xp_ref, xn_ref, w_ref, cw_ref, cb_ref, gg_ref, xc_ref, *, nts):
    i = pl.program_id(0)
    tm = x_ref.shape[0]
    x = x_ref[...].astype(BF16)
    xp = xp_ref[...].astype(BF16)
    xn = xn_ref[...].astype(BF16)
    first = (i % nts) == 0
    last = (i % nts) == nts - 1
    for k, gate in enumerate(_dot_by_columns(x, w_ref, 0, D_MODEL)):
        gg_ref[:, k * MXU_COLS:(k + 1) * MXU_COLS] = _gelu_tanh(gate)
    n_ext = tm + 2 * HALO
    for k in range(D_MODEL // MXU_COLS):
        cols = slice(k * MXU_COLS, (k + 1) * MXU_COLS)
        xb, = _dot_by_columns(x, w_ref, D_MODEL + k * MXU_COLS, MXU_COLS)
        xbp, = _dot_by_columns(xp, w_ref, D_MODEL + k * MXU_COLS, MXU_COLS)
        xbn, = _dot_by_columns(xn, w_ref, D_MODEL + k * MXU_COLS, MXU_COLS)
        ext = jnp.concatenate([jnp.where(first, 0.0, xbp), xb, jnp.where(last, 0.0, xbn)], axis=0)
        xc = cb_ref[:, cols]
        for tap in range(CONV_WIDTH):
            shift = (2 - tap) % n_ext
            src = ext if shift == 0 else pltpu.roll(ext, shift, axis=0)
            xc = xc + cw_ref[tap:tap + 1, cols] * src[HALO:HALO + tm, :]
        xc_ref[:, cols] = xc


def _lru_in(x2d, w_in, conv_w, conv_b, seq):
    T = x2d.shape[0]
    tm = min(ROW_TILE, seq)
    nts = seq // tm
    hb = tm // HALO
    nblk = T // HALO
    return pl.pallas_call(
        functools.partial(_lru_in_kernel, nts=nts),
        grid=(T // tm,),
        in_specs=[
            pl.BlockSpec((tm, D_MODEL), lambda i: (i, 0)),
            pl.BlockSpec((HALO, D_MODEL), lambda i: (jnp.maximum(i * hb - 1, 0), 0)),
            pl.BlockSpec((HALO, D_MODEL), lambda i: (jnp.minimum((i + 1) * hb, nblk - 1), 0)),
            pl.BlockSpec((D_MODEL, 2 * D_MODEL), lambda i: (0, 0)),
            pl.BlockSpec((CONV_WIDTH, D_MODEL), lambda i: (0, 0)),
            pl.BlockSpec((1, D_MODEL), lambda i: (0, 0)),
        ],
        out_specs=[
            pl.BlockSpec((tm, D_MODEL), lambda i: (i, 0)),
            pl.BlockSpec((tm, D_MODEL), lambda i: (i, 0)),
        ],
        out_shape=[
            jax.ShapeDtypeStruct((T, D_MODEL), F32),
            jax.ShapeDtypeStruct((T, D_MODEL), F32),
        ],
        compiler_params=pltpu.CompilerParams(dimension_semantics=("arbitrary",)),
        name="lru_in_conv",
    )(x2d, x2d, x2d, w_in, conv_w, conv_b)


SCAN_TM = 128
CHAIN_PITCH = SCAN_TM + SUBLANES
PAIRS = D_MODEL // (2 * LANES)


def _lru_chain_scan(xc_ref, wg_ref, br_ref, bi_ref, lam_ref, carry, abuf, ubuf, hbuf, *, reverse, seq):
    i = pl.program_id(0)
    nseq, tm = xc_ref.shape[0], xc_ref.shape[1]
    assert 2 * nseq == SUBLANES and tm == SCAN_TM
    nt = seq // tm
    tile = (nt - 1 - i) if reverse else i

    @pl.when(i == 0)
    def _():
        carry[...] = jnp.zeros_like(carry)

    half_nsp = (-0.5 * LRU_C) * jax.nn.softplus(-lam_ref[...])
    row = lax.broadcasted_iota(jnp.int32, (tm, 1), 0)
    is_start = (tile * tm + row) == (seq - 1 if reverse else 0)
    xc_all = jnp.concatenate([xc_ref[b] for b in range(nseq)], axis=0)
    xcb = xc_all.astype(BF16)
    r_parts, i_parts = [], []
    for n in range(LRU_BLOCKS):
        res = jnp.dot(xcb[:, n * LRU_BLOCK_W:(n + 1) * LRU_BLOCK_W], wg_ref[n],
                      preferred_element_type=F32)
        r_parts.append(res[:, :LRU_BLOCK_W])
        i_parts.append(res[:, LRU_BLOCK_W:])
    r_pre = jnp.concatenate(r_parts, axis=1)
    i_pre = jnp.concatenate(i_parts, axis=1)
    for b in range(nseq):
        rows_b = slice(b * tm, (b + 1) * tm)
        xc = xc_all[rows_b]
        tr = jnp.tanh(r_pre[rows_b] + br_ref[...])
        ig = 0.5 * jnp.tanh(i_pre[rows_b] + bi_ref[...]) + 0.5
        log_a = tr * half_nsp + half_nsp
        a = jnp.exp(log_a)
        z = -jnp.tanh(log_a) * (1.0 + a * a)
        root = jnp.where(z > 0.0, z * lax.rsqrt(z), 0.0)
        u = jnp.where(is_start, 1.0, root) * ig * xc
        for lb in range(2 * PAIRS):
            rows = slice((2 * b + lb % 2) * CHAIN_PITCH, (2 * b + lb % 2) * CHAIN_PITCH + tm)
            abuf[lb // 2, rows, :] = a[:, lb * LANES:(lb + 1) * LANES]
            ubuf[lb // 2, rows, :] = u[:, lb * LANES:(lb + 1) * LANES]

    def step(k, hs):
        t = (tm - 1 - k) if reverse else k
        chains = pl.ds(t, SUBLANES, stride=CHAIN_PITCH)
        out = []
        for m in range(PAIRS):
            h = abuf[m, chains, :] * hs[m] + ubuf[m, chains, :]
            hbuf[m, chains, :] = h
            out.append(h)
        return tuple(out)

    hs = lax.fori_loop(0, tm, step, tuple(carry[m] for m in range(PAIRS)), unroll=8)
    for m in range(PAIRS):
        carry[m] = hs[m]


def _chain_tile(hbuf, b, tm):
    return jnp.concatenate(
        [hbuf[lb // 2, (2 * b + lb % 2) * CHAIN_PITCH:(2 * b + lb % 2) * CHAIN_PITCH + tm, :]
         for lb in range(2 * PAIRS)], axis=1)


def _lru_fwd_kernel(xc_ref, wg_ref, br_ref, bi_ref, lam_ref, h_ref, carry, abuf, ubuf, hbuf, *, seq):
    _lru_chain_scan(xc_ref, wg_ref, br_ref, bi_ref, lam_ref, carry, abuf, ubuf, hbuf,
                    reverse=False, seq=seq)
    for b in range(h_ref.shape[0]):
        h_ref[b] = _chain_tile(hbuf, b, h_ref.shape[1])


def _lru_bwd_kernel(xc_ref, wg_ref, br_ref, bi_ref, lam_ref, hf_ref, gg_ref, x_ref, wo_ref,
                    g_ref, b_ref, wr_ref, xo_ref, aff_ref, carry, abuf, ubuf, hbuf, *, seq):
    _lru_chain_scan(xc_ref, wg_ref, br_ref, bi_ref, lam_ref, carry, abuf, ubuf, hbuf,
                    reverse=True, seq=seq)
    nseq, tm = x_ref.shape[0], x_ref.shape[1]
    gated = [((hf_ref[b] + _chain_tile(hbuf, b, tm)) * gg_ref[b]).astype(BF16) for b in range(nseq)]
    y_all = jnp.dot(jnp.concatenate(gated, axis=0), wo_ref[...], preferred_element_type=F32)
    for b in range(nseq):
        xn = _layer_norm(ALPHA * x_ref[b] + y_all[b * tm:(b + 1) * tm], g_ref[...], b_ref[...])
        _store_token_layout(xo_ref.at[b], xn)
        aff_ref[b] = _router_affinity_t(xn, wr_ref[...])


def _lru_specs(batch, tile_map):
    const2 = lambda i: (0, 0)
    return [
        pl.BlockSpec((batch, SCAN_TM, D_MODEL), tile_map),
        pl.BlockSpec((LRU_BLOCKS, LRU_BLOCK_W, 2 * LRU_BLOCK_W), lambda i: (0, 0, 0)),
        pl.BlockSpec((1, D_MODEL), const2),
        pl.BlockSpec((1, D_MODEL), const2),
        pl.BlockSpec((1, D_MODEL), const2),
    ]


def _lru_scratch():
    buf = pltpu.VMEM((PAIRS, SUBLANES * CHAIN_PITCH, LANES), F32)
    return [pltpu.VMEM((PAIRS, SUBLANES, LANES), F32), buf, buf, buf]


def _lru_forward(xc3, wg, br, bi, lam):
    batch, seq, _ = xc3.shape
    tile_map = lambda i: (0, i, 0)
    return pl.pallas_call(
        functools.partial(_lru_fwd_kernel, seq=seq),
        grid=(seq // SCAN_TM,),
        in_specs=_lru_specs(batch, tile_map),
        out_specs=pl.BlockSpec((batch, SCAN_TM, D_MODEL), tile_map),
        out_shape=jax.ShapeDtypeStruct((batch, seq, D_MODEL), F32),
        scratch_shapes=_lru_scratch(),
        compiler_params=pltpu.CompilerParams(dimension_semantics=("arbitrary",)),
        name="lru_forward_scan",
    )(xc3, wg, br, bi, lam)


def _lru_backward_out(xc3, wg, br, bi, lam, hf3, gg3, x3d, w_out, g, b, wr_t):
    batch, seq, _ = xc3.shape
    nt = seq // SCAN_TM
    tile_map = lambda i: (0, nt - 1 - i, 0)
    const2 = lambda i: (0, 0)
    tile = pl.BlockSpec((batch, SCAN_TM, D_MODEL), tile_map)
    return pl.pallas_call(
        functools.partial(_lru_bwd_kernel, seq=seq),
        grid=(nt,),
        in_specs=_lru_specs(batch, tile_map) + [
            tile, tile, tile,
            pl.BlockSpec((D_MODEL, D_MODEL), const2),
            pl.BlockSpec((1, D_MODEL), const2),
            pl.BlockSpec((1, D_MODEL), const2),
            pl.BlockSpec((N_EXPERTS, D_MODEL), const2),
        ],
        out_specs=[
            pl.BlockSpec((batch, SCAN_TM * TOK_ROWS, LANES), tile_map),
            pl.BlockSpec((batch, N_EXPERTS, SCAN_TM), lambda i: (0, 0, nt - 1 - i)),
        ],
        out_shape=[
            jax.ShapeDtypeStruct((batch, seq * TOK_ROWS, LANES), F32),
            jax.ShapeDtypeStruct((batch, N_EXPERTS, seq), F32),
        ],
        scratch_shapes=_lru_scratch(),
        compiler_params=pltpu.CompilerParams(dimension_semantics=("arbitrary",)),
        name="lru_backward_scan_out_ln_router",
    )(xc3, wg, br, bi, lam, hf3, gg3, x3d, w_out, g, b, wr_t)


def _rotary_tables(seq):
    pos = jnp.arange(seq, dtype=F32)
    inv_freq = ROPE_THETA ** (-jnp.arange(0, ROT_DIM, 2, dtype=F32) / ROT_DIM)
    ang = pos[:, None] * inv_freq[None, :]
    cos, sin = jnp.cos(ang), jnp.sin(ang)
    pad = HEAD_DIM - ROT_DIM
    cos_h = jnp.concatenate([cos, cos, jnp.ones((seq, pad), F32)], axis=1)
    sin_h = jnp.concatenate([-sin, sin, jnp.zeros((seq, pad), F32)], axis=1)
    reps = LANES // HEAD_DIM
    return jnp.tile(cos_h, (1, reps)), jnp.tile(sin_h, (1, reps))


def _moe_block(x1t, aff_t, w_gate_up, w_down, ln_g, ln_b, layer, batch, seq):
    cap = CAPACITY_FACTOR * seq // N_EXPERTS
    idx, gate_t = _route(aff_t, cap)
    return _moe(x1t, idx, gate_t, w_gate_up, w_down, ln_g[layer].reshape(1, -1), ln_b[layer].reshape(1, -1),
                layer, batch, seq)


def kernel(x, attn_w_qkv, attn_w_o, attn_sink, lru_w_in, lru_conv_w, lru_conv_b, lru_w_rgate,
           lru_b_rgate, lru_w_igate, lru_b_igate, lru_lambda, lru_w_out, moe_w_router, moe_w_gate_up,
           moe_w_down, ln_mix_g, ln_mix_b, ln_ffn_g, ln_ffn_b):
    batch, seq, _ = x.shape
    x2d = x.reshape(batch * seq, D_MODEL)
    row = lambda v: v.reshape(1, -1)

    cos_t, sin_t = _rotary_tables(seq)
    q, kv = _qkv_proj(x2d, attn_w_qkv[0].astype(BF16), cos_t, sin_t, seq)
    o = _attention(q, kv, attn_sink[0], batch, seq)
    x1, aff_t = _proj_ln(o, attn_w_o[0].astype(BF16), x2d, row(ln_mix_g[0]), row(ln_mix_b[0]),
                         moe_w_router[0].T.astype(BF16), batch, seq)
    x2 = _moe_block(x1, aff_t, moe_w_gate_up, moe_w_down, ln_ffn_g, ln_ffn_b, 0, batch, seq)

    gg, xc = _lru_in(x2, lru_w_in[0].astype(BF16), lru_conv_w[0], row(lru_conv_b[0]), seq)
    wg = (0.5 * jnp.concatenate([lru_w_rgate[0], lru_w_igate[0]], axis=-1)).astype(BF16)
    half_row = lambda v: 0.5 * v.reshape(1, -1)
    per_seq = lambda t: t.reshape(batch, seq, D_MODEL)
    xc3 = per_seq(xc)
    hf3 = _lru_forward(xc3, wg[0], half_row(lru_b_rgate[0, 0]), half_row(lru_b_igate[0, 0]),
                       row(lru_lambda[0, 0]))
    x3, aff_t = _lru_backward_out(xc3, wg[1], half_row(lru_b_rgate[0, 1]), half_row(lru_b_igate[0, 1]),
                                  row(lru_lambda[0, 1]), hf3, per_seq(gg), per_seq(x2),
                                  lru_w_out[0].astype(BF16), row(ln_mix_g[1]), row(ln_mix_b[1]),
                                  moe_w_router[1].T.astype(BF16))
    x3 = x3.reshape(batch * seq * TOK_ROWS, LANES)
    x4 = _moe_block(x3, aff_t, moe_w_gate_up, moe_w_down, ln_ffn_g, ln_ffn_b, 1, batch, seq)
    return x4.reshape(batch, seq, D_MODEL)
```

```python
import functools

import jax
import jax.numpy as jnp
from jax import lax
from jax.experimental import pallas as pl
from jax.experimental.pallas import tpu as pltpu

F32 = jnp.float32
BF16 = jnp.bfloat16

D_MODEL = 1024
HEAD_DIM = 64
N_Q_HEADS = 16
N_KV_HEADS = 4
GQA_GROUP = 4
WINDOW = 128
BLOCK = 128
ROT_DIM = 16
ROPE_THETA = 500000.0
LRU_BLOCKS = 4
LRU_BLOCK_W = 256
CONV_WIDTH = 4
LRU_C = 8.0
N_EXPERTS = 16
EXPERT_FF = 1024
CAPACITY_FACTOR = 2
LN_EPS = 1e-5
DEPTH = 2
ALPHA = (2 * DEPTH) ** 0.25

LANES = 128
SUBLANES = 8
MXU_COLS = 256
ROW_TILE = 1024
TOK_ROWS = D_MODEL // LANES
VMEM_LIMIT_MOE = 61 * 1024 * 1024


def _layer_norm(v, g, b):
    mu = jnp.mean(v, axis=-1, keepdims=True)
    vc = v - mu
    var = jnp.mean(vc * vc, axis=-1, keepdims=True)
    return vc * lax.rsqrt(var + LN_EPS) * g + b


def _store_token_layout(dst_ref, val):
    n = val.shape[0]
    for s in range(TOK_ROWS):
        dst_ref[pl.ds(s, n, stride=TOK_ROWS), :] = val[:, s * LANES:(s + 1) * LANES]


def _load_token_layout(src_ref, first_token, n):
    base = first_token * TOK_ROWS
    return jnp.concatenate(
        [src_ref[pl.ds(base + s, n, stride=TOK_ROWS), :] for s in range(TOK_ROWS)], axis=1)


def _router_affinity_t(xn, wr_t):
    logits_t = lax.dot_general(wr_t, xn.astype(BF16), (((1,), (1,)), ((), ())),
                               preferred_element_type=F32)
    m = jnp.max(logits_t, axis=0, keepdims=True)
    p = jnp.exp(logits_t - m)
    return p / jnp.sum(p, axis=0, keepdims=True)


def _dot_by_columns(lhs, w_ref, first_col, n_cols):
    return [jnp.dot(lhs, w_ref[:, c:c + MXU_COLS], preferred_element_type=F32)
            for c in range(first_col, first_col + n_cols, MXU_COLS)]


def _expert_weight_cast(w_gate_up, w_down, layer, n_steps, step_of):
    e, d, f2 = w_gate_up.shape[1:]
    f, d2 = w_down.shape[2:]
    if n_steps <= e:
        per = e // n_steps
        in_maps = lambda *g: (layer, step_of(*g), 0, 0)
        out_maps = lambda *g: (step_of(*g), 0, 0)
        blocks = [(per, d, f2), (per, f, d2)]
    else:
        k = n_steps // e
        in_maps = lambda *g: (layer, step_of(*g) // k, step_of(*g) % k, 0)
        out_maps = lambda *g: (step_of(*g) // k, step_of(*g) % k, 0)
        blocks = [(1, d // k, f2), (1, f // k, d2)]
    in_specs = [pl.BlockSpec((1,) + blk, in_maps) for blk in blocks]
    out_specs = [pl.BlockSpec(blk, out_maps) for blk in blocks]
    out_shape = [jax.ShapeDtypeStruct((e, d, f2), BF16), jax.ShapeDtypeStruct((e, f, d2), BF16)]
    return [w_gate_up, w_down], in_specs, out_specs, out_shape


def _cast_blocks(wgu_in, wd_in, wgu_out, wd_out):
    wgu_out[...] = wgu_in[0].astype(BF16)
    wd_out[...] = wd_in[0].astype(BF16)


LOG2E = 1.4426950408889634


def _qkv_kernel(x_ref, w_ref, cos_ref, sin_ref, q_ref, kv_ref):
    x = x_ref[...].astype(BF16)
    acc = jnp.dot(x, w_ref[...], preferred_element_type=F32)
    cosb = cos_ref[...]
    sinb = sin_ref[...]
    lane = lax.broadcasted_iota(jnp.int32, cosb.shape, 1)
    low = (lane % HEAD_DIM) < (ROT_DIM // 2)
    nq = N_Q_HEADS * HEAD_DIM
    nk = N_KV_HEADS * HEAD_DIM

    def rot(t):
        up = pltpu.roll(t, LANES - ROT_DIM // 2, axis=1)
        dn = pltpu.roll(t, ROT_DIM // 2, axis=1)
        return t * cosb + jnp.where(low, up, dn) * sinb

    qs = [rot(acc[:, c * LANES:(c + 1) * LANES]) * (LOG2E * HEAD_DIM ** -0.5) for c in range(nq // LANES)]
    q_ref[...] = jnp.concatenate(qs, axis=1).astype(BF16)
    ks = [rot(acc[:, nq + c * LANES:nq + (c + 1) * LANES]) for c in range(nk // LANES)]
    vs = [acc[:, nq + nk + c * LANES:nq + nk + (c + 1) * LANES] for c in range(nk // LANES)]
    swap = lambda t: pltpu.roll(t, HEAD_DIM, axis=1)
    kv_ref[...] = jnp.concatenate(
        ks + [swap(t) for t in ks] + vs + [swap(t) for t in vs], axis=1).astype(BF16)


def _qkv_proj(x2d, w_qkv, cos_t, sin_t, seq):
    T = x2d.shape[0]
    tm = min(ROW_TILE, seq)
    nq = N_Q_HEADS * HEAD_DIM
    nk = N_KV_HEADS * HEAD_DIM
    nts = seq // tm
    return pl.pallas_call(
        _qkv_kernel,
        grid=(T // tm,),
        in_specs=[
            pl.BlockSpec((tm, D_MODEL), lambda i: (i, 0)),
            pl.BlockSpec((D_MODEL, nq + 2 * nk), lambda i: (0, 0)),
            pl.BlockSpec((tm, LANES), lambda i: (i % nts, 0)),
            pl.BlockSpec((tm, LANES), lambda i: (i % nts, 0)),
        ],
        out_specs=[
            pl.BlockSpec((tm, nq), lambda i: (i, 0)),
            pl.BlockSpec((tm, 4 * nk), lambda i: (i, 0)),
        ],
        out_shape=[
            jax.ShapeDtypeStruct((T, nq), BF16),
            jax.ShapeDtypeStruct((T, 4 * nk), BF16),
        ],
        compiler_params=pltpu.CompilerParams(dimension_semantics=("arbitrary",)),
        name="qkv_proj",
    )(x2d, w_qkv, cos_t, sin_t)


QBLKS = 8


def _attn_kernel(sink_ref, q_ref, kvp_ref, kvc_ref, kvn_ref, wgu_in, wd_in, o_ref, wgu_out, wd_out, *, seq):
    _cast_blocks(wgu_in, wd_in, wgu_out, wd_out)
    i = pl.program_id(1)
    nk = N_KV_HEADS * HEAD_DIM
    kv = jnp.concatenate([kvp_ref[...], kvc_ref[...], kvn_ref[...]], axis=0)
    band = 3 * BLOCK
    lo = lax.broadcasted_iota(jnp.int32, (1, LANES), 1) < HEAD_DIM
    zero = jnp.zeros((), BF16)
    k_lo, k_hi, v_dup_t = [], [], []
    for h in range(N_KV_HEADS):
        grp, half = h // 2, h % 2

        def block(base, swapped):
            col = base + (nk if swapped else 0) + grp * LANES
            return kv[:, col:col + LANES]
        k_lo.append(jnp.where(lo, block(0, half == 1), zero))
        k_hi.append(jnp.where(lo, zero, block(0, half == 0)))
        v_dup = jnp.where(lo, block(2 * nk, half == 1), block(2 * nk, half == 0))
        v_dup_t.append(v_dup.astype(F32).T.astype(BF16))

    key = lax.broadcasted_iota(jnp.int32, (BLOCK, BLOCK), 0)
    qry = lax.broadcasted_iota(jnp.int32, (BLOCK, BLOCK), 1)
    feat_lo = lax.broadcasted_iota(jnp.int32, (LANES, 1), 0) < HEAD_DIM
    units = [(j, h) for j in range(QBLKS) for h in range(N_KV_HEADS)]

    def scores_t(j, h):
        rows = slice(j * BLOCK, (j + 1) * BLOCK)
        qp = jnp.concatenate([q_ref[rows, (2 * h) * LANES:(2 * h + 1) * LANES],
                              q_ref[rows, (2 * h + 1) * LANES:(2 * h + 2) * LANES]], axis=0)
        keys = jnp.concatenate([k_lo[h][j * BLOCK:j * BLOCK + band], k_hi[h][j * BLOCK:j * BLOCK + band]],
                               axis=0)
        return lax.dot_general(keys, qp, (((1,), (1,)), ((), ())), preferred_element_type=F32)

    s_next = scores_t(*units[0])
    for u, (j, h) in enumerate(units):
        s = s_next
        if u + 1 < len(units):
            s_next = scores_t(*units[u + 1])
        n = i * QBLKS + j
        ok_prev = jnp.concatenate([(key >= qry) & (n > 0)] * 2, axis=1)
        ok_next = jnp.concatenate([(key <= qry) & (n < seq // BLOCK - 1)] * 2, axis=1)
        vt = v_dup_t[h][:, j * BLOCK:j * BLOCK + band]
        halves = []
        for par in range(2):
            sp = s[par * band:(par + 1) * band, :]
            sp = jnp.concatenate([jnp.where(ok_prev, sp[:BLOCK], -jnp.inf), sp[BLOCK:2 * BLOCK],
                                  jnp.where(ok_next, sp[2 * BLOCK:], -jnp.inf)], axis=0)
            sink_row = jnp.concatenate(
                [jnp.full((1, BLOCK), sink_ref[h * GQA_GROUP + g] * LOG2E, F32) for g in (par, par + 2)],
                axis=1)
            m = jnp.maximum(jnp.max(sp, axis=0, keepdims=True), sink_row)
            p = jnp.exp2(sp - m)
            denom = jnp.sum(p, axis=0, keepdims=True) + jnp.exp2(sink_row - m)
            halves.append(jnp.dot(vt, p.astype(BF16), preferred_element_type=F32) / denom)
        out_t = jnp.where(feat_lo, halves[0], halves[1])
        o_ref[j * BLOCK:(j + 1) * BLOCK, (2 * h) * LANES:(2 * h + 2) * LANES] = jnp.concatenate(
            [out_t[:, :BLOCK].T, out_t[:, BLOCK:].T], axis=1).astype(BF16)


def _attention(q, kv, sink, w_gate_up, w_down, layer, batch, seq):
    T = q.shape[0]
    qt = QBLKS * BLOCK
    nq = seq // qt
    nb = seq // BLOCK
    wkv = kv.shape[1]
    cur = lambda b, i: (b * nq + i, 0)
    prev = lambda b, i: (b * nb + jnp.maximum(i * QBLKS - 1, 0), 0)
    nxt = lambda b, i: (b * nb + jnp.minimum(i * QBLKS + QBLKS, nb - 1), 0)
    w_ops, w_in, w_out, w_shape = _expert_weight_cast(w_gate_up, w_down, layer, batch * nq,
                                                      lambda b, i: b * nq + i)
    return pl.pallas_call(
        functools.partial(_attn_kernel, seq=seq),
        grid=(batch, nq),
        in_specs=[
            pl.BlockSpec(memory_space=pltpu.SMEM),
            pl.BlockSpec((qt, D_MODEL), cur),
            pl.BlockSpec((BLOCK, wkv), prev),
            pl.BlockSpec((qt, wkv), cur),
            pl.BlockSpec((BLOCK, wkv), nxt),
        ] + w_in,
        out_specs=[pl.BlockSpec((qt, D_MODEL), cur)] + w_out,
        out_shape=[jax.ShapeDtypeStruct((T, D_MODEL), BF16)] + w_shape,
        compiler_params=pltpu.CompilerParams(dimension_semantics=("arbitrary", "arbitrary")),
        name="swa_attention",
    )(sink, q, kv, kv, kv, *w_ops)


def _proj_ln_kernel(a_ref, w_ref, x_ref, g_ref, b_ref, wr_ref, xo_ref, aff_ref):
    y = jnp.dot(a_ref[...], w_ref[...], preferred_element_type=F32)
    xn = _layer_norm(ALPHA * x_ref[...] + y, g_ref[...], b_ref[...])
    _store_token_layout(xo_ref, xn)
    aff_ref[0] = _router_affinity_t(xn, wr_ref[...])


def _proj_ln(a, w, x2d, g, b, wr_t, batch, seq):
    T = x2d.shape[0]
    tm = min(ROW_TILE, seq)
    nts = seq // tm
    return pl.pallas_call(
        _proj_ln_kernel,
        grid=(T // tm,),
        in_specs=[
            pl.BlockSpec((tm, D_MODEL), lambda i: (i, 0)),
            pl.BlockSpec((D_MODEL, D_MODEL), lambda i: (0, 0)),
            pl.BlockSpec((tm, D_MODEL), lambda i: (i, 0)),
            pl.BlockSpec((1, D_MODEL), lambda i: (0, 0)),
            pl.BlockSpec((1, D_MODEL), lambda i: (0, 0)),
            pl.BlockSpec((N_EXPERTS, D_MODEL), lambda i: (0, 0)),
        ],
        out_specs=[
            pl.BlockSpec((tm * TOK_ROWS, LANES), lambda i: (i, 0)),
            pl.BlockSpec((1, N_EXPERTS, tm), lambda i: (i // nts, 0, i % nts)),
        ],
        out_shape=[
            jax.ShapeDtypeStruct((T * TOK_ROWS, LANES), F32),
            jax.ShapeDtypeStruct((batch, N_EXPERTS, seq), F32),
        ],
        compiler_params=pltpu.CompilerParams(dimension_semantics=("arbitrary",)),
        name="proj_ln_router",
    )(a, w, x2d, g, b, wr_t)


SLOT_BLOCK = 64


def _lane_prefix(mask, tri, dst_ref):
    seq = mask.shape[1]
    off = jnp.zeros((mask.shape[0], 1), F32)
    mb = mask.astype(BF16)
    ends = []
    for c in range(seq // LANES):
        res = jnp.dot(mb[:, c * LANES:(c + 1) * LANES], tri, preferred_element_type=F32) + off
        dst_ref[:, c * LANES:(c + 1) * LANES] = res
        off = res[:, LANES - 1:LANES]
        ends.append(off)
    return ends


def _select_kernel(aff_ref, pos_ref, bnd_ref, *, cap):
    aff = aff_ref[0]
    as_float = lambda bits: pltpu.bitcast(bits, F32)
    rows = aff.shape[0]
    cur = jnp.zeros((rows, 1), jnp.int32)
    for bit in range(30, -1, -1):
        cand = cur | (1 << bit)
        cnt = jnp.sum((aff >= as_float(cand)).astype(jnp.int32), axis=1, keepdims=True)
        cur = jnp.where(cnt >= cap, cand, cur)
    gt = aff >= as_float(cur + 1)
    eq = (aff >= as_float(cur)) & jnp.logical_not(gt)
    n_gt = jnp.sum(gt.astype(jnp.int32), axis=1, keepdims=True)
    ties_taken = (cap - n_gt).astype(F32)
    ri = lax.broadcasted_iota(jnp.int32, (LANES, LANES), 0)
    ci = lax.broadcasted_iota(jnp.int32, (LANES, LANES), 1)
    tri = (ri <= ci).astype(BF16)
    _lane_prefix(eq, tri, pos_ref.at[0])
    sel = gt | (eq & (pos_ref[0] <= ties_taken))
    ends = _lane_prefix(sel, tri, pos_ref.at[0])
    pos_ref[0] = jnp.where(sel, pos_ref[0], 0.0)
    cols = []
    for sb in range(cap // SLOT_BLOCK):
        for target in (sb * SLOT_BLOCK + 1, (sb + 1) * SLOT_BLOCK):
            blk = jnp.zeros((rows, 1), jnp.int32)
            for end in ends[:-1]:
                blk = blk + (end < target).astype(jnp.int32)
            cols.append(blk)
    bnd_ref[0] = jnp.concatenate(cols, axis=1)


def _compact_kernel(bnd_ref, pos_ref, aff_ref, idx_ref, gate_ref, ptok, pgate, *, cap):
    b = pl.program_id(0)
    n_sb = cap // SLOT_BLOCK
    groups = SLOT_BLOCK // SUBLANES
    lane = lax.broadcasted_iota(jnp.int32, (SUBLANES, LANES), 1).astype(F32)
    sub1 = (lax.broadcasted_iota(jnp.int32, (SUBLANES, 1), 0) + 1).astype(F32)
    lane_e = lax.broadcasted_iota(jnp.int32, (cap, N_EXPERTS), 1)
    idx_ref[0] = jnp.zeros((cap, N_EXPERTS), jnp.int32)
    gate_ref[0] = jnp.zeros((cap, N_EXPERTS), F32)

    def expert(e, carry):
        def slot_block(sb, carry2):
            base = ((b * N_EXPERTS + e) * n_sb + sb) * 2
            slot0 = jnp.asarray(sb * SLOT_BLOCK, F32)

            def lane_block(c, accs):
                off = pl.multiple_of(c * LANES, LANES)
                p = jnp.broadcast_to(pos_ref[0, e:e + 1, pl.ds(off, LANES)], (SUBLANES, LANES))
                a = jnp.broadcast_to(aff_ref[0, e:e + 1, pl.ds(off, LANES)], (SUBLANES, LANES))
                tokv = lane + jnp.asarray(c * LANES, F32)
                out = []
                for g in range(groups):
                    hit = p == (slot0 + (g * SUBLANES) + sub1)
                    out.append(jnp.where(hit, tokv, accs[2 * g]))
                    out.append(jnp.where(hit, a, accs[2 * g + 1]))
                return tuple(out)

            zeros = tuple(jnp.zeros((SUBLANES, LANES), F32) for _ in range(2 * groups))
            accs = lax.fori_loop(bnd_ref[base], bnd_ref[base + 1] + 1, lane_block, zeros)
            for g in range(groups):
                rows = pl.ds(pl.multiple_of(sb * SLOT_BLOCK + g * SUBLANES, SUBLANES), SUBLANES)
                ptok[rows, :] = accs[2 * g]
                pgate[rows, :] = accs[2 * g + 1]
            return carry2

        lax.fori_loop(0, n_sb, slot_block, 0)
        tok_col = jnp.sum(ptok[...], axis=1, keepdims=True).astype(jnp.int32)
        gate_col = jnp.sum(pgate[...], axis=1, keepdims=True)
        idx_ref[0] = jnp.where(lane_e == e, tok_col, idx_ref[0])
        gate_ref[0] = jnp.where(lane_e == e, gate_col, gate_ref[0])
        return carry

    for e in range(N_EXPERTS):
        expert(e, 0)


def _route(aff_t, cap):
    batch, _, seq = aff_t.shape
    n_bnd = 2 * (cap // SLOT_BLOCK)
    seq_spec = pl.BlockSpec((1, N_EXPERTS, seq), lambda b, *_: (b, 0, 0))
    rows = batch * N_EXPERTS
    pos, bnd = pl.pallas_call(
        functools.partial(_select_kernel, cap=cap),
        grid=(1,),
        in_specs=[pl.BlockSpec((1, rows, seq), lambda i: (0, 0, 0))],
        out_specs=[pl.BlockSpec((1, rows, seq), lambda i: (0, 0, 0)),
                   pl.BlockSpec((1, rows, n_bnd), lambda i: (0, 0, 0))],
        out_shape=[
            jax.ShapeDtypeStruct((1, rows, seq), F32),
            jax.ShapeDtypeStruct((1, rows, n_bnd), jnp.int32),
        ],
        compiler_params=pltpu.CompilerParams(dimension_semantics=("arbitrary",)),
        name="expert_choice_select",
    )(aff_t.reshape(1, rows, seq))
    pos = pos.reshape(batch, N_EXPERTS, seq)
    slot_spec = pl.BlockSpec((1, cap, N_EXPERTS), lambda b, *_: (b, 0, 0))
    idx_t, gate_t = pl.pallas_call(
        functools.partial(_compact_kernel, cap=cap),
        grid_spec=pltpu.PrefetchScalarGridSpec(
            num_scalar_prefetch=1,
            grid=(batch,),
            in_specs=[seq_spec, seq_spec],
            out_specs=[slot_spec, slot_spec],
            scratch_shapes=[pltpu.VMEM((cap, LANES), F32), pltpu.VMEM((cap, LANES), F32)],
        ),
        out_shape=[
            jax.ShapeDtypeStruct((batch, cap, N_EXPERTS), jnp.int32),
            jax.ShapeDtypeStruct((batch, cap, N_EXPERTS), F32),
        ],
        compiler_params=pltpu.CompilerParams(dimension_semantics=("arbitrary",)),
        name="expert_choice_compact",
    )(bnd.reshape(-1), pos, aff_t)
    return jnp.swapaxes(idx_t, 1, 2), gate_t


LN_CHUNK = 256
SCATTER_GROUP = 2


def _moe_kernel(idx_ref, x1t_hbm, gate_ref, wgu_ref, wd_ref, lng_ref, lnb_ref, out_hbm,
                x1v, acc, xs3, y3, xs2d, yacc, stage, sem_in, sem_out, *, seq, cap):
    b = pl.program_id(0)
    e = pl.program_id(1)
    tok_rows = seq * TOK_ROWS
    last_e = N_EXPERTS - 1

    def tok_slice(t):
        return pl.ds(pl.multiple_of(t * TOK_ROWS, TOK_ROWS), TOK_ROWS)

    def gather_rows(expert):
        base = (b * N_EXPERTS + expert) * cap
        for i in range(cap):
            xs3[i * TOK_ROWS:(i + 1) * TOK_ROWS, :] = x1v[tok_slice(idx_ref[base + i]), :]

    def scatter_rows(expert):
        base = (b * N_EXPERTS + expert) * cap
        for g0 in range(0, cap, SCATTER_GROUP):
            toks = [idx_ref[base + g0 + k] for k in range(SCATTER_GROUP)]
            vals = [acc[tok_slice(toks[k]), :] + y3[(g0 + k) * TOK_ROWS:(g0 + k + 1) * TOK_ROWS, :]
                    for k in range(SCATTER_GROUP)]
            for k in range(SCATTER_GROUP):
                acc[tok_slice(toks[k]), :] = vals[k]

    def load_sequence(seq_idx):
        return pltpu.make_async_copy(x1t_hbm.at[pl.ds(seq_idx * tok_rows, tok_rows)], x1v, sem_in)

    def init_accumulator():
        acc[...] = ALPHA * x1v[...]

    @pl.when((b == 0) & (e == 0))
    def _first_sequence():
        load_sequence(0).start()
        load_sequence(0).wait()
        init_accumulator()

    @pl.when(e == 0)
    def _start_sequence():
        yacc[...] = jnp.zeros_like(yacc)

        def body(i, carry):
            xs3[tok_slice(i), :] = x1v[tok_slice(idx_ref[b * N_EXPERTS * cap + i]), :]
            return carry
        lax.fori_loop(0, cap, body, 0, unroll=8)
        xs2d[0] = _load_token_layout(xs3, 0, cap).astype(BF16)

    def gate_column(expert):
        gates = gate_ref[0]
        lane_e = lax.broadcasted_iota(jnp.int32, gates.shape, 1)
        return jnp.sum(jnp.where(lane_e == expert, gates, 0.0), axis=1, keepdims=True)

    prev = jnp.maximum(e - 1, 0)
    _store_token_layout(y3, yacc[...] * gate_column(prev))
    scatter_rows(prev)
    gather_rows(jnp.minimum(e + 1, last_e))
    gu = jnp.dot(xs2d[e % 2], wgu_ref[0], preferred_element_type=F32)
    gate_h, up_h = gu[:, :EXPERT_FF], gu[:, EXPERT_FF:]
    act = (gate_h * jax.nn.sigmoid(gate_h) * up_h).astype(BF16)
    yacc[...] = jnp.dot(act, wd_ref[0], preferred_element_type=F32)
    xs2d[(e + 1) % 2] = _load_token_layout(xs3, 0, cap).astype(BF16)

    @pl.when(e == last_e)
    def _finalize():
        _store_token_layout(y3, yacc[...] * gate_column(last_e))

        def body(gi, carry):
            base = (b * N_EXPERTS + last_e) * cap + gi * SUBLANES
            toks = [idx_ref[base + k] for k in range(SUBLANES)]
            vals = [acc[tok_slice(toks[k]), :] + y3[tok_slice(gi * SUBLANES + k), :]
                    for k in range(SUBLANES)]
            for k in range(SUBLANES):
                acc[tok_slice(toks[k]), :] = vals[k]
            return carry
        lax.fori_loop(0, cap // SUBLANES, body, 0)

        n_seq = pl.num_programs(0)

        @pl.when(b + 1 < n_seq)
        def _():
            load_sequence(b + 1).start()

        n_chunks = seq // LN_CHUNK

        def out_copy(c, slot):
            return pltpu.make_async_copy(
                stage.at[slot], out_hbm.at[pl.ds(b * seq + c * LN_CHUNK, LN_CHUNK)], sem_out.at[slot])

        def ln_body(c, carry):
            slot = c % 2
            first = c * LN_CHUNK
            res = _layer_norm(_load_token_layout(acc, first, LN_CHUNK), lng_ref[...], lnb_ref[...])

            @pl.when(c >= 2)
            def _():
                out_copy(c - 2, slot).wait()
            stage[slot] = res
            out_copy(c, slot).start()
            return carry
        lax.fori_loop(0, n_chunks, ln_body, 0)
        out_copy(n_chunks - 2, (n_chunks - 2) % 2).wait()
        out_copy(n_chunks - 1, (n_chunks - 1) % 2).wait()

        @pl.when(b + 1 < n_seq)
        def _():
            load_sequence(b + 1).wait()
            init_accumulator()


def _moe(x1t, idx, gate_t, w_gate_up, w_down, ln_g, ln_b, batch, seq):
    cap = idx.shape[2]
    T = batch * seq
    idx_flat = idx.reshape(-1)
    grid_spec = pltpu.PrefetchScalarGridSpec(
        num_scalar_prefetch=1,
        grid=(batch, N_EXPERTS),
        in_specs=[
            pl.BlockSpec(memory_space=pl.ANY),
            pl.BlockSpec((1, cap, N_EXPERTS), lambda b, e, idx: (b, 0, 0)),
            pl.BlockSpec((1, D_MODEL, 2 * EXPERT_FF), lambda b, e, idx: (e, 0, 0)),
            pl.BlockSpec((1, EXPERT_FF, D_MODEL), lambda b, e, idx: (e, 0, 0)),
            pl.BlockSpec((1, D_MODEL), lambda b, e, idx: (0, 0)),
            pl.BlockSpec((1, D_MODEL), lambda b, e, idx: (0, 0)),
        ],
        out_specs=pl.BlockSpec(memory_space=pl.ANY),
        scratch_shapes=[
            pltpu.VMEM((seq * TOK_ROWS, LANES), F32),
            pltpu.VMEM((seq * TOK_ROWS, LANES), F32),
            pltpu.VMEM((cap * TOK_ROWS, LANES), F32),
            pltpu.VMEM((cap * TOK_ROWS, LANES), F32),
            pltpu.VMEM((2, cap, D_MODEL), BF16),
            pltpu.VMEM((cap, D_MODEL), F32),
            pltpu.VMEM((2, LN_CHUNK, D_MODEL), F32),
            pltpu.SemaphoreType.DMA(()),
            pltpu.SemaphoreType.DMA((2,)),
        ],
    )
    return pl.pallas_call(
        functools.partial(_moe_kernel, seq=seq, cap=cap),
        grid_spec=grid_spec,
        out_shape=jax.ShapeDtypeStruct((T, D_MODEL), F32),
        compiler_params=pltpu.CompilerParams(
            dimension_semantics=("arbitrary", "arbitrary"),
            vmem_limit_bytes=VMEM_LIMIT_MOE),
        name="expert_ffn_combine_ln",
    )(idx_flat, x1t, gate_t, w_gate_up, w_down, ln_g, ln_b)


HALO = SUBLANES


def _gelu_tanh(x):
    return 0.5 * x * (1.0 + jnp.tanh(0.7978845608028654 * (x + 0.044715 * (x * x * x))))


def _lru_in_kernel(x_ref, xp_ref, xn_ref, w_ref, cw_ref, cb_ref, gg_ref, xc_ref, *, nts):
    i = pl.program_id(0)
    tm = x_ref.shape[0]
    x = x_ref[...].astype(BF16)
    xp = xp_ref[...].astype(BF16)
    xn = xn_ref[...].astype(BF16)
    first = (i % nts) == 0
    last = (i % nts) == nts - 1
    for k, gate in enumerate(_dot_by_columns(x, w_ref, 0, D_MODEL)):
        gg_ref[:, k * MXU_COLS:(k + 1) * MXU_COLS] = _gelu_tanh(gate)
    n_ext = tm + 2 * HALO
    for k in range(D_MODEL // MXU_COLS):
        cols = slice(k * MXU_COLS, (k + 1) * MXU_COLS)
        xb, = _dot_by_columns(x, w_ref, D_MODEL + k * MXU_COLS, MXU_COLS)
        xbp, = _dot_by_columns(xp, w_ref, D_MODEL + k * MXU_COLS, MXU_COLS)
        xbn, = _dot_by_columns(xn, w_ref, D_MODEL + k * MXU_COLS, MXU_COLS)
        ext = jnp.concatenate([jnp.where(first, 0.0, xbp), xb, jnp.where(last, 0.0, xbn)], axis=0)
        xc = cb_ref[:, cols]
        for tap in range(CONV_WIDTH):
            shift = (2 - tap) % n_ext
            src = ext if shift == 0 else pltpu.roll(ext, shift, axis=0)
            xc = xc + cw_ref[tap:tap + 1, cols] * src[HALO:HALO + tm, :]
        xc_ref[:, cols] = xc


def _lru_in(x2d, w_in, conv_w, conv_b, seq):
    T = x2d.shape[0]
    tm = min(ROW_TILE, seq)
    nts = seq // tm
    hb = tm // HALO
    nblk = T // HALO
    return pl.pallas_call(
        functools.partial(_lru_in_kernel, nts=nts),
        grid=(T // tm,),
        in_specs=[
            pl.BlockSpec((tm, D_MODEL), lambda i: (i, 0)),
            pl.BlockSpec((HALO, D_MODEL), lambda i: (jnp.maximum(i * hb - 1, 0), 0)),
            pl.BlockSpec((HALO, D_MODEL), lambda i: (jnp.minimum((i + 1) * hb, nblk - 1), 0)),
            pl.BlockSpec((D_MODEL, 2 * D_MODEL), lambda i: (0, 0)),
            pl.BlockSpec((CONV_WIDTH, D_MODEL), lambda i: (0, 0)),
            pl.BlockSpec((1, D_MODEL), lambda i: (0, 0)),
        ],
        out_specs=[
            pl.BlockSpec((tm, D_MODEL), lambda i: (i, 0)),
            pl.BlockSpec((tm, D_MODEL), lambda i: (i, 0)),
        ],
        out_shape=[
            jax.ShapeDtypeStruct((T, D_MODEL), F32),
            jax.ShapeDtypeStruct((T, D_MODEL), F32),
        ],
        compiler_params=pltpu.CompilerParams(dimension_semantics=("arbitrary",)),
        name="lru_in_conv",
    )(x2d, x2d, x2d, w_in, conv_w, conv_b)


SCAN_TM = 128
CHAIN_PITCH = SCAN_TM + SUBLANES
PAIRS = D_MODEL // (2 * LANES)


def _lru_chain_scan(xc_ref, wg_ref, br_ref, bi_ref, lam_ref, carry, abuf, ubuf, hbuf, *, reverse, seq):
    i = pl.program_id(0)
    nseq, tm = xc_ref.shape[0], xc_ref.shape[1]
    assert 2 * nseq == SUBLANES and tm == SCAN_TM
    nt = seq // tm
    tile = (nt - 1 - i) if reverse else i

    @pl.when(i == 0)
    def _():
        carry[...] = jnp.zeros_like(carry)

    half_nsp = (-0.5 * LRU_C) * jax.nn.softplus(-lam_ref[...])
    row = lax.broadcasted_iota(jnp.int32, (tm, 1), 0)
    is_start = (tile * tm + row) == (seq - 1 if reverse else 0)
    xc_all = jnp.concatenate([xc_ref[b] for b in range(nseq)], axis=0)
    xcb = xc_all.astype(BF16)
    r_parts, i_parts = [], []
    for n in range(LRU_BLOCKS):
        res = jnp.dot(xcb[:, n * LRU_BLOCK_W:(n + 1) * LRU_BLOCK_W], wg_ref[n],
                      preferred_element_type=F32)
        r_parts.append(res[:, :LRU_BLOCK_W])
        i_parts.append(res[:, LRU_BLOCK_W:])
    r_pre = jnp.concatenate(r_parts, axis=1)
    i_pre = jnp.concatenate(i_parts, axis=1)
    for b in range(nseq):
        rows_b = slice(b * tm, (b + 1) * tm)
        xc = xc_all[rows_b]
        tr = jnp.tanh(r_pre[rows_b] + br_ref[...])
        ig = 0.5 * jnp.tanh(i_pre[rows_b] + bi_ref[...]) + 0.5
        log_a = tr * half_nsp + half_nsp
        a = jnp.exp(log_a)
        z = -jnp.tanh(log_a) * (1.0 + a * a)
        root = jnp.where(z > 0.0, z * lax.rsqrt(z), 0.0)
        u = jnp.where(is_start, 1.0, root) * ig * xc
        for lb in range(2 * PAIRS):
            rows = slice((2 * b + lb % 2) * CHAIN_PITCH, (2 * b + lb % 2) * CHAIN_PITCH + tm)
            abuf[lb // 2, rows, :] = a[:, lb * LANES:(lb + 1) * LANES]
            ubuf[lb // 2, rows, :] = u[:, lb * LANES:(lb + 1) * LANES]

    def step(k, hs):
        t = (tm - 1 - k) if reverse else k
        chains = pl.ds(t, SUBLANES, stride=CHAIN_PITCH)
        out = []
        for m in range(PAIRS):
            h = abuf[m, chains, :] * hs[m] + ubuf[m, chains, :]
            hbuf[m, chains, :] = h
            out.append(h)
        return tuple(out)

    hs = lax.fori_loop(0, tm, step, tuple(carry[m] for m in range(PAIRS)), unroll=8)
    for m in range(PAIRS):
        carry[m] = hs[m]


def _chain_tile(hbuf, b, tm):
    return jnp.concatenate(
        [hbuf[lb // 2, (2 * b + lb % 2) * CHAIN_PITCH:(2 * b + lb % 2) * CHAIN_PITCH + tm, :]
         for lb in range(2 * PAIRS)], axis=1)


def _lru_fwd_kernel(xc_ref, wg_ref, br_ref, bi_ref, lam_ref, wgu_in, wd_in, h_ref, wgu_out, wd_out,
                    carry, abuf, ubuf, hbuf, *, seq):
    _cast_blocks(wgu_in, wd_in, wgu_out, wd_out)
    _lru_chain_scan(xc_ref, wg_ref, br_ref, bi_ref, lam_ref, carry, abuf, ubuf, hbuf,
                    reverse=False, seq=seq)
    for b in range(h_ref.shape[0]):
        h_ref[b] = _chain_tile(hbuf, b, h_ref.shape[1])


def _lru_bwd_kernel(xc_ref, wg_ref, br_ref, bi_ref, lam_ref, hf_ref, gg_ref, x_ref, wo_ref,
                    g_ref, b_ref, wr_ref, xo_ref, aff_ref, carry, abuf, ubuf, hbuf, *, seq):
    _lru_chain_scan(xc_ref, wg_ref, br_ref, bi_ref, lam_ref, carry, abuf, ubuf, hbuf,
                    reverse=True, seq=seq)
    nseq, tm = x_ref.shape[0], x_ref.shape[1]
    gated = [((hf_ref[b] + _chain_tile(hbuf, b, tm)) * gg_ref[b]).astype(BF16) for b in range(nseq)]
    y_all = jnp.dot(jnp.concatenate(gated, axis=0), wo_ref[...], preferred_element_type=F32)
    for b in range(nseq):
        xn = _layer_norm(ALPHA * x_ref[b] + y_all[b * tm:(b + 1) * tm], g_ref[...], b_ref[...])
        _store_token_layout(xo_ref.at[b], xn)
        aff_ref[b] = _router_affinity_t(xn, wr_ref[...])


def _lru_specs(batch, tile_map):
    const2 = lambda i: (0, 0)
    return [
        pl.BlockSpec((batch, SCAN_TM, D_MODEL), tile_map),
        pl.BlockSpec((LRU_BLOCKS, LRU_BLOCK_W, 2 * LRU_BLOCK_W), lambda i: (0, 0, 0)),
        pl.BlockSpec((1, D_MODEL), const2),
        pl.BlockSpec((1, D_MODEL), const2),
        pl.BlockSpec((1, D_MODEL), const2),
    ]


def _lru_scratch():
    buf = pltpu.VMEM((PAIRS, SUBLANES * CHAIN_PITCH, LANES), F32)
    return [pltpu.VMEM((PAIRS, SUBLANES, LANES), F32), buf, buf, buf]


def _lru_forward(xc3, wg, br, bi, lam, w_gate_up, w_down, layer):
    batch, seq, _ = xc3.shape
    tile_map = lambda i: (0, i, 0)
    nt = seq // SCAN_TM
    w_ops, w_in, w_out, w_shape = _expert_weight_cast(w_gate_up, w_down, layer, nt, lambda i: i)
    return pl.pallas_call(
        functools.partial(_lru_fwd_kernel, seq=seq),
        grid=(nt,),
        in_specs=_lru_specs(batch, tile_map) + w_in,
        out_specs=[pl.BlockSpec((batch, SCAN_TM, D_MODEL), tile_map)] + w_out,
        out_shape=[jax.ShapeDtypeStruct((batch, seq, D_MODEL), F32)] + w_shape,
        scratch_shapes=_lru_scratch(),
        compiler_params=pltpu.CompilerParams(dimension_semantics=("arbitrary",)),
        name="lru_forward_scan",
    )(xc3, wg, br, bi, lam, *w_ops)


def _lru_backward_out(xc3, wg, br, bi, lam, hf3, gg3, x3d, w_out, g, b, wr_t):
    batch, seq, _ = xc3.shape
    nt = seq // SCAN_TM
    tile_map = lambda i: (0, nt - 1 - i, 0)
    const2 = lambda i: (0, 0)
    tile = pl.BlockSpec((batch, SCAN_TM, D_MODEL), tile_map)
    return pl.pallas_call(
        functools.partial(_lru_bwd_kernel, seq=seq),
        grid=(nt,),
        in_specs=_lru_specs(batch, tile_map) + [
            tile, tile, tile,
            pl.BlockSpec((D_MODEL, D_MODEL), const2),
            pl.BlockSpec((1, D_MODEL), const2),
            pl.BlockSpec((1, D_MODEL), const2),
            pl.BlockSpec((N_EXPERTS, D_MODEL), const2),
        ],
        out_specs=[
            pl.BlockSpec((batch, SCAN_TM * TOK_ROWS, LANES), tile_map),
            pl.BlockSpec((batch, N_EXPERTS, SCAN_TM), lambda i: (0, 0, nt - 1 - i)),
        ],
        out_shape=[
            jax.ShapeDtypeStruct((batch, seq * TOK_ROWS, LANES), F32),
            jax.ShapeDtypeStruct((batch, N_EXPERTS, seq), F32),
        ],
        scratch_shapes=_lru_scratch(),
        compiler_params=pltpu.CompilerParams(dimension_semantics=("arbitrary",)),
        name="lru_backward_scan_out_ln_router",
    )(xc3, wg, br, bi, lam, hf3, gg3, x3d, w_out, g, b, wr_t)


def _rotary_tables(seq):
    pos = jnp.arange(seq, dtype=F32)
    inv_freq = ROPE_THETA ** (-jnp.arange(0, ROT_DIM, 2, dtype=F32) / ROT_DIM)
    ang = pos[:, None] * inv_freq[None, :]
    cos, sin = jnp.cos(ang), jnp.sin(ang)
    pad = HEAD_DIM - ROT_DIM
    cos_h = jnp.concatenate([cos, cos, jnp.ones((seq, pad), F32)], axis=1)
    sin_h = jnp.concatenate([-sin, sin, jnp.zeros((seq, pad), F32)], axis=1)
    reps = LANES // HEAD_DIM
    return jnp.tile(cos_h, (1, reps)), jnp.tile(sin_h, (1, reps))


def _moe_block(x1t, aff_t, w_gate_up, w_down, ln_g, ln_b, batch, seq):
    cap = CAPACITY_FACTOR * seq // N_EXPERTS
    idx, gate_t = _route(aff_t, cap)
    return _moe(x1t, idx, gate_t, w_gate_up, w_down, ln_g.reshape(1, -1), ln_b.reshape(1, -1), batch, seq)


def kernel(x, attn_w_qkv, attn_w_o, attn_sink, lru_w_in, lru_conv_w, lru_conv_b, lru_w_rgate,
           lru_b_rgate, lru_w_igate, lru_b_igate, lru_lambda, lru_w_out, moe_w_router, moe_w_gate_up,
           moe_w_down, ln_mix_g, ln_mix_b, ln_ffn_g, ln_ffn_b):
    batch, seq, _ = x.shape
    x2d = x.reshape(batch * seq, D_MODEL)
    row = lambda v: v.reshape(1, -1)

    cos_t, sin_t = _rotary_tables(seq)
    q, kv = _qkv_proj(x2d, attn_w_qkv[0].astype(BF16), cos_t, sin_t, seq)
    o, wgu0, wd0 = _attention(q, kv, attn_sink[0], moe_w_gate_up, moe_w_down, 0, batch, seq)
    x1, aff_t = _proj_ln(o, attn_w_o[0].astype(BF16), x2d, row(ln_mix_g[0]), row(ln_mix_b[0]),
                         moe_w_router[0].T.astype(BF16), batch, seq)
    x2 = _moe_block(x1, aff_t, wgu0, wd0, ln_ffn_g[0], ln_ffn_b[0], batch, seq)

    gg, xc = _lru_in(x2, lru_w_in[0].astype(BF16), lru_conv_w[0], row(lru_conv_b[0]), seq)
    wg = (0.5 * jnp.concatenate([lru_w_rgate[0], lru_w_igate[0]], axis=-1)).astype(BF16)
    half_row = lambda v: 0.5 * v.reshape(1, -1)
    per_seq = lambda t: t.reshape(batch, seq, D_MODEL)
    xc3 = per_seq(xc)
    hf3, wgu1, wd1 = _lru_forward(xc3, wg[0], half_row(lru_b_rgate[0, 0]), half_row(lru_b_igate[0, 0]),
                                  row(lru_lambda[0, 0]), moe_w_gate_up, moe_w_down, 1)
    x3, aff_t = _lru_backward_out(xc3, wg[1], half_row(lru_b_rgate[0, 1]), half_row(lru_b_igate[0, 1]),
                                  row(lru_lambda[0, 1]), hf3, per_seq(gg), per_seq(x2),
                                  lru_w_out[0].astype(BF16), row(ln_mix_g[1]), row(ln_mix_b[1]),
                                  moe_w_router[1].T.astype(BF16))
    x3 = x3.reshape(batch * seq * TOK_ROWS, LANES)
    x4 = _moe_block(x3, aff_t, wgu1, wd1, ln_ffn_g[1], ln_ffn_b[1], batch, seq)
    return x4.reshape(batch, seq, D_MODEL)
```

```python
import functools

import jax
import jax.numpy as jnp
from jax import lax
from jax.experimental import pallas as pl
from jax.experimental.pallas import tpu as pltpu

F32 = jnp.float32
BF16 = jnp.bfloat16

D_MODEL = 1024
HEAD_DIM = 64
N_Q_HEADS = 16
N_KV_HEADS = 4
GQA_GROUP = 4
WINDOW = 128
BLOCK = 128
ROT_DIM = 16
ROPE_THETA = 500000.0
LRU_BLOCKS = 4
LRU_BLOCK_W = 256
CONV_WIDTH = 4
LRU_C = 8.0
N_EXPERTS = 16
EXPERT_FF = 1024
CAPACITY_FACTOR = 2
LN_EPS = 1e-5
DEPTH = 2
ALPHA = (2 * DEPTH) ** 0.25

LANES = 128
SUBLANES = 8
MXU_COLS = 256
ROW_TILE = 1024
TOK_ROWS = D_MODEL // LANES
VMEM_LIMIT_MOE = 61 * 1024 * 1024


def _layer_norm(v, g, b):
    mu = jnp.mean(v, axis=-1, keepdims=True)
    vc = v - mu
    var = jnp.mean(vc * vc, axis=-1, keepdims=True)
    return vc * lax.rsqrt(var + LN_EPS) * g + b


def _store_token_layout(dst_ref, val):
    n = val.shape[0]
    for s in range(TOK_ROWS):
        dst_ref[pl.ds(s, n, stride=TOK_ROWS), :] = val[:, s * LANES:(s + 1) * LANES]


def _load_token_layout(src_ref, first_token, n):
    base = first_token * TOK_ROWS
    return jnp.concatenate(
        [src_ref[pl.ds(base + s, n, stride=TOK_ROWS), :] for s in range(TOK_ROWS)], axis=1)


def _router_affinity_t(xn, wr_t):
    logits_t = lax.dot_general(wr_t, xn.astype(BF16), (((1,), (1,)), ((), ())),
                               preferred_element_type=F32)
    m = jnp.max(logits_t, axis=0, keepdims=True)
    p = jnp.exp(logits_t - m)
    return p / jnp.sum(p, axis=0, keepdims=True)


def _dot_by_columns(lhs, w_ref, first_col, n_cols):
    return [jnp.dot(lhs, w_ref[:, c:c + MXU_COLS], preferred_element_type=F32)
            for c in range(first_col, first_col + n_cols, MXU_COLS)]


def _weight_cast(w, layer, n_steps, step_of):
    e, r, c = w.shape[1:]
    if n_steps <= e:
        blk = (e // n_steps, r, c)
        in_map = lambda *g: (layer, step_of(*g), 0, 0)
        out_map = lambda *g: (step_of(*g), 0, 0)
    else:
        k = n_steps // e
        blk = (1, r // k, c)
        in_map = lambda *g: (layer, step_of(*g) // k, step_of(*g) % k, 0)
        out_map = lambda *g: (step_of(*g) // k, step_of(*g) % k, 0)
    return (pl.BlockSpec((1,) + blk, in_map), pl.BlockSpec(blk, out_map),
            jax.ShapeDtypeStruct((e, r, c), BF16))


LOG2E = 1.4426950408889634


def _qkv_kernel(x_ref, w_ref, cos_ref, sin_ref, q_ref, kv_ref):
    x = x_ref[...].astype(BF16)
    acc = jnp.dot(x, w_ref[...], preferred_element_type=F32)
    cosb = cos_ref[...]
    sinb = sin_ref[...]
    lane = lax.broadcasted_iota(jnp.int32, cosb.shape, 1)
    low = (lane % HEAD_DIM) < (ROT_DIM // 2)
    nq = N_Q_HEADS * HEAD_DIM
    nk = N_KV_HEADS * HEAD_DIM

    def rot(t):
        up = pltpu.roll(t, LANES - ROT_DIM // 2, axis=1)
        dn = pltpu.roll(t, ROT_DIM // 2, axis=1)
        return t * cosb + jnp.where(low, up, dn) * sinb

    qs = [rot(acc[:, c * LANES:(c + 1) * LANES]) * (LOG2E * HEAD_DIM ** -0.5) for c in range(nq // LANES)]
    q_ref[...] = jnp.concatenate(qs, axis=1).astype(BF16)
    ks = [rot(acc[:, nq + c * LANES:nq + (c + 1) * LANES]) for c in range(nk // LANES)]
    vs = [acc[:, nq + nk + c * LANES:nq + nk + (c + 1) * LANES] for c in range(nk // LANES)]
    swap = lambda t: pltpu.roll(t, HEAD_DIM, axis=1)
    kv_ref[...] = jnp.concatenate(
        ks + [swap(t) for t in ks] + vs + [swap(t) for t in vs], axis=1).astype(BF16)


def _qkv_proj(x2d, w_qkv, cos_t, sin_t, seq):
    T = x2d.shape[0]
    tm = min(ROW_TILE, seq)
    nq = N_Q_HEADS * HEAD_DIM
    nk = N_KV_HEADS * HEAD_DIM
    nts = seq // tm
    return pl.pallas_call(
        _qkv_kernel,
        grid=(T // tm,),
        in_specs=[
            pl.BlockSpec((tm, D_MODEL), lambda i: (i, 0)),
            pl.BlockSpec((D_MODEL, nq + 2 * nk), lambda i: (0, 0)),
            pl.BlockSpec((tm, LANES), lambda i: (i % nts, 0)),
            pl.BlockSpec((tm, LANES), lambda i: (i % nts, 0)),
        ],
        out_specs=[
            pl.BlockSpec((tm, nq), lambda i: (i, 0)),
            pl.BlockSpec((tm, 4 * nk), lambda i: (i, 0)),
        ],
        out_shape=[
            jax.ShapeDtypeStruct((T, nq), BF16),
            jax.ShapeDtypeStruct((T, 4 * nk), BF16),
        ],
        compiler_params=pltpu.CompilerParams(dimension_semantics=("arbitrary",)),
        name="qkv_proj",
    )(x2d, w_qkv, cos_t, sin_t)


QBLKS = 8


def _attn_kernel(sink_ref, q_ref, kvp_ref, kvc_ref, kvn_ref, wgu_in, wd_in, o_ref, wgu_out, wd_out, *, seq):
    wgu_out[...] = wgu_in[0].astype(BF16)
    wd_out[...] = wd_in[0].astype(BF16)
    i = pl.program_id(1)
    nk = N_KV_HEADS * HEAD_DIM
    kv = jnp.concatenate([kvp_ref[...], kvc_ref[...], kvn_ref[...]], axis=0)
    band = 3 * BLOCK
    lo = lax.broadcasted_iota(jnp.int32, (1, LANES), 1) < HEAD_DIM
    zero = jnp.zeros((), BF16)
    k_lo, k_hi, v_dup_t = [], [], []
    for h in range(N_KV_HEADS):
        grp, half = h // 2, h % 2

        def block(base, swapped):
            col = base + (nk if swapped else 0) + grp * LANES
            return kv[:, col:col + LANES]
        k_lo.append(jnp.where(lo, block(0, half == 1), zero))
        k_hi.append(jnp.where(lo, zero, block(0, half == 0)))
        v_dup = jnp.where(lo, block(2 * nk, half == 1), block(2 * nk, half == 0))
        v_dup_t.append(v_dup.astype(F32).T.astype(BF16))

    key = lax.broadcasted_iota(jnp.int32, (BLOCK, BLOCK), 0)
    qry = lax.broadcasted_iota(jnp.int32, (BLOCK, BLOCK), 1)
    feat_lo = lax.broadcasted_iota(jnp.int32, (LANES, 1), 0) < HEAD_DIM
    units = [(j, h) for j in range(QBLKS) for h in range(N_KV_HEADS)]

    def scores_t(j, h):
        rows = slice(j * BLOCK, (j + 1) * BLOCK)
        qp = jnp.concatenate([q_ref[rows, (2 * h) * LANES:(2 * h + 1) * LANES],
                              q_ref[rows, (2 * h + 1) * LANES:(2 * h + 2) * LANES]], axis=0)
        keys = jnp.concatenate([k_lo[h][j * BLOCK:j * BLOCK + band], k_hi[h][j * BLOCK:j * BLOCK + band]],
                               axis=0)
        return lax.dot_general(keys, qp, (((1,), (1,)), ((), ())), preferred_element_type=F32)

    s_next = scores_t(*units[0])
    for u, (j, h) in enumerate(units):
        s = s_next
        if u + 1 < len(units):
            s_next = scores_t(*units[u + 1])
        n = i * QBLKS + j
        ok_prev = jnp.concatenate([(key >= qry) & (n > 0)] * 2, axis=1)
        ok_next = jnp.concatenate([(key <= qry) & (n < seq // BLOCK - 1)] * 2, axis=1)
        vt = v_dup_t[h][:, j * BLOCK:j * BLOCK + band]
        halves = []
        for par in range(2):
            sp = s[par * band:(par + 1) * band, :]
            sp = jnp.concatenate([jnp.where(ok_prev, sp[:BLOCK], -jnp.inf), sp[BLOCK:2 * BLOCK],
                                  jnp.where(ok_next, sp[2 * BLOCK:], -jnp.inf)], axis=0)
            sink_row = jnp.concatenate(
                [jnp.full((1, BLOCK), sink_ref[h * GQA_GROUP + g] * LOG2E, F32) for g in (par, par + 2)],
                axis=1)
            m = jnp.maximum(jnp.max(sp, axis=0, keepdims=True), sink_row)
            p = jnp.exp2(sp - m)
            denom = jnp.sum(p, axis=0, keepdims=True) + jnp.exp2(sink_row - m)
            halves.append(jnp.dot(vt, p.astype(BF16), preferred_element_type=F32) / denom)
        out_t = jnp.where(feat_lo, halves[0], halves[1])
        o_ref[j * BLOCK:(j + 1) * BLOCK, (2 * h) * LANES:(2 * h + 2) * LANES] = jnp.concatenate(
            [out_t[:, :BLOCK].T, out_t[:, BLOCK:].T], axis=1).astype(BF16)


def _attention(q, kv, sink, w_gate_up, w_down, layer, batch, seq):
    T = q.shape[0]
    qt = QBLKS * BLOCK
    nq = seq // qt
    nb = seq // BLOCK
    wkv = kv.shape[1]
    cur = lambda b, i: (b * nq + i, 0)
    prev = lambda b, i: (b * nb + jnp.maximum(i * QBLKS - 1, 0), 0)
    nxt = lambda b, i: (b * nb + jnp.minimum(i * QBLKS + QBLKS, nb - 1), 0)
    casts = [_weight_cast(w, layer, batch * nq, lambda b, i: b * nq + i) for w in (w_gate_up, w_down)]
    w_in, w_out, w_shape = ([c[k] for c in casts] for k in range(3))
    return pl.pallas_call(
        functools.partial(_attn_kernel, seq=seq),
        grid=(batch, nq),
        in_specs=[
            pl.BlockSpec(memory_space=pltpu.SMEM),
            pl.BlockSpec((qt, D_MODEL), cur),
            pl.BlockSpec((BLOCK, wkv), prev),
            pl.BlockSpec((qt, wkv), cur),
            pl.BlockSpec((BLOCK, wkv), nxt),
        ] + w_in,
        out_specs=[pl.BlockSpec((qt, D_MODEL), cur)] + w_out,
        out_shape=[jax.ShapeDtypeStruct((T, D_MODEL), BF16)] + w_shape,
        compiler_params=pltpu.CompilerParams(dimension_semantics=("arbitrary", "arbitrary")),
        name="swa_attention",
    )(sink, q, kv, kv, kv, w_gate_up, w_down)


def _proj_ln_kernel(a_ref, w_ref, x_ref, g_ref, b_ref, wr_ref, xo_ref, aff_ref):
    y = jnp.dot(a_ref[...], w_ref[...], preferred_element_type=F32)
    xn = _layer_norm(ALPHA * x_ref[...] + y, g_ref[...], b_ref[...])
    _store_token_layout(xo_ref, xn)
    aff_ref[0] = _router_affinity_t(xn, wr_ref[...])


def _proj_ln(a, w, x2d, g, b, wr_t, batch, seq):
    T = x2d.shape[0]
    tm = min(ROW_TILE, seq)
    nts = seq // tm
    return pl.pallas_call(
        _proj_ln_kernel,
        grid=(T // tm,),
        in_specs=[
            pl.BlockSpec((tm, D_MODEL), lambda i: (i, 0)),
            pl.BlockSpec((D_MODEL, D_MODEL), lambda i: (0, 0)),
            pl.BlockSpec((tm, D_MODEL), lambda i: (i, 0)),
            pl.BlockSpec((1, D_MODEL), lambda i: (0, 0)),
            pl.BlockSpec((1, D_MODEL), lambda i: (0, 0)),
            pl.BlockSpec((N_EXPERTS, D_MODEL), lambda i: (0, 0)),
        ],
        out_specs=[
            pl.BlockSpec((tm * TOK_ROWS, LANES), lambda i: (i, 0)),
            pl.BlockSpec((1, N_EXPERTS, tm), lambda i: (i // nts, 0, i % nts)),
        ],
        out_shape=[
            jax.ShapeDtypeStruct((T * TOK_ROWS, LANES), F32),
            jax.ShapeDtypeStruct((batch, N_EXPERTS, seq), F32),
        ],
        compiler_params=pltpu.CompilerParams(dimension_semantics=("arbitrary",)),
        name="proj_ln_router",
    )(a, w, x2d, g, b, wr_t)


SLOT_BLOCK = 64


def _lane_prefix(mask, tri, dst_ref):
    seq = mask.shape[1]
    off = jnp.zeros((mask.shape[0], 1), F32)
    mb = mask.astype(BF16)
    ends = []
    for c in range(seq // LANES):
        res = jnp.dot(mb[:, c * LANES:(c + 1) * LANES], tri, preferred_element_type=F32) + off
        dst_ref[:, c * LANES:(c + 1) * LANES] = res
        off = res[:, LANES - 1:LANES]
        ends.append(off)
    return ends


def _select_kernel(aff_ref, pos_ref, bnd_ref, *, cap):
    aff = aff_ref[0]
    as_float = lambda bits: pltpu.bitcast(bits, F32)
    rows = aff.shape[0]
    cur = jnp.zeros((rows, 1), jnp.int32)
    for bit in range(30, -1, -1):
        cand = cur | (1 << bit)
        cnt = jnp.sum((aff >= as_float(cand)).astype(jnp.int32), axis=1, keepdims=True)
        cur = jnp.where(cnt >= cap, cand, cur)
    gt = aff >= as_float(cur + 1)
    eq = (aff >= as_float(cur)) & jnp.logical_not(gt)
    n_gt = jnp.sum(gt.astype(jnp.int32), axis=1, keepdims=True)
    ties_taken = (cap - n_gt).astype(F32)
    ri = lax.broadcasted_iota(jnp.int32, (LANES, LANES), 0)
    ci = lax.broadcasted_iota(jnp.int32, (LANES, LANES), 1)
    tri = (ri <= ci).astype(BF16)
    _lane_prefix(eq, tri, pos_ref.at[0])
    sel = gt | (eq & (pos_ref[0] <= ties_taken))
    ends = _lane_prefix(sel, tri, pos_ref.at[0])
    pos_ref[0] = jnp.where(sel, pos_ref[0], 0.0)
    cols = []
    for sb in range(cap // SLOT_BLOCK):
        for target in (sb * SLOT_BLOCK + 1, (sb + 1) * SLOT_BLOCK):
            blk = jnp.zeros((rows, 1), jnp.int32)
            for end in ends[:-1]:
                blk = blk + (end < target).astype(jnp.int32)
            cols.append(blk)
    bnd_ref[0] = jnp.concatenate(cols, axis=1)


def _compact_kernel(bnd_ref, pos_ref, aff_ref, idx_ref, gate_ref, ptok, pgate, *, cap):
    b = pl.program_id(0)
    n_sb = cap // SLOT_BLOCK
    groups = SLOT_BLOCK // SUBLANES
    lane = lax.broadcasted_iota(jnp.int32, (SUBLANES, LANES), 1).astype(F32)
    sub1 = (lax.broadcasted_iota(jnp.int32, (SUBLANES, 1), 0) + 1).astype(F32)
    lane_e = lax.broadcasted_iota(jnp.int32, (cap, N_EXPERTS), 1)
    idx_ref[0] = jnp.zeros((cap, N_EXPERTS), jnp.int32)
    gate_ref[0] = jnp.zeros((cap, N_EXPERTS), F32)

    def expert(e, carry):
        def slot_block(sb, carry2):
            base = ((b * N_EXPERTS + e) * n_sb + sb) * 2
            slot0 = jnp.asarray(sb * SLOT_BLOCK, F32)

            def lane_block(c, accs):
                off = pl.multiple_of(c * LANES, LANES)
                p = jnp.broadcast_to(pos_ref[0, e:e + 1, pl.ds(off, LANES)], (SUBLANES, LANES))
                a = jnp.broadcast_to(aff_ref[0, e:e + 1, pl.ds(off, LANES)], (SUBLANES, LANES))
                tokv = lane + jnp.asarray(c * LANES, F32)
                out = []
                for g in range(groups):
                    hit = p == (slot0 + (g * SUBLANES) + sub1)
                    out.append(jnp.where(hit, tokv, accs[2 * g]))
                    out.append(jnp.where(hit, a, accs[2 * g + 1]))
                return tuple(out)

            zeros = tuple(jnp.zeros((SUBLANES, LANES), F32) for _ in range(2 * groups))
            accs = lax.fori_loop(bnd_ref[base], bnd_ref[base + 1] + 1, lane_block, zeros)
            for g in range(groups):
                rows = pl.ds(pl.multiple_of(sb * SLOT_BLOCK + g * SUBLANES, SUBLANES), SUBLANES)
                ptok[rows, :] = accs[2 * g]
                pgate[rows, :] = accs[2 * g + 1]
            return carry2

        lax.fori_loop(0, n_sb, slot_block, 0)
        tok_col = jnp.sum(ptok[...], axis=1, keepdims=True).astype(jnp.int32)
        gate_col = jnp.sum(pgate[...], axis=1, keepdims=True)
        idx_ref[0] = jnp.where(lane_e == e, tok_col, idx_ref[0])
        gate_ref[0] = jnp.where(lane_e == e, gate_col, gate_ref[0])
        return carry

    for e in range(N_EXPERTS):
        expert(e, 0)


def _route(aff_t, cap):
    batch, _, seq = aff_t.shape
    n_bnd = 2 * (cap // SLOT_BLOCK)
    seq_spec = pl.BlockSpec((1, N_EXPERTS, seq), lambda b, *_: (b, 0, 0))
    rows = batch * N_EXPERTS
    pos, bnd = pl.pallas_call(
        functools.partial(_select_kernel, cap=cap),
        grid=(1,),
        in_specs=[pl.BlockSpec((1, rows, seq), lambda i: (0, 0, 0))],
        out_specs=[pl.BlockSpec((1, rows, seq), lambda i: (0, 0, 0)),
                   pl.BlockSpec((1, rows, n_bnd), lambda i: (0, 0, 0))],
        out_shape=[
            jax.ShapeDtypeStruct((1, rows, seq), F32),
            jax.ShapeDtypeStruct((1, rows, n_bnd), jnp.int32),
        ],
        compiler_params=pltpu.CompilerParams(dimension_semantics=("arbitrary",)),
        name="expert_choice_select",
    )(aff_t.reshape(1, rows, seq))
    pos = pos.reshape(batch, N_EXPERTS, seq)
    slot_spec = pl.BlockSpec((1, cap, N_EXPERTS), lambda b, *_: (b, 0, 0))
    idx_t, gate_t = pl.pallas_call(
        functools.partial(_compact_kernel, cap=cap),
        grid_spec=pltpu.PrefetchScalarGridSpec(
            num_scalar_prefetch=1,
            grid=(batch,),
            in_specs=[seq_spec, seq_spec],
            out_specs=[slot_spec, slot_spec],
            scratch_shapes=[pltpu.VMEM((cap, LANES), F32), pltpu.VMEM((cap, LANES), F32)],
        ),
        out_shape=[
            jax.ShapeDtypeStruct((batch, cap, N_EXPERTS), jnp.int32),
            jax.ShapeDtypeStruct((batch, cap, N_EXPERTS), F32),
        ],
        compiler_params=pltpu.CompilerParams(dimension_semantics=("arbitrary",)),
        name="expert_choice_compact",
    )(bnd.reshape(-1), pos, aff_t)
    return jnp.swapaxes(idx_t, 1, 2), gate_t


LN_CHUNK = 256
SCATTER_GROUP = 2


def _moe_kernel(idx_ref, x1t_hbm, gate_ref, wgu_ref, wd_ref, lng_ref, lnb_ref, out_hbm,
                x1v, acc, xs3, y3, xs2d, yacc, stage, sem_in, sem_out, *, seq, cap):
    b = pl.program_id(0)
    e = pl.program_id(1)
    tok_rows = seq * TOK_ROWS
    last_e = N_EXPERTS - 1

    def tok_slice(t):
        return pl.ds(pl.multiple_of(t * TOK_ROWS, TOK_ROWS), TOK_ROWS)

    def gather_rows(expert):
        base = (b * N_EXPERTS + expert) * cap
        for i in range(cap):
            xs3[i * TOK_ROWS:(i + 1) * TOK_ROWS, :] = x1v[tok_slice(idx_ref[base + i]), :]

    def scatter_rows(expert):
        base = (b * N_EXPERTS + expert) * cap
        for g0 in range(0, cap, SCATTER_GROUP):
            toks = [idx_ref[base + g0 + k] for k in range(SCATTER_GROUP)]
            vals = [acc[tok_slice(toks[k]), :] + y3[(g0 + k) * TOK_ROWS:(g0 + k + 1) * TOK_ROWS, :]
                    for k in range(SCATTER_GROUP)]
            for k in range(SCATTER_GROUP):
                acc[tok_slice(toks[k]), :] = vals[k]

    def load_sequence(seq_idx):
        return pltpu.make_async_copy(x1t_hbm.at[pl.ds(seq_idx * tok_rows, tok_rows)], x1v, sem_in)

    def init_accumulator():
        acc[...] = ALPHA * x1v[...]

    @pl.when((b == 0) & (e == 0))
    def _first_sequence():
        load_sequence(0).start()
        load_sequence(0).wait()
        init_accumulator()

    @pl.when(e == 0)
    def _start_sequence():
        yacc[...] = jnp.zeros_like(yacc)

        def body(i, carry):
            xs3[tok_slice(i), :] = x1v[tok_slice(idx_ref[b * N_EXPERTS * cap + i]), :]
            return carry
        lax.fori_loop(0, cap, body, 0, unroll=8)
        xs2d[0] = _load_token_layout(xs3, 0, cap).astype(BF16)

    def gate_column(expert):
        gates = gate_ref[0]
        lane_e = lax.broadcasted_iota(jnp.int32, gates.shape, 1)
        return jnp.sum(jnp.where(lane_e == expert, gates, 0.0), axis=1, keepdims=True)

    prev = jnp.maximum(e - 1, 0)
    _store_token_layout(y3, yacc[...] * gate_column(prev))
    scatter_rows(prev)
    gather_rows(jnp.minimum(e + 1, last_e))
    gu = jnp.dot(xs2d[e % 2], wgu_ref[0], preferred_element_type=F32)
    gate_h, up_h = gu[:, :EXPERT_FF], gu[:, EXPERT_FF:]
    act = (gate_h * jax.nn.sigmoid(gate_h) * up_h).astype(BF16)
    yacc[...] = jnp.dot(act, wd_ref[0], preferred_element_type=F32)
    xs2d[(e + 1) % 2] = _load_token_layout(xs3, 0, cap).astype(BF16)

    @pl.when(e == last_e)
    def _finalize():
        _store_token_layout(y3, yacc[...] * gate_column(last_e))

        def body(gi, carry):
            base = (b * N_EXPERTS + last_e) * cap + gi * SUBLANES
            toks = [idx_ref[base + k] for k in range(SUBLANES)]
            vals = [acc[tok_slice(toks[k]), :] + y3[tok_slice(gi * SUBLANES + k), :]
                    for k in range(SUBLANES)]
            for k in range(SUBLANES):
                acc[tok_slice(toks[k]), :] = vals[k]
            return carry
        lax.fori_loop(0, cap // SUBLANES, body, 0)

        n_seq = pl.num_programs(0)

        @pl.when(b + 1 < n_seq)
        def _():
            load_sequence(b + 1).start()

        n_chunks = seq // LN_CHUNK

        def out_copy(c, slot):
            return pltpu.make_async_copy(
                stage.at[slot], out_hbm.at[pl.ds(b * seq + c * LN_CHUNK, LN_CHUNK)], sem_out.at[slot])

        def ln_body(c, carry):
            slot = c % 2
            first = c * LN_CHUNK
            res = _layer_norm(_load_token_layout(acc, first, LN_CHUNK), lng_ref[...], lnb_ref[...])

            @pl.when(c >= 2)
            def _():
                out_copy(c - 2, slot).wait()
            stage[slot] = res
            out_copy(c, slot).start()
            return carry
        lax.fori_loop(0, n_chunks, ln_body, 0)
        out_copy(n_chunks - 2, (n_chunks - 2) % 2).wait()
        out_copy(n_chunks - 1, (n_chunks - 1) % 2).wait()

        @pl.when(b + 1 < n_seq)
        def _():
            load_sequence(b + 1).wait()
            init_accumulator()


def _moe(x1t, idx, gate_t, w_gate_up, w_down, ln_g, ln_b, batch, seq):
    cap = idx.shape[2]
    T = batch * seq
    idx_flat = idx.reshape(-1)
    grid_spec = pltpu.PrefetchScalarGridSpec(
        num_scalar_prefetch=1,
        grid=(batch, N_EXPERTS),
        in_specs=[
            pl.BlockSpec(memory_space=pl.ANY),
            pl.BlockSpec((1, cap, N_EXPERTS), lambda b, e, idx: (b, 0, 0)),
            pl.BlockSpec((1, D_MODEL, 2 * EXPERT_FF), lambda b, e, idx: (e, 0, 0)),
            pl.BlockSpec((1, EXPERT_FF, D_MODEL), lambda b, e, idx: (e, 0, 0)),
            pl.BlockSpec((1, D_MODEL), lambda b, e, idx: (0, 0)),
            pl.BlockSpec((1, D_MODEL), lambda b, e, idx: (0, 0)),
        ],
        out_specs=pl.BlockSpec(memory_space=pl.ANY),
        scratch_shapes=[
            pltpu.VMEM((seq * TOK_ROWS, LANES), F32),
            pltpu.VMEM((seq * TOK_ROWS, LANES), F32),
            pltpu.VMEM((cap * TOK_ROWS, LANES), F32),
            pltpu.VMEM((cap * TOK_ROWS, LANES), F32),
            pltpu.VMEM((2, cap, D_MODEL), BF16),
            pltpu.VMEM((cap, D_MODEL), F32),
            pltpu.VMEM((2, LN_CHUNK, D_MODEL), F32),
            pltpu.SemaphoreType.DMA(()),
            pltpu.SemaphoreType.DMA((2,)),
        ],
    )
    return pl.pallas_call(
        functools.partial(_moe_kernel, seq=seq, cap=cap),
        grid_spec=grid_spec,
        out_shape=jax.ShapeDtypeStruct((T, D_MODEL), F32),
        compiler_params=pltpu.CompilerParams(
            dimension_semantics=("arbitrary", "arbitrary"),
            vmem_limit_bytes=VMEM_LIMIT_MOE),
        name="expert_ffn_combine_ln",
    )(idx_flat, x1t, gate_t, w_gate_up, w_down, ln_g, ln_b)


HALO = SUBLANES


def _gelu_tanh(x):
    return 0.5 * x * (1.0 + jnp.tanh(0.7978845608028654 * (x + 0.044715 * (x * x * x))))


def _lru_in_kernel(x_ref, xp_ref, xn_ref, w_ref, cw_ref, cb_ref, gg_ref, xc_ref, *, nts):
    i = pl.program_id(0)
    tm = x_ref.shape[0]
    x = x_ref[...].astype(BF16)
    xp = xp_ref[...].astype(BF16)
    xn = xn_ref[...].astype(BF16)
    first = (i % nts) == 0
    last = (i % nts) == nts - 1
    for k, gate in enumerate(_dot_by_columns(x, w_ref, 0, D_MODEL)):
        gg_ref[:, k * MXU_COLS:(k + 1) * MXU_COLS] = _gelu_tanh(gate)
    n_ext = tm + 2 * HALO
    for k in range(D_MODEL // MXU_COLS):
        cols = slice(k * MXU_COLS, (k + 1) * MXU_COLS)
        xb, = _dot_by_columns(x, w_ref, D_MODEL + k * MXU_COLS, MXU_COLS)
        xbp, = _dot_by_columns(xp, w_ref, D_MODEL + k * MXU_COLS, MXU_COLS)
        xbn, = _dot_by_columns(xn, w_ref, D_MODEL + k * MXU_COLS, MXU_COLS)
        ext = jnp.concatenate([jnp.where(first, 0.0, xbp), xb, jnp.where(last, 0.0, xbn)], axis=0)
        xc = cb_ref[:, cols]
        for tap in range(CONV_WIDTH):
            shift = (2 - tap) % n_ext
            src = ext if shift == 0 else pltpu.roll(ext, shift, axis=0)
            xc = xc + cw_ref[tap:tap + 1, cols] * src[HALO:HALO + tm, :]
        xc_ref[:, cols] = xc


def _lru_in(x2d, w_in, conv_w, conv_b, seq):
    T = x2d.shape[0]
    tm = min(ROW_TILE, seq)
    nts = seq // tm
    hb = tm // HALO
    nblk = T // HALO
    return pl.pallas_call(
        functools.partial(_lru_in_kernel, nts=nts),
        grid=(T // tm,),
        in_specs=[
            pl.BlockSpec((tm, D_MODEL), lambda i: (i, 0)),
            pl.BlockSpec((HALO, D_MODEL), lambda i: (jnp.maximum(i * hb - 1, 0), 0)),
            pl.BlockSpec((HALO, D_MODEL), lambda i: (jnp.minimum((i + 1) * hb, nblk - 1), 0)),
            pl.BlockSpec((D_MODEL, 2 * D_MODEL), lambda i: (0, 0)),
            pl.BlockSpec((CONV_WIDTH, D_MODEL), lambda i: (0, 0)),
            pl.BlockSpec((1, D_MODEL), lambda i: (0, 0)),
        ],
        out_specs=[
            pl.BlockSpec((tm, D_MODEL), lambda i: (i, 0)),
            pl.BlockSpec((tm, D_MODEL), lambda i: (i, 0)),
        ],
        out_shape=[
            jax.ShapeDtypeStruct((T, D_MODEL), F32),
            jax.ShapeDtypeStruct((T, D_MODEL), F32),
        ],
        compiler_params=pltpu.CompilerParams(dimension_semantics=("arbitrary",)),
        name="lru_in_conv",
    )(x2d, x2d, x2d, w_in, conv_w, conv_b)


SCAN_TM = 128
CHAIN_PITCH = SCAN_TM + SUBLANES
PAIRS = D_MODEL // (2 * LANES)


def _lru_chain_scan(xc_ref, wg_ref, br_ref, bi_ref, lam_ref, carry, abuf, ubuf, hbuf, *, reverse, seq):
    i = pl.program_id(0)
    nseq, tm = xc_ref.shape[0], xc_ref.shape[1]
    assert 2 * nseq == SUBLANES and tm == SCAN_TM
    nt = seq // tm
    tile = (nt - 1 - i) if reverse else i

    @pl.when(i == 0)
    def _():
        carry[...] = jnp.zeros_like(carry)

    half_nsp = (-0.5 * LRU_C) * jax.nn.softplus(-lam_ref[...])
    row = lax.broadcasted_iota(jnp.int32, (tm, 1), 0)
    is_start = (tile * tm + row) == (seq - 1 if reverse else 0)
    xc_all = jnp.concatenate([xc_ref[b] for b in range(nseq)], axis=0)
    xcb = xc_all.astype(BF16)
    r_parts, i_parts = [], []
    for n in range(LRU_BLOCKS):
        res = jnp.dot(xcb[:, n * LRU_BLOCK_W:(n + 1) * LRU_BLOCK_W], wg_ref[n],
                      preferred_element_type=F32)
        r_parts.append(res[:, :LRU_BLOCK_W])
        i_parts.append(res[:, LRU_BLOCK_W:])
    r_pre = jnp.concatenate(r_parts, axis=1)
    i_pre = jnp.concatenate(i_parts, axis=1)
    for b in range(nseq):
        rows_b = slice(b * tm, (b + 1) * tm)
        xc = xc_all[rows_b]
        tr = jnp.tanh(r_pre[rows_b] + br_ref[...])
        ig = 0.5 * jnp.tanh(i_pre[rows_b] + bi_ref[...]) + 0.5
        log_a = tr * half_nsp + half_nsp
        a = jnp.exp(log_a)
        z = -jnp.tanh(log_a) * (1.0 + a * a)
        root = jnp.where(z > 0.0, z * lax.rsqrt(z), 0.0)
        u = jnp.where(is_start, 1.0, root) * ig * xc
        for lb in range(2 * PAIRS):
            rows = slice((2 * b + lb % 2) * CHAIN_PITCH, (2 * b + lb % 2) * CHAIN_PITCH + tm)
            abuf[lb // 2, rows, :] = a[:, lb * LANES:(lb + 1) * LANES]
            ubuf[lb // 2, rows, :] = u[:, lb * LANES:(lb + 1) * LANES]

    def step(k, hs):
        t = (tm - 1 - k) if reverse else k
        chains = pl.ds(t, SUBLANES, stride=CHAIN_PITCH)
        out = []
        for m in range(PAIRS):
            h = abuf[m, chains, :] * hs[m] + ubuf[m, chains, :]
            hbuf[m, chains, :] = h
            out.append(h)
        return tuple(out)

    hs = lax.fori_loop(0, tm, step, tuple(carry[m] for m in range(PAIRS)), unroll=8)
    for m in range(PAIRS):
        carry[m] = hs[m]


def _chain_tile(hbuf, b, tm):
    return jnp.concatenate(
        [hbuf[lb // 2, (2 * b + lb % 2) * CHAIN_PITCH:(2 * b + lb % 2) * CHAIN_PITCH + tm, :]
         for lb in range(2 * PAIRS)], axis=1)


def _lru_fwd_kernel(xc_ref, wg_ref, br_ref, bi_ref, lam_ref, wd_in, h_ref, wd_out,
                    carry, abuf, ubuf, hbuf, *, seq):
    wd_out[...] = wd_in[0].astype(BF16)
    _lru_chain_scan(xc_ref, wg_ref, br_ref, bi_ref, lam_ref, carry, abuf, ubuf, hbuf,
                    reverse=False, seq=seq)
    for b in range(h_ref.shape[0]):
        h_ref[b] = _chain_tile(hbuf, b, h_ref.shape[1])


def _lru_bwd_kernel(xc_ref, wg_ref, br_ref, bi_ref, lam_ref, hf_ref, gg_ref, x_ref, wo_ref,
                    g_ref, b_ref, wr_ref, wgu_in, xo_ref, aff_ref, wgu_out, carry, abuf, ubuf, hbuf, *, seq):
    wgu_out[...] = wgu_in[0].astype(BF16)
    _lru_chain_scan(xc_ref, wg_ref, br_ref, bi_ref, lam_ref, carry, abuf, ubuf, hbuf,
                    reverse=True, seq=seq)
    nseq, tm = x_ref.shape[0], x_ref.shape[1]
    gated = [((hf_ref[b] + _chain_tile(hbuf, b, tm)) * gg_ref[b]).astype(BF16) for b in range(nseq)]
    y_all = jnp.dot(jnp.concatenate(gated, axis=0), wo_ref[...], preferred_element_type=F32)
    for b in range(nseq):
        xn = _layer_norm(ALPHA * x_ref[b] + y_all[b * tm:(b + 1) * tm], g_ref[...], b_ref[...])
        _store_token_layout(xo_ref.at[b], xn)
        aff_ref[b] = _router_affinity_t(xn, wr_ref[...])


def _lru_specs(batch, tile_map):
    const2 = lambda i: (0, 0)
    return [
        pl.BlockSpec((batch, SCAN_TM, D_MODEL), tile_map),
        pl.BlockSpec((LRU_BLOCKS, LRU_BLOCK_W, 2 * LRU_BLOCK_W), lambda i: (0, 0, 0)),
        pl.BlockSpec((1, D_MODEL), const2),
        pl.BlockSpec((1, D_MODEL), const2),
        pl.BlockSpec((1, D_MODEL), const2),
    ]


def _lru_scratch():
    buf = pltpu.VMEM((PAIRS, SUBLANES * CHAIN_PITCH, LANES), F32)
    return [pltpu.VMEM((PAIRS, SUBLANES, LANES), F32), buf, buf, buf]


def _lru_forward(xc3, wg, br, bi, lam, w_down, layer):
    batch, seq, _ = xc3.shape
    tile_map = lambda i: (0, i, 0)
    nt = seq // SCAN_TM
    w_in, w_out, w_shape = _weight_cast(w_down, layer, nt, lambda i: i)
    return pl.pallas_call(
        functools.partial(_lru_fwd_kernel, seq=seq),
        grid=(nt,),
        in_specs=_lru_specs(batch, tile_map) + [w_in],
        out_specs=[pl.BlockSpec((batch, SCAN_TM, D_MODEL), tile_map), w_out],
        out_shape=[jax.ShapeDtypeStruct((batch, seq, D_MODEL), F32), w_shape],
        scratch_shapes=_lru_scratch(),
        compiler_params=pltpu.CompilerParams(dimension_semantics=("arbitrary",)),
        name="lru_forward_scan",
    )(xc3, wg, br, bi, lam, w_down)


def _lru_backward_out(xc3, wg, br, bi, lam, hf3, gg3, x3d, w_out, g, b, wr_t, w_gate_up, layer):
    batch, seq, _ = xc3.shape
    nt = seq // SCAN_TM
    tile_map = lambda i: (0, nt - 1 - i, 0)
    const2 = lambda i: (0, 0)
    tile = pl.BlockSpec((batch, SCAN_TM, D_MODEL), tile_map)
    w_in, w_out_spec, w_shape = _weight_cast(w_gate_up, layer, nt, lambda i: i)
    return pl.pallas_call(
        functools.partial(_lru_bwd_kernel, seq=seq),
        grid=(nt,),
        in_specs=_lru_specs(batch, tile_map) + [
            tile, tile, tile,
            pl.BlockSpec((D_MODEL, D_MODEL), const2),
            pl.BlockSpec((1, D_MODEL), const2),
            pl.BlockSpec((1, D_MODEL), const2),
            pl.BlockSpec((N_EXPERTS, D_MODEL), const2),
            w_in,
        ],
        out_specs=[
            pl.BlockSpec((batch, SCAN_TM * TOK_ROWS, LANES), tile_map),
            pl.BlockSpec((batch, N_EXPERTS, SCAN_TM), lambda i: (0, 0, nt - 1 - i)),
            w_out_spec,
        ],
        out_shape=[
            jax.ShapeDtypeStruct((batch, seq * TOK_ROWS, LANES), F32),
            jax.ShapeDtypeStruct((batch, N_EXPERTS, seq), F32),
            w_shape,
        ],
        scratch_shapes=_lru_scratch(),
        compiler_params=pltpu.CompilerParams(dimension_semantics=("arbitrary",)),
        name="lru_backward_scan_out_ln_router",
    )(xc3, wg, br, bi, lam, hf3, gg3, x3d, w_out, g, b, wr_t, w_gate_up)


def _rotary_tables(seq):
    pos = jnp.arange(seq, dtype=F32)
    inv_freq = ROPE_THETA ** (-jnp.arange(0, ROT_DIM, 2, dtype=F32) / ROT_DIM)
    ang = pos[:, None] * inv_freq[None, :]
    cos, sin = jnp.cos(ang), jnp.sin(ang)
    pad = HEAD_DIM - ROT_DIM
    cos_h = jnp.concatenate([cos, cos, jnp.ones((seq, pad), F32)], axis=1)
    sin_h = jnp.concatenate([-sin, sin, jnp.zeros((seq, pad), F32)], axis=1)
    reps = LANES // HEAD_DIM
    return jnp.tile(cos_h, (1, reps)), jnp.tile(sin_h, (1, reps))


def _moe_block(x1t, aff_t, w_gate_up, w_down, ln_g, ln_b, batch, seq):
    cap = CAPACITY_FACTOR * seq // N_EXPERTS
    idx, gate_t = _route(aff_t, cap)
    return _moe(x1t, idx, gate_t, w_gate_up, w_down, ln_g.reshape(1, -1), ln_b.reshape(1, -1), batch, seq)


def kernel(x, attn_w_qkv, attn_w_o, attn_sink, lru_w_in, lru_conv_w, lru_conv_b, lru_w_rgate,
           lru_b_rgate, lru_w_igate, lru_b_igate, lru_lambda, lru_w_out, moe_w_router, moe_w_gate_up,
           moe_w_down, ln_mix_g, ln_mix_b, ln_ffn_g, ln_ffn_b):
    batch, seq, _ = x.shape
    x2d = x.reshape(batch * seq, D_MODEL)
    row = lambda v: v.reshape(1, -1)

    cos_t, sin_t = _rotary_tables(seq)
    q, kv = _qkv_proj(x2d, attn_w_qkv[0].astype(BF16), cos_t, sin_t, seq)
    o, wgu0, wd0 = _attention(q, kv, attn_sink[0], moe_w_gate_up, moe_w_down, 0, batch, seq)
    x1, aff_t = _proj_ln(o, attn_w_o[0].astype(BF16), x2d, row(ln_mix_g[0]), row(ln_mix_b[0]),
                         moe_w_router[0].T.astype(BF16), batch, seq)
    x2 = _moe_block(x1, aff_t, wgu0, wd0, ln_ffn_g[0], ln_ffn_b[0], batch, seq)

    gg, xc = _lru_in(x2, lru_w_in[0].astype(BF16), lru_conv_w[0], row(lru_conv_b[0]), seq)
    wg = (0.5 * jnp.concatenate([lru_w_rgate[0], lru_w_igate[0]], axis=-1)).astype(BF16)
    half_row = lambda v: 0.5 * v.reshape(1, -1)
    per_seq = lambda t: t.reshape(batch, seq, D_MODEL)
    xc3 = per_seq(xc)
    hf3, wd1 = _lru_forward(xc3, wg[0], half_row(lru_b_rgate[0, 0]), half_row(lru_b_igate[0, 0]),
                            row(lru_lambda[0, 0]), moe_w_down, 1)
    x3, aff_t, wgu1 = _lru_backward_out(xc3, wg[1], half_row(lru_b_rgate[0, 1]), half_row(lru_b_igate[0, 1]),
                                        row(lru_lambda[0, 1]), hf3, per_seq(gg), per_seq(x2),
                                        lru_w_out[0].astype(BF16), row(ln_mix_g[1]), row(ln_mix_b[1]),
                                        moe_w_router[1].T.astype(BF16), moe_w_gate_up, 1)
    x3 = x3.reshape(batch * seq * TOK_ROWS, LANES)
    x4 = _moe_block(x3, aff_t, wgu1, wd1, ln_ffn_g[1], ln_ffn_b[1], batch, seq)
    return x4.reshape(batch, seq, D_MODEL)
```

```python
import functools

import jax
import jax.numpy as jnp
from jax import lax
from jax.experimental import pallas as pl
from jax.experimental.pallas import tpu as pltpu

F32 = jnp.float32
BF16 = jnp.bfloat16

D_MODEL = 1024
HEAD_DIM = 64
N_Q_HEADS = 16
N_KV_HEADS = 4
GQA_GROUP = 4
WINDOW = 128
BLOCK = 128
ROT_DIM = 16
ROPE_THETA = 500000.0
LRU_BLOCKS = 4
LRU_BLOCK_W = 256
CONV_WIDTH = 4
LRU_C = 8.0
N_EXPERTS = 16
EXPERT_FF = 1024
CAPACITY_FACTOR = 2
LN_EPS = 1e-5
DEPTH = 2
ALPHA = (2 * DEPTH) ** 0.25

LANES = 128
SUBLANES = 8
MXU_COLS = 256
ROW_TILE = 1024
TOK_ROWS = D_MODEL // LANES
VMEM_LIMIT_MOE = 61 * 1024 * 1024


def _layer_norm(v, g, b):
    mu = jnp.mean(v, axis=-1, keepdims=True)
    vc = v - mu
    var = jnp.mean(vc * vc, axis=-1, keepdims=True)
    return vc * lax.rsqrt(var + LN_EPS) * g + b


def _store_token_layout(dst_ref, val):
    n = val.shape[0]
    for s in range(TOK_ROWS):
        dst_ref[pl.ds(s, n, stride=TOK_ROWS), :] = val[:, s * LANES:(s + 1) * LANES]


def _load_token_layout(src_ref, first_token, n):
    base = first_token * TOK_ROWS
    return jnp.concatenate(
        [src_ref[pl.ds(base + s, n, stride=TOK_ROWS), :] for s in range(TOK_ROWS)], axis=1)


def _router_affinity_t(xn, wr_t):
    logits_t = lax.dot_general(wr_t, xn.astype(BF16), (((1,), (1,)), ((), ())),
                               preferred_element_type=F32)
    m = jnp.max(logits_t, axis=0, keepdims=True)
    p = jnp.exp(logits_t - m)
    return p / jnp.sum(p, axis=0, keepdims=True)


def _dot_by_columns(lhs, w_ref, first_col, n_cols):
    return [jnp.dot(lhs, w_ref[:, c:c + MXU_COLS], preferred_element_type=F32)
            for c in range(first_col, first_col + n_cols, MXU_COLS)]


def _weight_cast(w, layer, n_steps, step_of):
    e, r, c = w.shape[1:]
    if n_steps <= e:
        blk = (e // n_steps, r, c)
        in_map = lambda *g: (layer, step_of(*g), 0, 0)
        out_map = lambda *g: (step_of(*g), 0, 0)
    else:
        k = n_steps // e
        blk = (1, r // k, c)
        in_map = lambda *g: (layer, step_of(*g) // k, step_of(*g) % k, 0)
        out_map = lambda *g: (step_of(*g) // k, step_of(*g) % k, 0)
    return (pl.BlockSpec((1,) + blk, in_map), pl.BlockSpec(blk, out_map),
            jax.ShapeDtypeStruct((e, r, c), BF16))


LOG2E = 1.4426950408889634


def _qkv_kernel(x_ref, w_ref, cos_ref, sin_ref, q_ref, kv_ref):
    x = x_ref[...].astype(BF16)
    acc = jnp.dot(x, w_ref[...], preferred_element_type=F32)
    cosb = cos_ref[...]
    sinb = sin_ref[...]
    lane = lax.broadcasted_iota(jnp.int32, cosb.shape, 1)
    low = (lane % HEAD_DIM) < (ROT_DIM // 2)
    nq = N_Q_HEADS * HEAD_DIM
    nk = N_KV_HEADS * HEAD_DIM

    def rot(t):
        up = pltpu.roll(t, LANES - ROT_DIM // 2, axis=1)
        dn = pltpu.roll(t, ROT_DIM // 2, axis=1)
        return t * cosb + jnp.where(low, up, dn) * sinb

    qs = [rot(acc[:, c * LANES:(c + 1) * LANES]) * (LOG2E * HEAD_DIM ** -0.5) for c in range(nq // LANES)]
    q_ref[...] = jnp.concatenate(qs, axis=1).astype(BF16)
    ks = [rot(acc[:, nq + c * LANES:nq + (c + 1) * LANES]) for c in range(nk // LANES)]
    vs = [acc[:, nq + nk + c * LANES:nq + nk + (c + 1) * LANES] for c in range(nk // LANES)]
    swap = lambda t: pltpu.roll(t, HEAD_DIM, axis=1)
    kv_ref[...] = jnp.concatenate(
        ks + [swap(t) for t in ks] + vs + [swap(t) for t in vs], axis=1).astype(BF16)


def _qkv_proj(x2d, w_qkv, cos_t, sin_t, seq):
    T = x2d.shape[0]
    tm = min(ROW_TILE, seq)
    nq = N_Q_HEADS * HEAD_DIM
    nk = N_KV_HEADS * HEAD_DIM
    nts = seq // tm
    return pl.pallas_call(
        _qkv_kernel,
        grid=(T // tm,),
        in_specs=[
            pl.BlockSpec((tm, D_MODEL), lambda i: (i, 0)),
            pl.BlockSpec((D_MODEL, nq + 2 * nk), lambda i: (0, 0)),
            pl.BlockSpec((tm, LANES), lambda i: (i % nts, 0)),
            pl.BlockSpec((tm, LANES), lambda i: (i % nts, 0)),
        ],
        out_specs=[
            pl.BlockSpec((tm, nq), lambda i: (i, 0)),
            pl.BlockSpec((tm, 4 * nk), lambda i: (i, 0)),
        ],
        out_shape=[
            jax.ShapeDtypeStruct((T, nq), BF16),
            jax.ShapeDtypeStruct((T, 4 * nk), BF16),
        ],
        compiler_params=pltpu.CompilerParams(dimension_semantics=("arbitrary",)),
        name="qkv_proj",
    )(x2d, w_qkv, cos_t, sin_t)


QBLKS = 8


def _attn_kernel(sink_ref, q_ref, kvp_ref, kvc_ref, kvn_ref, wgu_in, wd_in, o_ref, wgu_out, wd_out, *, seq):
    wgu_out[...] = wgu_in[0].astype(BF16)
    wd_out[...] = wd_in[0].astype(BF16)
    i = pl.program_id(1)
    nk = N_KV_HEADS * HEAD_DIM
    kv = jnp.concatenate([kvp_ref[...], kvc_ref[...], kvn_ref[...]], axis=0)
    band = 3 * BLOCK
    lo = lax.broadcasted_iota(jnp.int32, (1, LANES), 1) < HEAD_DIM
    zero = jnp.zeros((), BF16)
    k_lo, k_hi, v_dup_t = [], [], []
    for h in range(N_KV_HEADS):
        grp, half = h // 2, h % 2

        def block(base, swapped):
            col = base + (nk if swapped else 0) + grp * LANES
            return kv[:, col:col + LANES]
        k_lo.append(jnp.where(lo, block(0, half == 1), zero))
        k_hi.append(jnp.where(lo, zero, block(0, half == 0)))
        v_dup = jnp.where(lo, block(2 * nk, half == 1), block(2 * nk, half == 0))
        v_dup_t.append(v_dup.astype(F32).T.astype(BF16))

    key = lax.broadcasted_iota(jnp.int32, (BLOCK, BLOCK), 0)
    qry = lax.broadcasted_iota(jnp.int32, (BLOCK, BLOCK), 1)
    feat_lo = lax.broadcasted_iota(jnp.int32, (LANES, 1), 0) < HEAD_DIM
    units = [(j, h) for j in range(QBLKS) for h in range(N_KV_HEADS)]

    def scores_t(j, h):
        rows = slice(j * BLOCK, (j + 1) * BLOCK)
        qp = jnp.concatenate([q_ref[rows, (2 * h) * LANES:(2 * h + 1) * LANES],
                              q_ref[rows, (2 * h + 1) * LANES:(2 * h + 2) * LANES]], axis=0)
        keys = jnp.concatenate([k_lo[h][j * BLOCK:j * BLOCK + band], k_hi[h][j * BLOCK:j * BLOCK + band]],
                               axis=0)
        return lax.dot_general(keys, qp, (((1,), (1,)), ((), ())), preferred_element_type=F32)

    s_next = scores_t(*units[0])
    for u, (j, h) in enumerate(units):
        s = s_next
        if u + 1 < len(units):
            s_next = scores_t(*units[u + 1])
        n = i * QBLKS + j
        ok_prev = jnp.concatenate([(key >= qry) & (n > 0)] * 2, axis=1)
        ok_next = jnp.concatenate([(key <= qry) & (n < seq // BLOCK - 1)] * 2, axis=1)
        vt = v_dup_t[h][:, j * BLOCK:j * BLOCK + band]
        halves = []
        for par in range(2):
            sp = s[par * band:(par + 1) * band, :]
            sp = jnp.concatenate([jnp.where(ok_prev, sp[:BLOCK], -jnp.inf), sp[BLOCK:2 * BLOCK],
                                  jnp.where(ok_next, sp[2 * BLOCK:], -jnp.inf)], axis=0)
            sink_row = jnp.concatenate(
                [jnp.full((1, BLOCK), sink_ref[h * GQA_GROUP + g] * LOG2E, F32) for g in (par, par + 2)],
                axis=1)
            m = jnp.maximum(jnp.max(sp, axis=0, keepdims=True), sink_row)
            p = jnp.exp2(sp - m)
            denom = jnp.sum(p, axis=0, keepdims=True) + jnp.exp2(sink_row - m)
            halves.append(jnp.dot(vt, p.astype(BF16), preferred_element_type=F32) / denom)
        out_t = jnp.where(feat_lo, halves[0], halves[1])
        o_ref[j * BLOCK:(j + 1) * BLOCK, (2 * h) * LANES:(2 * h + 2) * LANES] = jnp.concatenate(
            [out_t[:, :BLOCK].T, out_t[:, BLOCK:].T], axis=1).astype(BF16)


def _attention(q, kv, sink, w_gate_up, w_down, layer, batch, seq):
    T = q.shape[0]
    qt = QBLKS * BLOCK
    nq = seq // qt
    nb = seq // BLOCK
    wkv = kv.shape[1]
    cur = lambda b, i: (b * nq + i, 0)
    prev = lambda b, i: (b * nb + jnp.maximum(i * QBLKS - 1, 0), 0)
    nxt = lambda b, i: (b * nb + jnp.minimum(i * QBLKS + QBLKS, nb - 1), 0)
    casts = [_weight_cast(w, layer, batch * nq, lambda b, i: b * nq + i) for w in (w_gate_up, w_down)]
    w_in, w_out, w_shape = ([c[k] for c in casts] for k in range(3))
    return pl.pallas_call(
        functools.partial(_attn_kernel, seq=seq),
        grid=(batch, nq),
        in_specs=[
            pl.BlockSpec(memory_space=pltpu.SMEM),
            pl.BlockSpec((qt, D_MODEL), cur),
            pl.BlockSpec((BLOCK, wkv), prev),
            pl.BlockSpec((qt, wkv), cur),
            pl.BlockSpec((BLOCK, wkv), nxt),
        ] + w_in,
        out_specs=[pl.BlockSpec((qt, D_MODEL), cur)] + w_out,
        out_shape=[jax.ShapeDtypeStruct((T, D_MODEL), BF16)] + w_shape,
        compiler_params=pltpu.CompilerParams(dimension_semantics=("arbitrary", "arbitrary")),
        name="swa_attention",
    )(sink, q, kv, kv, kv, w_gate_up, w_down)


def _proj_ln_kernel(a_ref, w_ref, x_ref, g_ref, b_ref, wr_ref, xo_ref, aff_ref):
    y = jnp.dot(a_ref[...], w_ref[...], preferred_element_type=F32)
    xn = _layer_norm(ALPHA * x_ref[...] + y, g_ref[...], b_ref[...])
    _store_token_layout(xo_ref, xn)
    aff_ref[0] = _router_affinity_t(xn, wr_ref[...])


def _proj_ln(a, w, x2d, g, b, wr_t, batch, seq):
    T = x2d.shape[0]
    tm = min(ROW_TILE, seq)
    nts = seq // tm
    return pl.pallas_call(
        _proj_ln_kernel,
        grid=(T // tm,),
        in_specs=[
            pl.BlockSpec((tm, D_MODEL), lambda i: (i, 0)),
            pl.BlockSpec((D_MODEL, D_MODEL), lambda i: (0, 0)),
            pl.BlockSpec((tm, D_MODEL), lambda i: (i, 0)),
            pl.BlockSpec((1, D_MODEL), lambda i: (0, 0)),
            pl.BlockSpec((1, D_MODEL), lambda i: (0, 0)),
            pl.BlockSpec((N_EXPERTS, D_MODEL), lambda i: (0, 0)),
        ],
        out_specs=[
            pl.BlockSpec((tm * TOK_ROWS, LANES), lambda i: (i, 0)),
            pl.BlockSpec((1, N_EXPERTS, tm), lambda i: (i // nts, 0, i % nts)),
        ],
        out_shape=[
            jax.ShapeDtypeStruct((T * TOK_ROWS, LANES), F32),
            jax.ShapeDtypeStruct((batch, N_EXPERTS, seq), F32),
        ],
        compiler_params=pltpu.CompilerParams(dimension_semantics=("arbitrary",)),
        name="proj_ln_router",
    )(a, w, x2d, g, b, wr_t)


SLOT_BLOCK = 64


def _lane_prefix(mask, tri, dst_ref):
    seq = mask.shape[1]
    off = jnp.zeros((mask.shape[0], 1), F32)
    mb = mask.astype(BF16)
    ends = []
    for c in range(seq // LANES):
        res = jnp.dot(mb[:, c * LANES:(c + 1) * LANES], tri, preferred_element_type=F32) + off
        dst_ref[:, c * LANES:(c + 1) * LANES] = res
        off = res[:, LANES - 1:LANES]
        ends.append(off)
    return ends


def _select_kernel(aff_ref, pos_ref, bnd_ref, *, cap):
    aff = aff_ref[0]
    as_float = lambda bits: pltpu.bitcast(bits, F32)
    rows = aff.shape[0]
    cur = jnp.zeros((rows, 1), jnp.int32)
    for bit in range(30, -1, -1):
        cand = cur | (1 << bit)
        cnt = jnp.sum((aff >= as_float(cand)).astype(jnp.int32), axis=1, keepdims=True)
        cur = jnp.where(cnt >= cap, cand, cur)
    gt = aff >= as_float(cur + 1)
    eq = (aff >= as_float(cur)) & jnp.logical_not(gt)
    n_gt = jnp.sum(gt.astype(jnp.int32), axis=1, keepdims=True)
    ties_taken = (cap - n_gt).astype(F32)
    ri = lax.broadcasted_iota(jnp.int32, (LANES, LANES), 0)
    ci = lax.broadcasted_iota(jnp.int32, (LANES, LANES), 1)
    tri = (ri <= ci).astype(BF16)
    _lane_prefix(eq, tri, pos_ref.at[0])
    sel = gt | (eq & (pos_ref[0] <= ties_taken))
    ends = _lane_prefix(sel, tri, pos_ref.at[0])
    pos_ref[0] = jnp.where(sel, pos_ref[0], 0.0)
    cols = []
    for sb in range(cap // SLOT_BLOCK):
        for target in (sb * SLOT_BLOCK + 1, (sb + 1) * SLOT_BLOCK):
            blk = jnp.zeros((rows, 1), jnp.int32)
            for end in ends[:-1]:
                blk = blk + (end < target).astype(jnp.int32)
            cols.append(blk)
    bnd_ref[0] = jnp.concatenate(cols, axis=1)


def _compact_kernel(bnd_ref, pos_ref, aff_ref, idx_ref, gate_ref, ptok, pgate, *, cap):
    b = pl.program_id(0)
    n_sb = cap // SLOT_BLOCK
    groups = SLOT_BLOCK // SUBLANES
    lane = lax.broadcasted_iota(jnp.int32, (SUBLANES, LANES), 1).astype(F32)
    sub1 = (lax.broadcasted_iota(jnp.int32, (SUBLANES, 1), 0) + 1).astype(F32)
    lane_e = lax.broadcasted_iota(jnp.int32, (cap, N_EXPERTS), 1)
    idx_ref[0] = jnp.zeros((cap, N_EXPERTS), jnp.int32)
    gate_ref[0] = jnp.zeros((cap, N_EXPERTS), F32)

    def expert(e, carry):
        def slot_block(sb, carry2):
            base = ((b * N_EXPERTS + e) * n_sb + sb) * 2
            slot0 = jnp.asarray(sb * SLOT_BLOCK, F32)

            def lane_block(c, accs):
                off = pl.multiple_of(c * LANES, LANES)
                p = jnp.broadcast_to(pos_ref[0, e:e + 1, pl.ds(off, LANES)], (SUBLANES, LANES))
                a = jnp.broadcast_to(aff_ref[0, e:e + 1, pl.ds(off, LANES)], (SUBLANES, LANES))
                tokv = lane + jnp.asarray(c * LANES, F32)
                out = []
                for g in range(groups):
                    hit = p == (slot0 + (g * SUBLANES) + sub1)
                    out.append(jnp.where(hit, tokv, accs[2 * g]))
                    out.append(jnp.where(hit, a, accs[2 * g + 1]))
                return tuple(out)

            zeros = tuple(jnp.zeros((SUBLANES, LANES), F32) for _ in range(2 * groups))
            accs = lax.fori_loop(bnd_ref[base], bnd_ref[base + 1] + 1, lane_block, zeros)
            for g in range(groups):
                rows = pl.ds(pl.multiple_of(sb * SLOT_BLOCK + g * SUBLANES, SUBLANES), SUBLANES)
                ptok[rows, :] = accs[2 * g]
                pgate[rows, :] = accs[2 * g + 1]
            return carry2

        lax.fori_loop(0, n_sb, slot_block, 0)
        tok_col = jnp.sum(ptok[...], axis=1, keepdims=True).astype(jnp.int32)
        gate_col = jnp.sum(pgate[...], axis=1, keepdims=True)
        idx_ref[0] = jnp.where(lane_e == e, tok_col, idx_ref[0])
        gate_ref[0] = jnp.where(lane_e == e, gate_col, gate_ref[0])
        return carry

    for e in range(N_EXPERTS):
        expert(e, 0)


def _route(aff_t, cap):
    batch, _, seq = aff_t.shape
    n_bnd = 2 * (cap // SLOT_BLOCK)
    seq_spec = pl.BlockSpec((1, N_EXPERTS, seq), lambda b, *_: (b, 0, 0))
    rows = batch * N_EXPERTS
    pos, bnd = pl.pallas_call(
        functools.partial(_select_kernel, cap=cap),
        grid=(1,),
        in_specs=[pl.BlockSpec((1, rows, seq), lambda i: (0, 0, 0))],
        out_specs=[pl.BlockSpec((1, rows, seq), lambda i: (0, 0, 0)),
                   pl.BlockSpec((1, rows, n_bnd), lambda i: (0, 0, 0))],
        out_shape=[
            jax.ShapeDtypeStruct((1, rows, seq), F32),
            jax.ShapeDtypeStruct((1, rows, n_bnd), jnp.int32),
        ],
        compiler_params=pltpu.CompilerParams(dimension_semantics=("arbitrary",)),
        name="expert_choice_select",
    )(aff_t.reshape(1, rows, seq))
    pos = pos.reshape(batch, N_EXPERTS, seq)
    slot_spec = pl.BlockSpec((1, cap, N_EXPERTS), lambda b, *_: (b, 0, 0))
    idx_t, gate_t = pl.pallas_call(
        functools.partial(_compact_kernel, cap=cap),
        grid_spec=pltpu.PrefetchScalarGridSpec(
            num_scalar_prefetch=1,
            grid=(batch,),
            in_specs=[seq_spec, seq_spec],
            out_specs=[slot_spec, slot_spec],
            scratch_shapes=[pltpu.VMEM((cap, LANES), F32), pltpu.VMEM((cap, LANES), F32)],
        ),
        out_shape=[
            jax.ShapeDtypeStruct((batch, cap, N_EXPERTS), jnp.int32),
            jax.ShapeDtypeStruct((batch, cap, N_EXPERTS), F32),
        ],
        compiler_params=pltpu.CompilerParams(dimension_semantics=("arbitrary",)),
        name="expert_choice_compact",
    )(bnd.reshape(-1), pos, aff_t)
    return jnp.swapaxes(idx_t, 1, 2), gate_t


LN_CHUNK = 256
SCATTER_GROUP = 2


def _moe_kernel(idx_ref, x1t_hbm, gate_ref, wgu_ref, wd_ref, lng_ref, lnb_ref, out_hbm,
                x1v, acc, xs3, y3, xs2d, yacc, stage, sem_in, sem_out, *, seq, cap):
    b = pl.program_id(0)
    e = pl.program_id(1)
    tok_rows = seq * TOK_ROWS
    last_e = N_EXPERTS - 1

    def tok_slice(t):
        return pl.ds(pl.multiple_of(t * TOK_ROWS, TOK_ROWS), TOK_ROWS)

    def gather_rows(expert):
        base = (b * N_EXPERTS + expert) * cap
        for i in range(cap):
            xs3[i * TOK_ROWS:(i + 1) * TOK_ROWS, :] = x1v[tok_slice(idx_ref[base + i]), :]

    def scatter_rows(expert):
        base = (b * N_EXPERTS + expert) * cap
        for g0 in range(0, cap, SCATTER_GROUP):
            toks = [idx_ref[base + g0 + k] for k in range(SCATTER_GROUP)]
            vals = [acc[tok_slice(toks[k]), :] + y3[(g0 + k) * TOK_ROWS:(g0 + k + 1) * TOK_ROWS, :]
                    for k in range(SCATTER_GROUP)]
            for k in range(SCATTER_GROUP):
                acc[tok_slice(toks[k]), :] = vals[k]

    def load_sequence(seq_idx):
        return pltpu.make_async_copy(x1t_hbm.at[pl.ds(seq_idx * tok_rows, tok_rows)], x1v, sem_in)

    def init_accumulator():
        acc[...] = ALPHA * x1v[...]

    @pl.when((b == 0) & (e == 0))
    def _first_sequence():
        load_sequence(0).start()
        load_sequence(0).wait()
        init_accumulator()

    @pl.when(e == 0)
    def _start_sequence():
        yacc[...] = jnp.zeros_like(yacc)

        def body(i, carry):
            xs3[tok_slice(i), :] = x1v[tok_slice(idx_ref[b * N_EXPERTS * cap + i]), :]
            return carry
        lax.fori_loop(0, cap, body, 0, unroll=8)
        xs2d[0] = _load_token_layout(xs3, 0, cap).astype(BF16)

    def gate_column(expert):
        gates = gate_ref[0]
        lane_e = lax.broadcasted_iota(jnp.int32, gates.shape, 1)
        return jnp.sum(jnp.where(lane_e == expert, gates, 0.0), axis=1, keepdims=True)

    prev = jnp.maximum(e - 1, 0)
    _store_token_layout(y3, yacc[...] * gate_column(prev))
    scatter_rows(prev)
    gather_rows(jnp.minimum(e + 1, last_e))
    gu = jnp.dot(xs2d[e % 2], wgu_ref[0], preferred_element_type=F32)
    gate_h, up_h = gu[:, :EXPERT_FF], gu[:, EXPERT_FF:]
    act = (gate_h * jax.nn.sigmoid(gate_h) * up_h).astype(BF16)
    yacc[...] = jnp.dot(act, wd_ref[0], preferred_element_type=F32)
    xs2d[(e + 1) % 2] = _load_token_layout(xs3, 0, cap).astype(BF16)

    @pl.when(e == last_e)
    def _finalize():
        _store_token_layout(y3, yacc[...] * gate_column(last_e))

        def body(gi, carry):
            base = (b * N_EXPERTS + last_e) * cap + gi * SUBLANES
            toks = [idx_ref[base + k] for k in range(SUBLANES)]
            vals = [acc[tok_slice(toks[k]), :] + y3[tok_slice(gi * SUBLANES + k), :]
                    for k in range(SUBLANES)]
            for k in range(SUBLANES):
                acc[tok_slice(toks[k]), :] = vals[k]
            return carry
        lax.fori_loop(0, cap // SUBLANES, body, 0)

        n_seq = pl.num_programs(0)

        @pl.when(b + 1 < n_seq)
        def _():
            load_sequence(b + 1).start()

        n_chunks = seq // LN_CHUNK

        def out_copy(c, slot):
            return pltpu.make_async_copy(
                stage.at[slot], out_hbm.at[pl.ds(b * seq + c * LN_CHUNK, LN_CHUNK)], sem_out.at[slot])

        def ln_body(c, carry):
            slot = c % 2
            first = c * LN_CHUNK
            res = _layer_norm(_load_token_layout(acc, first, LN_CHUNK), lng_ref[...], lnb_ref[...])

            @pl.when(c >= 2)
            def _():
                out_copy(c - 2, slot).wait()
            stage[slot] = res
            out_copy(c, slot).start()
            return carry
        lax.fori_loop(0, n_chunks, ln_body, 0)
        out_copy(n_chunks - 2, (n_chunks - 2) % 2).wait()
        out_copy(n_chunks - 1, (n_chunks - 1) % 2).wait()

        @pl.when(b + 1 < n_seq)
        def _():
            load_sequence(b + 1).wait()
            init_accumulator()


def _moe(x1t, idx, gate_t, w_gate_up, w_down, ln_g, ln_b, batch, seq):
    cap = idx.shape[2]
    T = batch * seq
    idx_flat = idx.reshape(-1)
    grid_spec = pltpu.PrefetchScalarGridSpec(
        num_scalar_prefetch=1,
        grid=(batch, N_EXPERTS),
        in_specs=[
            pl.BlockSpec(memory_space=pl.ANY),
            pl.BlockSpec((1, cap, N_EXPERTS), lambda b, e, idx: (b, 0, 0)),
            pl.BlockSpec((1, D_MODEL, 2 * EXPERT_FF), lambda b, e, idx: (e, 0, 0)),
            pl.BlockSpec((1, EXPERT_FF, D_MODEL), lambda b, e, idx: (e, 0, 0)),
            pl.BlockSpec((1, D_MODEL), lambda b, e, idx: (0, 0)),
            pl.BlockSpec((1, D_MODEL), lambda b, e, idx: (0, 0)),
        ],
        out_specs=pl.BlockSpec(memory_space=pl.ANY),
        scratch_shapes=[
            pltpu.VMEM((seq * TOK_ROWS, LANES), F32),
            pltpu.VMEM((seq * TOK_ROWS, LANES), F32),
            pltpu.VMEM((cap * TOK_ROWS, LANES), F32),
            pltpu.VMEM((cap * TOK_ROWS, LANES), F32),
            pltpu.VMEM((2, cap, D_MODEL), BF16),
            pltpu.VMEM((cap, D_MODEL), F32),
            pltpu.VMEM((2, LN_CHUNK, D_MODEL), F32),
            pltpu.SemaphoreType.DMA(()),
            pltpu.SemaphoreType.DMA((2,)),
        ],
    )
    return pl.pallas_call(
        functools.partial(_moe_kernel, seq=seq, cap=cap),
        grid_spec=grid_spec,
        out_shape=jax.ShapeDtypeStruct((T, D_MODEL), F32),
        compiler_params=pltpu.CompilerParams(
            dimension_semantics=("arbitrary", "arbitrary"),
            vmem_limit_bytes=VMEM_LIMIT_MOE),
        name="expert_ffn_combine_ln",
    )(idx_flat, x1t, gate_t, w_gate_up, w_down, ln_g, ln_b)


HALO = SUBLANES


def _gelu_tanh(x):
    return 0.5 * x * (1.0 + jnp.tanh(0.7978845608028654 * (x + 0.044715 * (x * x * x))))


def _lru_in_kernel(x_ref, xp_ref, xn_ref, w_ref, cw_ref, cb_ref, wd_in, gg_ref, xc_ref, wd_out, *, nts):
    wd_out[...] = wd_in[0].astype(BF16)
    i = pl.program_id(0)
    tm = x_ref.shape[0]
    x = x_ref[...].astype(BF16)
    xp = xp_ref[...].astype(BF16)
    xn = xn_ref[...].astype(BF16)
    first = (i % nts) == 0
    last = (i % nts) == nts - 1
    for k, gate in enumerate(_dot_by_columns(x, w_ref, 0, D_MODEL)):
        gg_ref[:, k * MXU_COLS:(k + 1) * MXU_COLS] = _gelu_tanh(gate)
    n_ext = tm + 2 * HALO
    for k in range(D_MODEL // MXU_COLS):
        cols = slice(k * MXU_COLS, (k + 1) * MXU_COLS)
        xb, = _dot_by_columns(x, w_ref, D_MODEL + k * MXU_COLS, MXU_COLS)
        xbp, = _dot_by_columns(xp, w_ref, D_MODEL + k * MXU_COLS, MXU_COLS)
        xbn, = _dot_by_columns(xn, w_ref, D_MODEL + k * MXU_COLS, MXU_COLS)
        ext = jnp.concatenate([jnp.where(first, 0.0, xbp), xb, jnp.where(last, 0.0, xbn)], axis=0)
        xc = cb_ref[:, cols]
        for tap in range(CONV_WIDTH):
            shift = (2 - tap) % n_ext
            src = ext if shift == 0 else pltpu.roll(ext, shift, axis=0)
            xc = xc + cw_ref[tap:tap + 1, cols] * src[HALO:HALO + tm, :]
        xc_ref[:, cols] = xc


def _lru_in(x2d, w_in, conv_w, conv_b, w_down, layer, seq):
    T = x2d.shape[0]
    tm = min(ROW_TILE, seq)
    nts = seq // tm
    hb = tm // HALO
    nblk = T // HALO
    w_spec_in, w_spec_out, w_shape = _weight_cast(w_down, layer, T // tm, lambda i: i)
    return pl.pallas_call(
        functools.partial(_lru_in_kernel, nts=nts),
        grid=(T // tm,),
        in_specs=[
            pl.BlockSpec((tm, D_MODEL), lambda i: (i, 0)),
            pl.BlockSpec((HALO, D_MODEL), lambda i: (jnp.maximum(i * hb - 1, 0), 0)),
            pl.BlockSpec((HALO, D_MODEL), lambda i: (jnp.minimum((i + 1) * hb, nblk - 1), 0)),
            pl.BlockSpec((D_MODEL, 2 * D_MODEL), lambda i: (0, 0)),
            pl.BlockSpec((CONV_WIDTH, D_MODEL), lambda i: (0, 0)),
            pl.BlockSpec((1, D_MODEL), lambda i: (0, 0)),
            w_spec_in,
        ],
        out_specs=[
            pl.BlockSpec((tm, D_MODEL), lambda i: (i, 0)),
            pl.BlockSpec((tm, D_MODEL), lambda i: (i, 0)),
            w_spec_out,
        ],
        out_shape=[
            jax.ShapeDtypeStruct((T, D_MODEL), F32),
            jax.ShapeDtypeStruct((T, D_MODEL), F32),
            w_shape,
        ],
        compiler_params=pltpu.CompilerParams(dimension_semantics=("arbitrary",)),
        name="lru_in_conv",
    )(x2d, x2d, x2d, w_in, conv_w, conv_b, w_down)


SCAN_TM = 128
CHAIN_PITCH = SCAN_TM + SUBLANES
PAIRS = D_MODEL // (2 * LANES)


def _lru_chain_scan(xc_ref, wg_ref, br_ref, bi_ref, lam_ref, carry, abuf, ubuf, hbuf, *, reverse, seq):
    i = pl.program_id(0)
    nseq, tm = xc_ref.shape[0], xc_ref.shape[1]
    assert 2 * nseq == SUBLANES and tm == SCAN_TM
    nt = seq // tm
    tile = (nt - 1 - i) if reverse else i

    @pl.when(i == 0)
    def _():
        carry[...] = jnp.zeros_like(carry)

    half_nsp = (-0.5 * LRU_C) * jax.nn.softplus(-lam_ref[...])
    row = lax.broadcasted_iota(jnp.int32, (tm, 1), 0)
    is_start = (tile * tm + row) == (seq - 1 if reverse else 0)
    xc_all = jnp.concatenate([xc_ref[b] for b in range(nseq)], axis=0)
    xcb = xc_all.astype(BF16)
    r_parts, i_parts = [], []
    for n in range(LRU_BLOCKS):
        res = jnp.dot(xcb[:, n * LRU_BLOCK_W:(n + 1) * LRU_BLOCK_W], wg_ref[n],
                      preferred_element_type=F32)
        r_parts.append(res[:, :LRU_BLOCK_W])
        i_parts.append(res[:, LRU_BLOCK_W:])
    r_pre = jnp.concatenate(r_parts, axis=1)
    i_pre = jnp.concatenate(i_parts, axis=1)
    for b in range(nseq):
        rows_b = slice(b * tm, (b + 1) * tm)
        xc = xc_all[rows_b]
        tr = jnp.tanh(r_pre[rows_b] + br_ref[...])
        ig = 0.5 * jnp.tanh(i_pre[rows_b] + bi_ref[...]) + 0.5
        log_a = tr * half_nsp + half_nsp
        a = jnp.exp(log_a)
        z = -jnp.tanh(log_a) * (1.0 + a * a)
        root = jnp.where(z > 0.0, z * lax.rsqrt(z), 0.0)
        u = jnp.where(is_start, 1.0, root) * ig * xc
        for lb in range(2 * PAIRS):
            rows = slice((2 * b + lb % 2) * CHAIN_PITCH, (2 * b + lb % 2) * CHAIN_PITCH + tm)
            abuf[lb // 2, rows, :] = a[:, lb * LANES:(lb + 1) * LANES]
            ubuf[lb // 2, rows, :] = u[:, lb * LANES:(lb + 1) * LANES]

    def step(k, hs):
        t = (tm - 1 - k) if reverse else k
        chains = pl.ds(t, SUBLANES, stride=CHAIN_PITCH)
        out = []
        for m in range(PAIRS):
            h = abuf[m, chains, :] * hs[m] + ubuf[m, chains, :]
            hbuf[m, chains, :] = h
            out.append(h)
        return tuple(out)

    hs = lax.fori_loop(0, tm, step, tuple(carry[m] for m in range(PAIRS)), unroll=8)
    for m in range(PAIRS):
        carry[m] = hs[m]


def _chain_tile(hbuf, b, tm):
    return jnp.concatenate(
        [hbuf[lb // 2, (2 * b + lb % 2) * CHAIN_PITCH:(2 * b + lb % 2) * CHAIN_PITCH + tm, :]
         for lb in range(2 * PAIRS)], axis=1)


def _lru_fwd_kernel(xc_ref, wg_ref, br_ref, bi_ref, lam_ref, h_ref, carry, abuf, ubuf, hbuf, *, seq):
    _lru_chain_scan(xc_ref, wg_ref, br_ref, bi_ref, lam_ref, carry, abuf, ubuf, hbuf,
                    reverse=False, seq=seq)
    for b in range(h_ref.shape[0]):
        h_ref[b] = _chain_tile(hbuf, b, h_ref.shape[1])


def _lru_bwd_kernel(xc_ref, wg_ref, br_ref, bi_ref, lam_ref, hf_ref, gg_ref, x_ref, wo_ref,
                    g_ref, b_ref, wr_ref, wgu_in, xo_ref, aff_ref, wgu_out, carry, abuf, ubuf, hbuf, *, seq):
    wgu_out[...] = wgu_in[0].astype(BF16)
    _lru_chain_scan(xc_ref, wg_ref, br_ref, bi_ref, lam_ref, carry, abuf, ubuf, hbuf,
                    reverse=True, seq=seq)
    nseq, tm = x_ref.shape[0], x_ref.shape[1]
    gated = [((hf_ref[b] + _chain_tile(hbuf, b, tm)) * gg_ref[b]).astype(BF16) for b in range(nseq)]
    y_all = jnp.dot(jnp.concatenate(gated, axis=0), wo_ref[...], preferred_element_type=F32)
    for b in range(nseq):
        xn = _layer_norm(ALPHA * x_ref[b] + y_all[b * tm:(b + 1) * tm], g_ref[...], b_ref[...])
        _store_token_layout(xo_ref.at[b], xn)
        aff_ref[b] = _router_affinity_t(xn, wr_ref[...])


def _lru_specs(batch, tile_map):
    const2 = lambda i: (0, 0)
    return [
        pl.BlockSpec((batch, SCAN_TM, D_MODEL), tile_map),
        pl.BlockSpec((LRU_BLOCKS, LRU_BLOCK_W, 2 * LRU_BLOCK_W), lambda i: (0, 0, 0)),
        pl.BlockSpec((1, D_MODEL), const2),
        pl.BlockSpec((1, D_MODEL), const2),
        pl.BlockSpec((1, D_MODEL), const2),
    ]


def _lru_scratch():
    buf = pltpu.VMEM((PAIRS, SUBLANES * CHAIN_PITCH, LANES), F32)
    return [pltpu.VMEM((PAIRS, SUBLANES, LANES), F32), buf, buf, buf]


def _lru_forward(xc3, wg, br, bi, lam):
    batch, seq, _ = xc3.shape
    tile_map = lambda i: (0, i, 0)
    nt = seq // SCAN_TM
    return pl.pallas_call(
        functools.partial(_lru_fwd_kernel, seq=seq),
        grid=(nt,),
        in_specs=_lru_specs(batch, tile_map),
        out_specs=pl.BlockSpec((batch, SCAN_TM, D_MODEL), tile_map),
        out_shape=jax.ShapeDtypeStruct((batch, seq, D_MODEL), F32),
        scratch_shapes=_lru_scratch(),
        compiler_params=pltpu.CompilerParams(dimension_semantics=("arbitrary",)),
        name="lru_forward_scan",
    )(xc3, wg, br, bi, lam)


def _lru_backward_out(xc3, wg, br, bi, lam, hf3, gg3, x3d, w_out, g, b, wr_t, w_gate_up, layer):
    batch, seq, _ = xc3.shape
    nt = seq // SCAN_TM
    tile_map = lambda i: (0, nt - 1 - i, 0)
    const2 = lambda i: (0, 0)
    tile = pl.BlockSpec((batch, SCAN_TM, D_MODEL), tile_map)
    w_in, w_out_spec, w_shape = _weight_cast(w_gate_up, layer, nt, lambda i: i)
    return pl.pallas_call(
        functools.partial(_lru_bwd_kernel, seq=seq),
        grid=(nt,),
        in_specs=_lru_specs(batch, tile_map) + [
            tile, tile, tile,
            pl.BlockSpec((D_MODEL, D_MODEL), const2),
            pl.BlockSpec((1, D_MODEL), const2),
            pl.BlockSpec((1, D_MODEL), const2),
            pl.BlockSpec((N_EXPERTS, D_MODEL), const2),
            w_in,
        ],
        out_specs=[
            pl.BlockSpec((batch, SCAN_TM * TOK_ROWS, LANES), tile_map),
            pl.BlockSpec((batch, N_EXPERTS, SCAN_TM), lambda i: (0, 0, nt - 1 - i)),
            w_out_spec,
        ],
        out_shape=[
            jax.ShapeDtypeStruct((batch, seq * TOK_ROWS, LANES), F32),
            jax.ShapeDtypeStruct((batch, N_EXPERTS, seq), F32),
            w_shape,
        ],
        scratch_shapes=_lru_scratch(),
        compiler_params=pltpu.CompilerParams(dimension_semantics=("arbitrary",)),
        name="lru_backward_scan_out_ln_router",
    )(xc3, wg, br, bi, lam, hf3, gg3, x3d, w_out, g, b, wr_t, w_gate_up)


def _rotary_tables(seq):
    pos = jnp.arange(seq, dtype=F32)
    inv_freq = ROPE_THETA ** (-jnp.arange(0, ROT_DIM, 2, dtype=F32) / ROT_DIM)
    ang = pos[:, None] * inv_freq[None, :]
    cos, sin = jnp.cos(ang), jnp.sin(ang)
    pad = HEAD_DIM - ROT_DIM
    cos_h = jnp.concatenate([cos, cos, jnp.ones((seq, pad), F32)], axis=1)
    sin_h = jnp.concatenate([-sin, sin, jnp.zeros((seq, pad), F32)], axis=1)
    reps = LANES // HEAD_DIM
    return jnp.tile(cos_h, (1, reps)), jnp.tile(sin_h, (1, reps))


def _moe_block(x1t, aff_t, w_gate_up, w_down, ln_g, ln_b, batch, seq):
    cap = CAPACITY_FACTOR * seq // N_EXPERTS
    idx, gate_t = _route(aff_t, cap)
    return _moe(x1t, idx, gate_t, w_gate_up, w_down, ln_g.reshape(1, -1), ln_b.reshape(1, -1), batch, seq)


def kernel(x, attn_w_qkv, attn_w_o, attn_sink, lru_w_in, lru_conv_w, lru_conv_b, lru_w_rgate,
           lru_b_rgate, lru_w_igate, lru_b_igate, lru_lambda, lru_w_out, moe_w_router, moe_w_gate_up,
           moe_w_down, ln_mix_g, ln_mix_b, ln_ffn_g, ln_ffn_b):
    batch, seq, _ = x.shape
    x2d = x.reshape(batch * seq, D_MODEL)
    row = lambda v: v.reshape(1, -1)

    cos_t, sin_t = _rotary_tables(seq)
    q, kv = _qkv_proj(x2d, attn_w_qkv[0].astype(BF16), cos_t, sin_t, seq)
    o, wgu0, wd0 = _attention(q, kv, attn_sink[0], moe_w_gate_up, moe_w_down, 0, batch, seq)
    x1, aff_t = _proj_ln(o, attn_w_o[0].astype(BF16), x2d, row(ln_mix_g[0]), row(ln_mix_b[0]),
                         moe_w_router[0].T.astype(BF16), batch, seq)
    x2 = _moe_block(x1, aff_t, wgu0, wd0, ln_ffn_g[0], ln_ffn_b[0], batch, seq)

    gg, xc, wd1 = _lru_in(x2, lru_w_in[0].astype(BF16), lru_conv_w[0], row(lru_conv_b[0]), moe_w_down, 1, seq)
    wg = (0.5 * jnp.concatenate([lru_w_rgate[0], lru_w_igate[0]], axis=-1)).astype(BF16)
    half_row = lambda v: 0.5 * v.reshape(1, -1)
    per_seq = lambda t: t.reshape(batch, seq, D_MODEL)
    xc3 = per_seq(xc)
    hf3 = _lru_forward(xc3, wg[0], half_row(lru_b_rgate[0, 0]), half_row(lru_b_igate[0, 0]),
                       row(lru_lambda[0, 0]))
    x3, aff_t, wgu1 = _lru_backward_out(xc3, wg[1], half_row(lru_b_rgate[0, 1]), half_row(lru_b_igate[0, 1]),
                                        row(lru_lambda[0, 1]), hf3, per_seq(gg), per_seq(x2),
                                        lru_w_out[0].astype(BF16), row(ln_mix_g[1]), row(ln_mix_b[1]),
                                        moe_w_router[1].T.astype(BF16), moe_w_gate_up, 1)
    x3 = x3.reshape(batch * seq * TOK_ROWS, LANES)
    x4 = _moe_block(x3, aff_t, wgu1, wd1, ln_ffn_g[1], ln_ffn_b[1], batch, seq)
    return x4.reshape(batch, seq, D_MODEL)
```

```python
import functools

import jax
import jax.numpy as jnp
from jax import lax
from jax.experimental import pallas as pl
from jax.experimental.pallas import tpu as pltpu

F32 = jnp.float32
BF16 = jnp.bfloat16

D_MODEL = 1024
HEAD_DIM = 64
N_Q_HEADS = 16
N_KV_HEADS = 4
GQA_GROUP = 4
WINDOW = 128
BLOCK = 128
ROT_DIM = 16
ROPE_THETA = 500000.0
LRU_BLOCKS = 4
LRU_BLOCK_W = 256
CONV_WIDTH = 4
LRU_C = 8.0
N_EXPERTS = 16
EXPERT_FF = 1024
CAPACITY_FACTOR = 2
LN_EPS = 1e-5
DEPTH = 2
ALPHA = (2 * DEPTH) ** 0.25

LANES = 128
SUBLANES = 8
MXU_COLS = 256
ROW_TILE = 1024
TOK_ROWS = D_MODEL // LANES
VMEM_LIMIT_MOE = 61 * 1024 * 1024


def _layer_norm(v, g, b):
    mu = jnp.mean(v, axis=-1, keepdims=True)
    vc = v - mu
    var = jnp.mean(vc * vc, axis=-1, keepdims=True)
    return vc * lax.rsqrt(var + LN_EPS) * g + b


def _store_token_layout(dst_ref, val):
    n = val.shape[0]
    for s in range(TOK_ROWS):
        dst_ref[pl.ds(s, n, stride=TOK_ROWS), :] = val[:, s * LANES:(s + 1) * LANES]


def _load_token_layout(src_ref, first_token, n):
    base = first_token * TOK_ROWS
    return jnp.concatenate(
        [src_ref[pl.ds(base + s, n, stride=TOK_ROWS), :] for s in range(TOK_ROWS)], axis=1)


def _router_affinity_t(xn, wr_t):
    logits_t = lax.dot_general(wr_t, xn.astype(BF16), (((1,), (1,)), ((), ())),
                               preferred_element_type=F32)
    m = jnp.max(logits_t, axis=0, keepdims=True)
    p = jnp.exp(logits_t - m)
    return p / jnp.sum(p, axis=0, keepdims=True)


def _dot_by_columns(lhs, w_ref, first_col, n_cols):
    return [jnp.dot(lhs, w_ref[:, c:c + MXU_COLS], preferred_element_type=F32)
            for c in range(first_col, first_col + n_cols, MXU_COLS)]


def _weight_cast(w, layer, n_steps, step_of):
    e, r, c = w.shape[1:]
    if n_steps <= e:
        blk = (e // n_steps, r, c)
        in_map = lambda *g: (layer, step_of(*g), 0, 0)
        out_map = lambda *g: (step_of(*g), 0, 0)
    else:
        k = n_steps // e
        blk = (1, r // k, c)
        in_map = lambda *g: (layer, step_of(*g) // k, step_of(*g) % k, 0)
        out_map = lambda *g: (step_of(*g) // k, step_of(*g) % k, 0)
    return (pl.BlockSpec((1,) + blk, in_map), pl.BlockSpec(blk, out_map),
            jax.ShapeDtypeStruct((e, r, c), BF16))


LOG2E = 1.4426950408889634


def _qkv_kernel(x_ref, w_ref, cos_ref, sin_ref, q_ref, kv_ref):
    x = x_ref[...].astype(BF16)
    acc = jnp.dot(x, w_ref[...], preferred_element_type=F32)
    cosb = cos_ref[...]
    sinb = sin_ref[...]
    lane = lax.broadcasted_iota(jnp.int32, cosb.shape, 1)
    low = (lane % HEAD_DIM) < (ROT_DIM // 2)
    nq = N_Q_HEADS * HEAD_DIM
    nk = N_KV_HEADS * HEAD_DIM

    def rot(t):
        up = pltpu.roll(t, LANES - ROT_DIM // 2, axis=1)
        dn = pltpu.roll(t, ROT_DIM // 2, axis=1)
        return t * cosb + jnp.where(low, up, dn) * sinb

    qs = [rot(acc[:, c * LANES:(c + 1) * LANES]) * (LOG2E * HEAD_DIM ** -0.5) for c in range(nq // LANES)]
    q_ref[...] = jnp.concatenate(qs, axis=1).astype(BF16)
    ks = [rot(acc[:, nq + c * LANES:nq + (c + 1) * LANES]) for c in range(nk // LANES)]
    vs = [acc[:, nq + nk + c * LANES:nq + nk + (c + 1) * LANES] for c in range(nk // LANES)]
    swap = lambda t: pltpu.roll(t, HEAD_DIM, axis=1)
    kv_ref[...] = jnp.concatenate(
        ks + [swap(t) for t in ks] + vs + [swap(t) for t in vs], axis=1).astype(BF16)


def _qkv_proj(x2d, w_qkv, cos_t, sin_t, seq):
    T = x2d.shape[0]
    tm = min(ROW_TILE, seq)
    nq = N_Q_HEADS * HEAD_DIM
    nk = N_KV_HEADS * HEAD_DIM
    nts = seq // tm
    return pl.pallas_call(
        _qkv_kernel,
        grid=(T // tm,),
        in_specs=[
            pl.BlockSpec((tm, D_MODEL), lambda i: (i, 0)),
            pl.BlockSpec((D_MODEL, nq + 2 * nk), lambda i: (0, 0)),
            pl.BlockSpec((tm, LANES), lambda i: (i % nts, 0)),
            pl.BlockSpec((tm, LANES), lambda i: (i % nts, 0)),
        ],
        out_specs=[
            pl.BlockSpec((tm, nq), lambda i: (i, 0)),
            pl.BlockSpec((tm, 4 * nk), lambda i: (i, 0)),
        ],
        out_shape=[
            jax.ShapeDtypeStruct((T, nq), BF16),
            jax.ShapeDtypeStruct((T, 4 * nk), BF16),
        ],
        compiler_params=pltpu.CompilerParams(dimension_semantics=("arbitrary",)),
        name="qkv_proj",
    )(x2d, w_qkv, cos_t, sin_t)


QBLKS = 8


def _attn_kernel(sink_ref, q_ref, kvp_ref, kvc_ref, kvn_ref, wgu_in, wd_in, o_ref, wgu_out, wd_out, *, seq):
    i = pl.program_id(1)
    nk = N_KV_HEADS * HEAD_DIM
    kv = jnp.concatenate([kvp_ref[...], kvc_ref[...], kvn_ref[...]], axis=0)
    band = 3 * BLOCK
    lo = lax.broadcasted_iota(jnp.int32, (1, LANES), 1) < HEAD_DIM
    zero = jnp.zeros((), BF16)
    k_lo, k_hi, v_dup_t = [], [], []
    for h in range(N_KV_HEADS):
        grp, half = h // 2, h % 2

        def block(base, swapped):
            col = base + (nk if swapped else 0) + grp * LANES
            return kv[:, col:col + LANES]
        k_lo.append(jnp.where(lo, block(0, half == 1), zero))
        k_hi.append(jnp.where(lo, zero, block(0, half == 0)))
        v_dup = jnp.where(lo, block(2 * nk, half == 1), block(2 * nk, half == 0))
        v_dup_t.append(v_dup.astype(F32).T.astype(BF16))

    key = lax.broadcasted_iota(jnp.int32, (BLOCK, BLOCK), 0)
    qry = lax.broadcasted_iota(jnp.int32, (BLOCK, BLOCK), 1)
    feat_lo = lax.broadcasted_iota(jnp.int32, (LANES, 1), 0) < HEAD_DIM
    units = [(j, h) for j in range(QBLKS) for h in range(N_KV_HEADS)]

    def scores_t(j, h):
        rows = slice(j * BLOCK, (j + 1) * BLOCK)
        qp = jnp.concatenate([q_ref[rows, (2 * h) * LANES:(2 * h + 1) * LANES],
                              q_ref[rows, (2 * h + 1) * LANES:(2 * h + 2) * LANES]], axis=0)
        keys = jnp.concatenate([k_lo[h][j * BLOCK:j * BLOCK + band], k_hi[h][j * BLOCK:j * BLOCK + band]],
                               axis=0)
        return lax.dot_general(keys, qp, (((1,), (1,)), ((), ())), preferred_element_type=F32)

    def cast_share(u):
        for src, dst in ((wgu_in, wgu_out), (wd_in, wd_out)):
            rows = dst.shape[1] // len(units)
            dst[:, u * rows:(u + 1) * rows, :] = src[0, :, u * rows:(u + 1) * rows, :].astype(BF16)

    s_next = scores_t(*units[0])
    for u, (j, h) in enumerate(units):
        cast_share(u)
        s = s_next
        if u + 1 < len(units):
            s_next = scores_t(*units[u + 1])
        n = i * QBLKS + j
        ok_prev = jnp.concatenate([(key >= qry) & (n > 0)] * 2, axis=1)
        ok_next = jnp.concatenate([(key <= qry) & (n < seq // BLOCK - 1)] * 2, axis=1)
        vt = v_dup_t[h][:, j * BLOCK:j * BLOCK + band]
        halves = []
        for par in range(2):
            sp = s[par * band:(par + 1) * band, :]
            sp = jnp.concatenate([jnp.where(ok_prev, sp[:BLOCK], -jnp.inf), sp[BLOCK:2 * BLOCK],
                                  jnp.where(ok_next, sp[2 * BLOCK:], -jnp.inf)], axis=0)
            sink_row = jnp.concatenate(
                [jnp.full((1, BLOCK), sink_ref[h * GQA_GROUP + g] * LOG2E, F32) for g in (par, par + 2)],
                axis=1)
            m = jnp.maximum(jnp.max(sp, axis=0, keepdims=True), sink_row)
            p = jnp.exp2(sp - m)
            denom = jnp.sum(p, axis=0, keepdims=True) + jnp.exp2(sink_row - m)
            halves.append(jnp.dot(vt, p.astype(BF16), preferred_element_type=F32) / denom)
        out_t = jnp.where(feat_lo, halves[0], halves[1])
        o_ref[j * BLOCK:(j + 1) * BLOCK, (2 * h) * LANES:(2 * h + 2) * LANES] = jnp.concatenate(
            [out_t[:, :BLOCK].T, out_t[:, BLOCK:].T], axis=1).astype(BF16)


def _attention(q, kv, sink, w_gate_up, w_down, layer, batch, seq):
    T = q.shape[0]
    qt = QBLKS * BLOCK
    nq = seq // qt
    nb = seq // BLOCK
    wkv = kv.shape[1]
    cur = lambda b, i: (b * nq + i, 0)
    prev = lambda b, i: (b * nb + jnp.maximum(i * QBLKS - 1, 0), 0)
    nxt = lambda b, i: (b * nb + jnp.minimum(i * QBLKS + QBLKS, nb - 1), 0)
    casts = [_weight_cast(w, layer, batch * nq, lambda b, i: b * nq + i) for w in (w_gate_up, w_down)]
    w_in, w_out, w_shape = ([c[k] for c in casts] for k in range(3))
    return pl.pallas_call(
        functools.partial(_attn_kernel, seq=seq),
        grid=(batch, nq),
        in_specs=[
            pl.BlockSpec(memory_space=pltpu.SMEM),
            pl.BlockSpec((qt, D_MODEL), cur),
            pl.BlockSpec((BLOCK, wkv), prev),
            pl.BlockSpec((qt, wkv), cur),
            pl.BlockSpec((BLOCK, wkv), nxt),
        ] + w_in,
        out_specs=[pl.BlockSpec((qt, D_MODEL), cur)] + w_out,
        out_shape=[jax.ShapeDtypeStruct((T, D_MODEL), BF16)] + w_shape,
        compiler_params=pltpu.CompilerParams(dimension_semantics=("arbitrary", "arbitrary")),
        name="swa_attention",
    )(sink, q, kv, kv, kv, w_gate_up, w_down)


def _proj_ln_kernel(a_ref, w_ref, x_ref, g_ref, b_ref, wr_ref, xo_ref, aff_ref):
    y = jnp.dot(a_ref[...], w_ref[...], preferred_element_type=F32)
    xn = _layer_norm(ALPHA * x_ref[...] + y, g_ref[...], b_ref[...])
    _store_token_layout(xo_ref, xn)
    aff_ref[0] = _router_affinity_t(xn, wr_ref[...])


def _proj_ln(a, w, x2d, g, b, wr_t, batch, seq):
    T = x2d.shape[0]
    tm = min(ROW_TILE, seq)
    nts = seq // tm
    return pl.pallas_call(
        _proj_ln_kernel,
        grid=(T // tm,),
        in_specs=[
            pl.BlockSpec((tm, D_MODEL), lambda i: (i, 0)),
            pl.BlockSpec((D_MODEL, D_MODEL), lambda i: (0, 0)),
            pl.BlockSpec((tm, D_MODEL), lambda i: (i, 0)),
            pl.BlockSpec((1, D_MODEL), lambda i: (0, 0)),
            pl.BlockSpec((1, D_MODEL), lambda i: (0, 0)),
            pl.BlockSpec((N_EXPERTS, D_MODEL), lambda i: (0, 0)),
        ],
        out_specs=[
            pl.BlockSpec((tm * TOK_ROWS, LANES), lambda i: (i, 0)),
            pl.BlockSpec((1, N_EXPERTS, tm), lambda i: (i // nts, 0, i % nts)),
        ],
        out_shape=[
            jax.ShapeDtypeStruct((T * TOK_ROWS, LANES), F32),
            jax.ShapeDtypeStruct((batch, N_EXPERTS, seq), F32),
        ],
        compiler_params=pltpu.CompilerParams(dimension_semantics=("arbitrary",)),
        name="proj_ln_router",
    )(a, w, x2d, g, b, wr_t)


SLOT_BLOCK = 64


def _lane_prefix(mask, tri, dst_ref):
    seq = mask.shape[1]
    off = jnp.zeros((mask.shape[0], 1), F32)
    mb = mask.astype(BF16)
    ends = []
    for c in range(seq // LANES):
        res = jnp.dot(mb[:, c * LANES:(c + 1) * LANES], tri, preferred_element_type=F32) + off
        dst_ref[:, c * LANES:(c + 1) * LANES] = res
        off = res[:, LANES - 1:LANES]
        ends.append(off)
    return ends


def _select_kernel(aff_ref, pos_ref, bnd_ref, *, cap):
    aff = aff_ref[0]
    as_float = lambda bits: pltpu.bitcast(bits, F32)
    rows = aff.shape[0]
    cur = jnp.zeros((rows, 1), jnp.int32)
    for bit in range(30, -1, -1):
        cand = cur | (1 << bit)
        cnt = jnp.sum((aff >= as_float(cand)).astype(jnp.int32), axis=1, keepdims=True)
        cur = jnp.where(cnt >= cap, cand, cur)
    gt = aff >= as_float(cur + 1)
    eq = (aff >= as_float(cur)) & jnp.logical_not(gt)
    n_gt = jnp.sum(gt.astype(jnp.int32), axis=1, keepdims=True)
    ties_taken = (cap - n_gt).astype(F32)
    ri = lax.broadcasted_iota(jnp.int32, (LANES, LANES), 0)
    ci = lax.broadcasted_iota(jnp.int32, (LANES, LANES), 1)
    tri = (ri <= ci).astype(BF16)
    _lane_prefix(eq, tri, pos_ref.at[0])
    sel = gt | (eq & (pos_ref[0] <= ties_taken))
    ends = _lane_prefix(sel, tri, pos_ref.at[0])
    pos_ref[0] = jnp.where(sel, pos_ref[0], 0.0)
    cols = []
    for sb in range(cap // SLOT_BLOCK):
        for target in (sb * SLOT_BLOCK + 1, (sb + 1) * SLOT_BLOCK):
            blk = jnp.zeros((rows, 1), jnp.int32)
            for end in ends[:-1]:
                blk = blk + (end < target).astype(jnp.int32)
            cols.append(blk)
    bnd_ref[0] = jnp.concatenate(cols, axis=1)


def _compact_kernel(bnd_ref, pos_ref, aff_ref, idx_ref, gate_ref, ptok, pgate, *, cap):
    b = pl.program_id(0)
    n_sb = cap // SLOT_BLOCK
    groups = SLOT_BLOCK // SUBLANES
    lane = lax.broadcasted_iota(jnp.int32, (SUBLANES, LANES), 1).astype(F32)
    sub1 = (lax.broadcasted_iota(jnp.int32, (SUBLANES, 1), 0) + 1).astype(F32)
    lane_e = lax.broadcasted_iota(jnp.int32, (cap, N_EXPERTS), 1)
    idx_ref[0] = jnp.zeros((cap, N_EXPERTS), jnp.int32)
    gate_ref[0] = jnp.zeros((cap, N_EXPERTS), F32)

    def expert(e, carry):
        def slot_block(sb, carry2):
            base = ((b * N_EXPERTS + e) * n_sb + sb) * 2
            slot0 = jnp.asarray(sb * SLOT_BLOCK, F32)

            def lane_block(c, accs):
                off = pl.multiple_of(c * LANES, LANES)
                p = jnp.broadcast_to(pos_ref[0, e:e + 1, pl.ds(off, LANES)], (SUBLANES, LANES))
                a = jnp.broadcast_to(aff_ref[0, e:e + 1, pl.ds(off, LANES)], (SUBLANES, LANES))
                tokv = lane + jnp.asarray(c * LANES, F32)
                out = []
                for g in range(groups):
                    hit = p == (slot0 + (g * SUBLANES) + sub1)
                    out.append(jnp.where(hit, tokv, accs[2 * g]))
                    out.append(jnp.where(hit, a, accs[2 * g + 1]))
                return tuple(out)

            zeros = tuple(jnp.zeros((SUBLANES, LANES), F32) for _ in range(2 * groups))
            accs = lax.fori_loop(bnd_ref[base], bnd_ref[base + 1] + 1, lane_block, zeros)
            for g in range(groups):
                rows = pl.ds(pl.multiple_of(sb * SLOT_BLOCK + g * SUBLANES, SUBLANES), SUBLANES)
                ptok[rows, :] = accs[2 * g]
                pgate[rows, :] = accs[2 * g + 1]
            return carry2

        lax.fori_loop(0, n_sb, slot_block, 0)
        tok_col = jnp.sum(ptok[...], axis=1, keepdims=True).astype(jnp.int32)
        gate_col = jnp.sum(pgate[...], axis=1, keepdims=True)
        idx_ref[0] = jnp.where(lane_e == e, tok_col, idx_ref[0])
        gate_ref[0] = jnp.where(lane_e == e, gate_col, gate_ref[0])
        return carry

    for e in range(N_EXPERTS):
        expert(e, 0)


def _route(aff_t, cap):
    batch, _, seq = aff_t.shape
    n_bnd = 2 * (cap // SLOT_BLOCK)
    seq_spec = pl.BlockSpec((1, N_EXPERTS, seq), lambda b, *_: (b, 0, 0))
    rows = batch * N_EXPERTS
    pos, bnd = pl.pallas_call(
        functools.partial(_select_kernel, cap=cap),
        grid=(1,),
        in_specs=[pl.BlockSpec((1, rows, seq), lambda i: (0, 0, 0))],
        out_specs=[pl.BlockSpec((1, rows, seq), lambda i: (0, 0, 0)),
                   pl.BlockSpec((1, rows, n_bnd), lambda i: (0, 0, 0))],
        out_shape=[
            jax.ShapeDtypeStruct((1, rows, seq), F32),
            jax.ShapeDtypeStruct((1, rows, n_bnd), jnp.int32),
        ],
        compiler_params=pltpu.CompilerParams(dimension_semantics=("arbitrary",)),
        name="expert_choice_select",
    )(aff_t.reshape(1, rows, seq))
    pos = pos.reshape(batch, N_EXPERTS, seq)
    slot_spec = pl.BlockSpec((1, cap, N_EXPERTS), lambda b, *_: (b, 0, 0))
    idx_t, gate_t = pl.pallas_call(
        functools.partial(_compact_kernel, cap=cap),
        grid_spec=pltpu.PrefetchScalarGridSpec(
            num_scalar_prefetch=1,
            grid=(batch,),
            in_specs=[seq_spec, seq_spec],
            out_specs=[slot_spec, slot_spec],
            scratch_shapes=[pltpu.VMEM((cap, LANES), F32), pltpu.VMEM((cap, LANES), F32)],
        ),
        out_shape=[
            jax.ShapeDtypeStruct((batch, cap, N_EXPERTS), jnp.int32),
            jax.ShapeDtypeStruct((batch, cap, N_EXPERTS), F32),
        ],
        compiler_params=pltpu.CompilerParams(dimension_semantics=("arbitrary",)),
        name="expert_choice_compact",
    )(bnd.reshape(-1), pos, aff_t)
    return jnp.swapaxes(idx_t, 1, 2), gate_t


LN_CHUNK = 256
SCATTER_GROUP = 1


def _moe_kernel(idx_ref, x1t_hbm, gate_ref, wgu_ref, wd_ref, lng_ref, lnb_ref, out_hbm,
                x1v, acc, xs3, y3, xs2d, yacc, stage, sem_in, sem_out, *, seq, cap):
    b = pl.program_id(0)
    e = pl.program_id(1)
    tok_rows = seq * TOK_ROWS
    last_e = N_EXPERTS - 1

    def tok_slice(t):
        return pl.ds(pl.multiple_of(t * TOK_ROWS, TOK_ROWS), TOK_ROWS)

    def gather_rows(expert):
        base = (b * N_EXPERTS + expert) * cap
        for i in range(cap):
            xs3[i * TOK_ROWS:(i + 1) * TOK_ROWS, :] = x1v[tok_slice(idx_ref[base + i]), :]

    def scatter_rows(expert):
        base = (b * N_EXPERTS + expert) * cap
        for g0 in range(0, cap, SCATTER_GROUP):
            toks = [idx_ref[base + g0 + k] for k in range(SCATTER_GROUP)]
            vals = [acc[tok_slice(toks[k]), :] + y3[(g0 + k) * TOK_ROWS:(g0 + k + 1) * TOK_ROWS, :]
                    for k in range(SCATTER_GROUP)]
            for k in range(SCATTER_GROUP):
                acc[tok_slice(toks[k]), :] = vals[k]

    def load_sequence(seq_idx):
        return pltpu.make_async_copy(x1t_hbm.at[pl.ds(seq_idx * tok_rows, tok_rows)], x1v, sem_in)

    def init_accumulator():
        acc[...] = ALPHA * x1v[...]

    @pl.when((b == 0) & (e == 0))
    def _first_sequence():
        load_sequence(0).start()
        load_sequence(0).wait()
        init_accumulator()

    @pl.when(e == 0)
    def _start_sequence():
        yacc[...] = jnp.zeros_like(yacc)

        def body(i, carry):
            xs3[tok_slice(i), :] = x1v[tok_slice(idx_ref[b * N_EXPERTS * cap + i]), :]
            return carry
        lax.fori_loop(0, cap, body, 0, unroll=8)
        xs2d[0] = _load_token_layout(xs3, 0, cap).astype(BF16)

    def gate_column(expert):
        gates = gate_ref[0]
        lane_e = lax.broadcasted_iota(jnp.int32, gates.shape, 1)
        return jnp.sum(jnp.where(lane_e == expert, gates, 0.0), axis=1, keepdims=True)

    prev = jnp.maximum(e - 1, 0)
    _store_token_layout(y3, yacc[...] * gate_column(prev))
    scatter_rows(prev)
    gather_rows(jnp.minimum(e + 1, last_e))
    gu = jnp.dot(xs2d[e % 2], wgu_ref[0], preferred_element_type=F32)
    gate_h, up_h = gu[:, :EXPERT_FF], gu[:, EXPERT_FF:]
    act = (gate_h * jax.nn.sigmoid(gate_h) * up_h).astype(BF16)
    yacc[...] = jnp.dot(act, wd_ref[0], preferred_element_type=F32)
    xs2d[(e + 1) % 2] = _load_token_layout(xs3, 0, cap).astype(BF16)

    @pl.when(e == last_e)
    def _finalize():
        _store_token_layout(y3, yacc[...] * gate_column(last_e))

        def body(gi, carry):
            base = (b * N_EXPERTS + last_e) * cap + gi * SUBLANES
            toks = [idx_ref[base + k] for k in range(SUBLANES)]
            vals = [acc[tok_slice(toks[k]), :] + y3[tok_slice(gi * SUBLANES + k), :]
                    for k in range(SUBLANES)]
            for k in range(SUBLANES):
                acc[tok_slice(toks[k]), :] = vals[k]
            return carry
        lax.fori_loop(0, cap // SUBLANES, body, 0)

        n_seq = pl.num_programs(0)

        @pl.when(b + 1 < n_seq)
        def _():
            load_sequence(b + 1).start()

        n_chunks = seq // LN_CHUNK

        def out_copy(c, slot):
            return pltpu.make_async_copy(
                stage.at[slot], out_hbm.at[pl.ds(b * seq + c * LN_CHUNK, LN_CHUNK)], sem_out.at[slot])

        def ln_body(c, carry):
            slot = c % 2
            first = c * LN_CHUNK
            res = _layer_norm(_load_token_layout(acc, first, LN_CHUNK), lng_ref[...], lnb_ref[...])

            @pl.when(c >= 2)
            def _():
                out_copy(c - 2, slot).wait()
            stage[slot] = res
            out_copy(c, slot).start()
            return carry
        lax.fori_loop(0, n_chunks, ln_body, 0)
        out_copy(n_chunks - 2, (n_chunks - 2) % 2).wait()
        out_copy(n_chunks - 1, (n_chunks - 1) % 2).wait()

        @pl.when(b + 1 < n_seq)
        def _():
            load_sequence(b + 1).wait()
            init_accumulator()


def _moe(x1t, idx, gate_t, w_gate_up, w_down, ln_g, ln_b, batch, seq):
    cap = idx.shape[2]
    T = batch * seq
    idx_flat = idx.reshape(-1)
    grid_spec = pltpu.PrefetchScalarGridSpec(
        num_scalar_prefetch=1,
        grid=(batch, N_EXPERTS),
        in_specs=[
            pl.BlockSpec(memory_space=pl.ANY),
            pl.BlockSpec((1, cap, N_EXPERTS), lambda b, e, idx: (b, 0, 0)),
            pl.BlockSpec((1, D_MODEL, 2 * EXPERT_FF), lambda b, e, idx: (e, 0, 0)),
            pl.BlockSpec((1, EXPERT_FF, D_MODEL), lambda b, e, idx: (e, 0, 0)),
            pl.BlockSpec((1, D_MODEL), lambda b, e, idx: (0, 0)),
            pl.BlockSpec((1, D_MODEL), lambda b, e, idx: (0, 0)),
        ],
        out_specs=pl.BlockSpec(memory_space=pl.ANY),
        scratch_shapes=[
            pltpu.VMEM((seq * TOK_ROWS, LANES), F32),
            pltpu.VMEM((seq * TOK_ROWS, LANES), F32),
            pltpu.VMEM((cap * TOK_ROWS, LANES), F32),
            pltpu.VMEM((cap * TOK_ROWS, LANES), F32),
            pltpu.VMEM((2, cap, D_MODEL), BF16),
            pltpu.VMEM((cap, D_MODEL), F32),
            pltpu.VMEM((2, LN_CHUNK, D_MODEL), F32),
            pltpu.SemaphoreType.DMA(()),
            pltpu.SemaphoreType.DMA((2,)),
        ],
    )
    return pl.pallas_call(
        functools.partial(_moe_kernel, seq=seq, cap=cap),
        grid_spec=grid_spec,
        out_shape=jax.ShapeDtypeStruct((T, D_MODEL), F32),
        compiler_params=pltpu.CompilerParams(
            dimension_semantics=("arbitrary", "arbitrary"),
            vmem_limit_bytes=VMEM_LIMIT_MOE),
        name="expert_ffn_combine_ln",
    )(idx_flat, x1t, gate_t, w_gate_up, w_down, ln_g, ln_b)


HALO = SUBLANES


def _gelu_tanh(x):
    return 0.5 * x * (1.0 + jnp.tanh(0.7978845608028654 * (x + 0.044715 * (x * x * x))))


def _lru_in_kernel(x_ref, xp_ref, xn_ref, w_ref, cw_ref, cb_ref, wd_in, gg_ref, xc_ref, wd_out, *, nts):
    wd_out[...] = wd_in[0].astype(BF16)
    i = pl.program_id(0)
    tm = x_ref.shape[0]
    x = x_ref[...].astype(BF16)
    xp = xp_ref[...].astype(BF16)
    xn = xn_ref[...].astype(BF16)
    first = (i % nts) == 0
    last = (i % nts) == nts - 1
    for k, gate in enumerate(_dot_by_columns(x, w_ref, 0, D_MODEL)):
        gg_ref[:, k * MXU_COLS:(k + 1) * MXU_COLS] = _gelu_tanh(gate)
    n_ext = tm + 2 * HALO
    for k in range(D_MODEL // MXU_COLS):
        cols = slice(k * MXU_COLS, (k + 1) * MXU_COLS)
        xb, = _dot_by_columns(x, w_ref, D_MODEL + k * MXU_COLS, MXU_COLS)
        xbp, = _dot_by_columns(xp, w_ref, D_MODEL + k * MXU_COLS, MXU_COLS)
        xbn, = _dot_by_columns(xn, w_ref, D_MODEL + k * MXU_COLS, MXU_COLS)
        ext = jnp.concatenate([jnp.where(first, 0.0, xbp), xb, jnp.where(last, 0.0, xbn)], axis=0)
        xc = cb_ref[:, cols]
        for tap in range(CONV_WIDTH):
            shift = (2 - tap) % n_ext
            src = ext if shift == 0 else pltpu.roll(ext, shift, axis=0)
            xc = xc + cw_ref[tap:tap + 1, cols] * src[HALO:HALO + tm, :]
        xc_ref[:, cols] = xc


def _lru_in(x2d, w_in, conv_w, conv_b, w_down, layer, seq):
    T = x2d.shape[0]
    tm = min(ROW_TILE, seq)
    nts = seq // tm
    hb = tm // HALO
    nblk = T // HALO
    w_spec_in, w_spec_out, w_shape = _weight_cast(w_down, layer, T // tm, lambda i: i)
    return pl.pallas_call(
        functools.partial(_lru_in_kernel, nts=nts),
        grid=(T // tm,),
        in_specs=[
            pl.BlockSpec((tm, D_MODEL), lambda i: (i, 0)),
            pl.BlockSpec((HALO, D_MODEL), lambda i: (jnp.maximum(i * hb - 1, 0), 0)),
            pl.BlockSpec((HALO, D_MODEL), lambda i: (jnp.minimum((i + 1) * hb, nblk - 1), 0)),
            pl.BlockSpec((D_MODEL, 2 * D_MODEL), lambda i: (0, 0)),
            pl.BlockSpec((CONV_WIDTH, D_MODEL), lambda i: (0, 0)),
            pl.BlockSpec((1, D_MODEL), lambda i: (0, 0)),
            w_spec_in,
        ],
        out_specs=[
            pl.BlockSpec((tm, D_MODEL), lambda i: (i, 0)),
            pl.BlockSpec((tm, D_MODEL), lambda i: (i, 0)),
            w_spec_out,
        ],
        out_shape=[
            jax.ShapeDtypeStruct((T, D_MODEL), F32),
            jax.ShapeDtypeStruct((T, D_MODEL), F32),
            w_shape,
        ],
        compiler_params=pltpu.CompilerParams(dimension_semantics=("arbitrary",)),
        name="lru_in_conv",
    )(x2d, x2d, x2d, w_in, conv_w, conv_b, w_down)


SCAN_TM = 128
CHAIN_PITCH = SCAN_TM + SUBLANES
PAIRS = D_MODEL // (2 * LANES)


def _lru_chain_scan(xc_ref, wg_ref, br_ref, bi_ref, lam_ref, carry, abuf, ubuf, hbuf, *, reverse, seq):
    i = pl.program_id(0)
    nseq, tm = xc_ref.shape[0], xc_ref.shape[1]
    assert 2 * nseq == SUBLANES and tm == SCAN_TM
    nt = seq // tm
    tile = (nt - 1 - i) if reverse else i

    @pl.when(i == 0)
    def _():
        carry[...] = jnp.zeros_like(carry)

    half_nsp = (-0.5 * LRU_C) * jax.nn.softplus(-lam_ref[...])
    row = lax.broadcasted_iota(jnp.int32, (tm, 1), 0)
    is_start = (tile * tm + row) == (seq - 1 if reverse else 0)
    xc_all = jnp.concatenate([xc_ref[b] for b in range(nseq)], axis=0)
    xcb = xc_all.astype(BF16)
    r_parts, i_parts = [], []
    for n in range(LRU_BLOCKS):
        res = jnp.dot(xcb[:, n * LRU_BLOCK_W:(n + 1) * LRU_BLOCK_W], wg_ref[n],
                      preferred_element_type=F32)
        r_parts.append(res[:, :LRU_BLOCK_W])
        i_parts.append(res[:, LRU_BLOCK_W:])
    r_pre = jnp.concatenate(r_parts, axis=1)
    i_pre = jnp.concatenate(i_parts, axis=1)
    for b in range(nseq):
        rows_b = slice(b * tm, (b + 1) * tm)
        xc = xc_all[rows_b]
        tr = jnp.tanh(r_pre[rows_b] + br_ref[...])
        ig = 0.5 * jnp.tanh(i_pre[rows_b] + bi_ref[...]) + 0.5
        log_a = tr * half_nsp + half_nsp
        a = jnp.exp(log_a)
        z = -jnp.tanh(log_a) * (1.0 + a * a)
        root = jnp.where(z > 0.0, z * lax.rsqrt(z), 0.0)
        u = jnp.where(is_start, 1.0, root) * ig * xc
        for lb in range(2 * PAIRS):
            rows = slice((2 * b + lb % 2) * CHAIN_PITCH, (2 * b + lb % 2) * CHAIN_PITCH + tm)
            abuf[lb // 2, rows, :] = a[:, lb * LANES:(lb + 1) * LANES]
            ubuf[lb // 2, rows, :] = u[:, lb * LANES:(lb + 1) * LANES]

    def step(k, hs):
        t = (tm - 1 - k) if reverse else k
        chains = pl.ds(t, SUBLANES, stride=CHAIN_PITCH)
        out = []
        for m in range(PAIRS):
            h = abuf[m, chains, :] * hs[m] + ubuf[m, chains, :]
            hbuf[m, chains, :] = h
            out.append(h)
        return tuple(out)

    hs = lax.fori_loop(0, tm, step, tuple(carry[m] for m in range(PAIRS)), unroll=8)
    for m in range(PAIRS):
        carry[m] = hs[m]


def _chain_tile(hbuf, b, tm):
    return jnp.concatenate(
        [hbuf[lb // 2, (2 * b + lb % 2) * CHAIN_PITCH:(2 * b + lb % 2) * CHAIN_PITCH + tm, :]
         for lb in range(2 * PAIRS)], axis=1)


def _lru_fwd_kernel(xc_ref, wg_ref, br_ref, bi_ref, lam_ref, h_ref, carry, abuf, ubuf, hbuf, *, seq):
    _lru_chain_scan(xc_ref, wg_ref, br_ref, bi_ref, lam_ref, carry, abuf, ubuf, hbuf,
                    reverse=False, seq=seq)
    for b in range(h_ref.shape[0]):
        h_ref[b] = _chain_tile(hbuf, b, h_ref.shape[1])


def _lru_bwd_kernel(xc_ref, wg_ref, br_ref, bi_ref, lam_ref, hf_ref, gg_ref, x_ref, wo_ref,
                    g_ref, b_ref, wr_ref, wgu_in, xo_ref, aff_ref, wgu_out, carry, abuf, ubuf, hbuf, *, seq):
    wgu_out[...] = wgu_in[0].astype(BF16)
    _lru_chain_scan(xc_ref, wg_ref, br_ref, bi_ref, lam_ref, carry, abuf, ubuf, hbuf,
                    reverse=True, seq=seq)
    nseq, tm = x_ref.shape[0], x_ref.shape[1]
    gated = [((hf_ref[b] + _chain_tile(hbuf, b, tm)) * gg_ref[b]).astype(BF16) for b in range(nseq)]
    y_all = jnp.dot(jnp.concatenate(gated, axis=0), wo_ref[...], preferred_element_type=F32)
    for b in range(nseq):
        xn = _layer_norm(ALPHA * x_ref[b] + y_all[b * tm:(b + 1) * tm], g_ref[...], b_ref[...])
        _store_token_layout(xo_ref.at[b], xn)
        aff_ref[b] = _router_affinity_t(xn, wr_ref[...])


def _lru_specs(batch, tile_map):
    const2 = lambda i: (0, 0)
    return [
        pl.BlockSpec((batch, SCAN_TM, D_MODEL), tile_map),
        pl.BlockSpec((LRU_BLOCKS, LRU_BLOCK_W, 2 * LRU_BLOCK_W), lambda i: (0, 0, 0)),
        pl.BlockSpec((1, D_MODEL), const2),
        pl.BlockSpec((1, D_MODEL), const2),
        pl.BlockSpec((1, D_MODEL), const2),
    ]


def _lru_scratch():
    buf = pltpu.VMEM((PAIRS, SUBLANES * CHAIN_PITCH, LANES), F32)
    return [pltpu.VMEM((PAIRS, SUBLANES, LANES), F32), buf, buf, buf]


def _lru_forward(xc3, wg, br, bi, lam):
    batch, seq, _ = xc3.shape
    tile_map = lambda i: (0, i, 0)
    nt = seq // SCAN_TM
    return pl.pallas_call(
        functools.partial(_lru_fwd_kernel, seq=seq),
        grid=(nt,),
        in_specs=_lru_specs(batch, tile_map),
        out_specs=pl.BlockSpec((batch, SCAN_TM, D_MODEL), tile_map),
        out_shape=jax.ShapeDtypeStruct((batch, seq, D_MODEL), F32),
        scratch_shapes=_lru_scratch(),
        compiler_params=pltpu.CompilerParams(dimension_semantics=("arbitrary",)),
        name="lru_forward_scan",
    )(xc3, wg, br, bi, lam)


def _lru_backward_out(xc3, wg, br, bi, lam, hf3, gg3, x3d, w_out, g, b, wr_t, w_gate_up, layer):
    batch, seq, _ = xc3.shape
    nt = seq // SCAN_TM
    tile_map = lambda i: (0, nt - 1 - i, 0)
    const2 = lambda i: (0, 0)
    tile = pl.BlockSpec((batch, SCAN_TM, D_MODEL), tile_map)
    w_in, w_out_spec, w_shape = _weight_cast(w_gate_up, layer, nt, lambda i: i)
    return pl.pallas_call(
        functools.partial(_lru_bwd_kernel, seq=seq),
        grid=(nt,),
        in_specs=_lru_specs(batch, tile_map) + [
            tile, tile, tile,
            pl.BlockSpec((D_MODEL, D_MODEL), const2),
            pl.BlockSpec((1, D_MODEL), const2),
            pl.BlockSpec((1, D_MODEL), const2),
            pl.BlockSpec((N_EXPERTS, D_MODEL), const2),
            w_in,
        ],
        out_specs=[
            pl.BlockSpec((batch, SCAN_TM * TOK_ROWS, LANES), tile_map),
            pl.BlockSpec((batch, N_EXPERTS, SCAN_TM), lambda i: (0, 0, nt - 1 - i)),
            w_out_spec,
        ],
        out_shape=[
            jax.ShapeDtypeStruct((batch, seq * TOK_ROWS, LANES), F32),
            jax.ShapeDtypeStruct((batch, N_EXPERTS, seq), F32),
            w_shape,
        ],
        scratch_shapes=_lru_scratch(),
        compiler_params=pltpu.CompilerParams(dimension_semantics=("arbitrary",)),
        name="lru_backward_scan_out_ln_router",
    )(xc3, wg, br, bi, lam, hf3, gg3, x3d, w_out, g, b, wr_t, w_gate_up)


def _rotary_tables(seq):
    pos = jnp.arange(seq, dtype=F32)
    inv_freq = ROPE_THETA ** (-jnp.arange(0, ROT_DIM, 2, dtype=F32) / ROT_DIM)
    ang = pos[:, None] * inv_freq[None, :]
    cos, sin = jnp.cos(ang), jnp.sin(ang)
    pad = HEAD_DIM - ROT_DIM
    cos_h = jnp.concatenate([cos, cos, jnp.ones((seq, pad), F32)], axis=1)
    sin_h = jnp.concatenate([-sin, sin, jnp.zeros((seq, pad), F32)], axis=1)
    reps = LANES // HEAD_DIM
    return jnp.tile(cos_h, (1, reps)), jnp.tile(sin_h, (1, reps))


def _moe_block(x1t, aff_t, w_gate_up, w_down, ln_g, ln_b, batch, seq):
    cap = CAPACITY_FACTOR * seq // N_EXPERTS
    idx, gate_t = _route(aff_t, cap)
    return _moe(x1t, idx, gate_t, w_gate_up, w_down, ln_g.reshape(1, -1), ln_b.reshape(1, -1), batch, seq)


def kernel(x, attn_w_qkv, attn_w_o, attn_sink, lru_w_in, lru_conv_w, lru_conv_b, lru_w_rgate,
           lru_b_rgate, lru_w_igate, lru_b_igate, lru_lambda, lru_w_out, moe_w_router, moe_w_gate_up,
           moe_w_down, ln_mix_g, ln_mix_b, ln_ffn_g, ln_ffn_b):
    batch, seq, _ = x.shape
    x2d = x.reshape(batch * seq, D_MODEL)
    row = lambda v: v.reshape(1, -1)

    cos_t, sin_t = _rotary_tables(seq)
    q, kv = _qkv_proj(x2d, attn_w_qkv[0].astype(BF16), cos_t, sin_t, seq)
    o, wgu0, wd0 = _attention(q, kv, attn_sink[0], moe_w_gate_up, moe_w_down, 0, batch, seq)
    x1, aff_t = _proj_ln(o, attn_w_o[0].astype(BF16), x2d, row(ln_mix_g[0]), row(ln_mix_b[0]),
                         moe_w_router[0].T.astype(BF16), batch, seq)
    x2 = _moe_block(x1, aff_t, wgu0, wd0, ln_ffn_g[0], ln_ffn_b[0], batch, seq)

    gg, xc, wd1 = _lru_in(x2, lru_w_in[0].astype(BF16), lru_conv_w[0], row(lru_conv_b[0]), moe_w_down, 1, seq)
    wg = (0.5 * jnp.concatenate([lru_w_rgate[0], lru_w_igate[0]], axis=-1)).astype(BF16)
    half_row = lambda v: 0.5 * v.reshape(1, -1)
    per_seq = lambda t: t.reshape(batch, seq, D_MODEL)
    xc3 = per_seq(xc)
    hf3 = _lru_forward(xc3, wg[0], half_row(lru_b_rgate[0, 0]), half_row(lru_b_igate[0, 0]),
                       row(lru_lambda[0, 0]))
    x3, aff_t, wgu1 = _lru_backward_out(xc3, wg[1], half_row(lru_b_rgate[0, 1]), half_row(lru_b_igate[0, 1]),
                                        row(lru_lambda[0, 1]), hf3, per_seq(gg), per_seq(x2),
                                        lru_w_out[0].astype(BF16), row(ln_mix_g[1]), row(ln_mix_b[1]),
                                        moe_w_router[1].T.astype(BF16), moe_w_gate_up, 1)
    x3 = x3.reshape(batch * seq * TOK_ROWS, LANES)
    x4 = _moe_block(x3, aff_t, wgu1, wd1, ln_ffn_g[1], ln_ffn_b[1], batch, seq)
    return x4.reshape(batch, seq, D_MODEL)
```
